```python
import jax, jax.numpy as jnp
from jax import lax
import numpy as np

D_MODEL = 1024
BATCH = 4
SEQ = 8192
DEPTH = 1
DEC_BATCH = 32
DEC_SEQ = 16
PAST_LEN = 4096

CHUNK = 64
D_MIX = D_MODEL
GLA_WIDTH = D_MIX // 2
GLA_HEADS = 4
GLA_DV = GLA_WIDTH // GLA_HEADS
GLA_DK = GLA_DV // 2
GLA_LOWRANK = 16
GLA_GATE_NORM = 16.0
GMLP_WIDTH = D_MIX - GLA_WIDTH
GMLP_GROUPS = 4
GMLP_GC = GMLP_WIDTH // GMLP_GROUPS
GMLP_BLOCK = 128
N_EXPERTS = 32
TOP_K = 4
D_FF = D_MODEL
SWIGLU_LIMIT = 7.0
SWIGLU_ALPHA = 1.702
EPS = 1e-6
N_MOD = 6
IN_COLS = 2 * GLA_HEADS * GLA_DK + GLA_WIDTH + GLA_LOWRANK + GLA_WIDTH + 2 * GMLP_WIDTH

kernel_name = "hymba_gla_gmlp_moe_adaln_stream"


def _rmsnorm(x, g):
    xf = x.astype(jnp.float32)
    y = xf * lax.rsqrt(jnp.mean(xf * xf, axis=-1, keepdims=True) + EPS)
    return (y * g).astype(x.dtype)


def _layernorm(x, g, b):
    xf = x.astype(jnp.float32)
    mu = jnp.mean(xf, axis=-1, keepdims=True)
    var = jnp.mean(jnp.square(xf - mu), axis=-1, keepdims=True)
    return ((xf - mu) * lax.rsqrt(var + EPS) * g + b).astype(x.dtype)


def _gla(q, k, v, g, s0):
    bsz, t_len, nh, dk = q.shape
    dv = v.shape[-1]
    c = min(CHUNK, t_len)
    n = t_len // c

    def to_blocks(a):
        return a.astype(jnp.float32).reshape(bsz, n, c, nh, a.shape[-1]).transpose(1, 0, 3, 2, 4)

    qc, kc, vc, gc = to_blocks(q), to_blocks(k), to_blocks(v), to_blocks(g)
    causal = jnp.tril(jnp.ones((c, c), dtype=bool))

    def step(S, inp):
        qb, kb, vb, gb = inp
        G = jnp.cumsum(gb, axis=2)
        o_inter = jnp.einsum('bhtd,bhde->bhte', qb * jnp.exp(G), S)
        diff = G[:, :, :, None, :] - G[:, :, None, :, :]
        decay = jnp.exp(jnp.where(causal[:, :, None], diff, -jnp.inf))
        attn = jnp.einsum('bhtd,bhsd,bhtsd->bhts', qb, kb, decay)
        o = o_inter + jnp.einsum('bhts,bhse->bhte', attn, vb)
        G_last = G[:, :, -1:, :]
        S_new = jnp.exp(G_last[:, :, 0, :, None]) * S + jnp.einsum(
            'bhsd,bhse->bhde', kb * jnp.exp(G_last - G), vb)
        return S_new, o

    S, o = lax.scan(step, s0.astype(jnp.float32), (qc, kc, vc, gc))
    o = o.transpose(1, 0, 3, 2, 4).reshape(bsz, t_len, nh, dv)
    return o.astype(v.dtype), S.astype(s0.dtype)


def _gmlp(u, v, ln_g, ln_b, w_s, b_s):
    bsz, t_len, _ = v.shape
    L = min(GMLP_BLOCK, t_len)
    n = t_len // L
    vn = _layernorm(v, ln_g, ln_b)
    vb = vn.reshape(bsz, n, L, GMLP_GROUPS, GMLP_GC)
    pos = jnp.arange(L)
    mask = (pos[None, :] // CHUNK) <= (pos[:, None] // CHUNK)
    w = jnp.where(mask[None], w_s[:, :L, :L], 0.0)
    mixed = jnp.einsum('gij,bnjgc->bnigc', w, vb) + b_s[:, :L].T[None, None, :, :, None]
    return u * mixed.reshape(bsz, t_len, GMLP_WIDTH).astype(u.dtype), vn


def _moe(h, w_router, b_router, w_gate, b_gate, w_up, b_up, w_down, b_down):
    shape = h.shape
    t = h.reshape(-1, D_MODEL)
    logits = (t @ w_router + b_router).astype(jnp.float32)
    top_l, top_i = lax.top_k(logits, TOP_K)
    top_w = jax.nn.softmax(top_l, axis=-1)
    combine = jnp.sum(jax.nn.one_hot(top_i, N_EXPERTS, dtype=jnp.float32) * top_w[..., None], axis=1)
    out = jnp.zeros(t.shape, jnp.float32)
    for e in range(N_EXPERTS):
        gate = jnp.minimum(t @ w_gate[e] + b_gate[e], SWIGLU_LIMIT)
        up = jnp.clip(t @ w_up[e] + b_up[e], -SWIGLU_LIMIT, SWIGLU_LIMIT)
        act = (up + 1.0) * gate * jax.nn.sigmoid(SWIGLU_ALPHA * gate)
        out = out + combine[:, e:e + 1] * (act @ w_down[e] + b_down[e])
    return out.astype(h.dtype).reshape(shape)


def _layer(x, c, s0, w_ada, b_ada, norm1, w_in, w_gk, b_gk, gla_norm, gmlp_ln_g, gmlp_ln_b,
           gmlp_w_s, gmlp_b_s, w_out, norm2, w_router, b_router, w_gate, b_gate, w_up, b_up,
           w_down, b_down):
    bsz, t_len, _ = x.shape
    mod = jax.nn.silu(c) @ w_ada + b_ada
    sh1, sc1, g1, sh2, sc2, g2 = [m[:, None, :] for m in jnp.split(mod, N_MOD, axis=-1)]
    h = _rmsnorm(x, norm1) * (1.0 + sc1) + sh1
    proj = h @ w_in
    sizes = (GLA_HEADS * GLA_DK, GLA_HEADS * GLA_DK, GLA_WIDTH, GLA_LOWRANK, GLA_WIDTH,
             GMLP_WIDTH, GMLP_WIDTH)
    cuts = [int(s) for s in np.cumsum(sizes)[:-1]]
    q, k, v, lr, r, u, vg = jnp.split(proj, cuts, axis=-1)
    q = q.reshape(bsz, t_len, GLA_HEADS, GLA_DK) * (GLA_DK ** -0.5)
    k = k.reshape(bsz, t_len, GLA_HEADS, GLA_DK)
    v = v.reshape(bsz, t_len, GLA_HEADS, GLA_DV)
    gk = (jax.nn.log_sigmoid((lr @ w_gk + b_gk).astype(jnp.float32)) / GLA_GATE_NORM
          ).reshape(bsz, t_len, GLA_HEADS, GLA_DK)
    o, s_new = _gla(q, k, v, gk, s0)
    o = _rmsnorm(o, gla_norm) * jax.nn.silu(r.reshape(bsz, t_len, GLA_HEADS, GLA_DV))
    gla_out = o.reshape(bsz, t_len, GLA_WIDTH)
    gm_out, v_rows = _gmlp(jax.nn.gelu(u, approximate=False), jax.nn.gelu(vg, approximate=False),
                           gmlp_ln_g, gmlp_ln_b, gmlp_w_s, gmlp_b_s)
    x = x + g1 * (jnp.concatenate([gla_out, gm_out], axis=-1) @ w_out)
    h2 = _rmsnorm(x, norm2) * (1.0 + sc2) + sh2
    x = x + g2 * _moe(h2, w_router, b_router, w_gate, b_gate, w_up, b_up, w_down, b_down)
    return x, s_new, v_rows


def setup_inputs(seed: int = 0) -> dict:
    key = jax.random.key(seed)
    ks = jax.random.split(key, 32)
    f = jnp.float32
    nrm = lambda k, shape, s: jax.random.normal(k, shape, f) * s
    D = D_MODEL
    return {
        'x_prompt': nrm(ks[0], (BATCH, SEQ, D), 1.0),
        'x_sample': nrm(ks[1], (DEC_BATCH, DEC_SEQ, D), 1.0),
        'state_gla': nrm(ks[2], (DEPTH, DEC_BATCH, GLA_HEADS, GLA_DK, GLA_DV), 1.0),
        'c_prompt': nrm(ks[3], (BATCH, D), 1.0),
        'c_sample': nrm(ks[4], (DEC_BATCH, D), 1.0),
        'w_ada': nrm(ks[5], (DEPTH, D, N_MOD * D), 0.5 * D ** -0.5),
        'b_ada': nrm(ks[6], (DEPTH, N_MOD * D), 0.01),
        'norm1': 1.0 + nrm(ks[7], (DEPTH, D), 0.02),
        'w_in': nrm(ks[8], (DEPTH, D, IN_COLS), D ** -0.5),
        'w_gk': nrm(ks[9], (DEPTH, GLA_LOWRANK, GLA_HEADS * GLA_DK), GLA_LOWRANK ** -0.5),
        'b_gk': nrm(ks[10], (DEPTH, GLA_HEADS * GLA_DK), 0.1),
        'gla_norm': 1.0 + nrm(ks[11], (DEPTH, GLA_DV), 0.02),
        'gmlp_ln_g': 1.0 + nrm(ks[12], (DEPTH, GMLP_WIDTH), 0.02),
        'gmlp_ln_b': nrm(ks[13], (DEPTH, GMLP_WIDTH), 0.02),
        'gmlp_w_s': nrm(ks[14], (DEPTH, GMLP_GROUPS, GMLP_BLOCK, GMLP_BLOCK), GMLP_BLOCK ** -0.5),
        'gmlp_b_s': 1.0 + nrm(ks[15], (DEPTH, GMLP_GROUPS, GMLP_BLOCK), 0.02),
        'w_out': nrm(ks[16], (DEPTH, D_MIX, D), D_MIX ** -0.5),
        'norm2': 1.0 + nrm(ks[17], (DEPTH, D), 0.02),
        'w_router': nrm(ks[18], (DEPTH, D, N_EXPERTS), D ** -0.5),
        'b_router': nrm(ks[19], (DEPTH, N_EXPERTS), 0.01),
        'w_gate': nrm(ks[20], (DEPTH, N_EXPERTS, D, D_FF), D ** -0.5),
        'b_gate': nrm(ks[21], (DEPTH, N_EXPERTS, D_FF), 0.01),
        'w_up': nrm(ks[22], (DEPTH, N_EXPERTS, D, D_FF), D ** -0.5),
        'b_up': nrm(ks[23], (DEPTH, N_EXPERTS, D_FF), 0.01),
        'w_down': nrm(ks[24], (DEPTH, N_EXPERTS, D_FF, D), D_FF ** -0.5),
        'b_down': nrm(ks[25], (DEPTH, N_EXPERTS, D), 0.01),
        'norm_f': 1.0 + nrm(ks[26], (D,), 0.02),
    }


def reference(x_prompt, x_sample, state_gla, c_prompt, c_sample, w_ada, b_ada, norm1, w_in, w_gk,
              b_gk, gla_norm, gmlp_ln_g, gmlp_ln_b, gmlp_w_s, gmlp_b_s, w_out, norm2, w_router,
              b_router, w_gate, b_gate, w_up, b_up, w_down, b_down, norm_f):
    yp, ys = x_prompt, x_sample
    sp_list, ss_list, vs_list = [], [], []
    for l in range(DEPTH):
        params = (w_ada[l], b_ada[l], norm1[l], w_in[l], w_gk[l], b_gk[l], gla_norm[l],
                  gmlp_ln_g[l], gmlp_ln_b[l], gmlp_w_s[l], gmlp_b_s[l], w_out[l], norm2[l],
                  w_router[l], b_router[l], w_gate[l], b_gate[l], w_up[l], b_up[l], w_down[l],
                  b_down[l])
        s0_prompt = jnp.zeros((yp.shape[0], GLA_HEADS, GLA_DK, GLA_DV), state_gla.dtype)
        yp, sp, _ = _layer(yp, c_prompt, s0_prompt, *params)
        ys, ss, vs = _layer(ys, c_sample, state_gla[l], *params)
        sp_list.append(sp)
        ss_list.append(ss)
        vs_list.append(vs)
    y_prompt = _rmsnorm(yp, norm_f)
    y_sample = _rmsnorm(ys, norm_f)
    state_gla_prompt = jnp.stack(sp_list, axis=0)
    state_gla_sample = jnp.stack(ss_list, axis=0)
    gmlp_v_sample = jnp.stack(vs_list, axis=0)
    return (y_prompt, y_sample, state_gla_prompt, state_gla_sample, gmlp_v_sample)
```

```python
import functools

import jax
import jax.numpy as jnp
from jax import lax
from jax.experimental import pallas as pl
from jax.experimental.pallas import tpu as pltpu

F32 = jnp.float32
BF16 = jnp.bfloat16
HIGHEST = lax.Precision.HIGHEST

CHUNK = 64
GLA_HEADS = 4
GLA_DK = 64
GLA_DV = 128
GLA_QK = GLA_HEADS * GLA_DK
GLA_WIDTH = GLA_HEADS * GLA_DV
GLA_LOWRANK = 16
GLA_GATE_NORM = 16.0
GMLP_WIDTH = 512
GMLP_GROUPS = 4
GMLP_GC = GMLP_WIDTH // GMLP_GROUPS
GMLP_BLOCK = 128
N_EXPERTS = 32
TOP_K = 4
SWIGLU_LIMIT = 7.0
SWIGLU_ALPHA = 1.702
EPS = 1e-6
N_MOD = 6

LANE = 128
SUBLANE = 8

Q0 = 0
K0 = Q0 + GLA_QK
V0 = K0 + GLA_QK
R0 = V0 + GLA_WIDTH
U0 = R0 + GLA_WIDTH
G0 = U0 + GMLP_WIDTH
PROJ_COLS = G0 + GMLP_WIDTH

ROW_BLOCK = 128
TOKEN_TILE = 512
EXPERT_TILE = 512
MOVE_TILE = 256
FF_CHUNK = 512
VMEM_LIMIT = 48 * 1024 * 1024


def _cparams(*sem):
    return pltpu.CompilerParams(dimension_semantics=sem, vmem_limit_bytes=VMEM_LIMIT)


def _dot(a, b, **kw):
    return jnp.dot(a, b, preferred_element_type=F32, **kw)


def _dot_nt(a, b):
    return lax.dot_general(a, b, (((1,), (1,)), ((), ())), preferred_element_type=F32)


def _gelu(x):
    return 0.5 * x * (1.0 + lax.erf(x * (0.5 ** 0.5)))


def _dot_tn(a, b):
    return lax.dot_general(a, b, (((0,), (0,)), ((), ())), preferred_element_type=F32)


def _ada_kernel(c_ref, w_ref, b_ref, o_ref):
    c = c_ref[...]
    s = c * jax.nn.sigmoid(c)
    o_ref[...] = _dot(s, w_ref[...], precision=HIGHEST) + b_ref[...]


def _ada(c, w_ada, b_ada):
    n, d = c.shape
    return pl.pallas_call(
        _ada_kernel,
        out_shape=jax.ShapeDtypeStruct((n, N_MOD * d), F32),
        grid=(N_MOD,),
        in_specs=[
            pl.BlockSpec((n, d), lambda j: (0, 0)),
            pl.BlockSpec((d, d), lambda j: (0, j)),
            pl.BlockSpec((1, d), lambda j: (0, j)),
        ],
        out_specs=pl.BlockSpec((n, d), lambda j: (0, j)),
        compiler_params=_cparams("arbitrary"),
        name="ada",
    )(c, w_ada, b_ada)


def _inproj_kernel(x_ref, sc_ref, sh_ref, n1_ref, wm_ref, wlr_ref, wgk_ref, bgk_ref, proj_ref, gk_ref):
    sb, rb, d = x_ref.shape
    x = x_ref[...]
    ms = jnp.mean(x * x, axis=-1, keepdims=True)
    h = x * lax.rsqrt(ms + EPS) * n1_ref[...]
    h = h * (1.0 + sc_ref[...]) + sh_ref[...]
    hb = h.reshape(sb * rb, d).astype(BF16)
    cw = 512
    for c in range(PROJ_COLS // cw):
        p = _dot(hb, wm_ref[:, c * cw:(c + 1) * cw])
        proj_ref[:, :, c * cw:(c + 1) * cw] = p.astype(BF16).reshape(sb, rb, cw)
    lr = _dot(hb, wlr_ref[...])
    gk = _dot(lr, wgk_ref[...], precision=HIGHEST) + bgk_ref[...]
    gk_ref[...] = gk.reshape(sb, rb, GLA_QK)


def _inproj(x, sc, sh, n1, wm, wlr, wgk, bgk, sb, rb):
    nseq, L, d = x.shape
    grid = (nseq // sb, L // rb)
    const = lambda s, t: (0, 0)
    return pl.pallas_call(
        _inproj_kernel,
        out_shape=(jax.ShapeDtypeStruct((nseq, L, PROJ_COLS), BF16),
                   jax.ShapeDtypeStruct((nseq, L, GLA_QK), F32)),
        grid=grid,
        in_specs=[
            pl.BlockSpec((sb, rb, d), lambda s, t: (s, t, 0)),
            pl.BlockSpec((sb, 1, d), lambda s, t: (s, 0, 0)),
            pl.BlockSpec((sb, 1, d), lambda s, t: (s, 0, 0)),
            pl.BlockSpec((1, d), const),
            pl.BlockSpec((d, PROJ_COLS), const),
            pl.BlockSpec((d, LANE), const),
            pl.BlockSpec((LANE, GLA_QK), const),
            pl.BlockSpec((1, GLA_QK), const),
        ],
        out_specs=(pl.BlockSpec((sb, rb, PROJ_COLS), lambda s, t: (s, t, 0)),
                   pl.BlockSpec((sb, rb, GLA_QK), lambda s, t: (s, t, 0))),
        compiler_params=_cparams("arbitrary", "arbitrary"),
        name="inproj",
    )(x, sc, sh, n1, wm, wlr, wgk, bgk)


def _head_masks():
    lane = lax.broadcasted_iota(jnp.int32, (1, GLA_QK), 1)
    return [(lane // GLA_DK) == h for h in range(GLA_HEADS)]


def _mixer_block(p, gkpre, states, c_len, gn, lng, lnb, wm_ref, bsb):
    rows = p.shape[0]
    n_chunks = rows // c_len
    chained = len(states) == 1
    hm = _head_masks()

    q = p[:, Q0:Q0 + GLA_QK].astype(F32)
    k = p[:, K0:K0 + GLA_QK].astype(F32)
    v = p[:, V0:V0 + GLA_WIDTH]
    r = p[:, R0:R0 + GLA_WIDTH].astype(F32)

    g = jax.nn.log_sigmoid(gkpre) / GLA_GATE_NORM
    ri = lax.broadcasted_iota(jnp.int32, (rows, rows), 0)
    ci = lax.broadcasted_iota(jnp.int32, (rows, rows), 1)
    tri = ((ci <= ri) & ((ci // c_len) == (ri // c_len))).astype(F32)
    G = _dot(tri, g, precision=HIGHEST)
    qe = (q * (GLA_DK ** -0.5)) * jnp.exp(G)
    ke = (k * jnp.exp(-G)).astype(BF16)

    ti = lax.broadcasted_iota(jnp.int32, (GLA_HEADS * c_len, c_len), 0) % c_len
    si = lax.broadcasted_iota(jnp.int32, (GLA_HEADS * c_len, c_len), 1)
    causal = si <= ti

    new_states = []
    o_rows = []
    st = states[0]
    for c in range(n_chunks):
        lo, hi = c * c_len, (c + 1) * c_len
        if not chained:
            st = states[c]
        qe_c = qe[lo:hi]
        q4 = jnp.concatenate([jnp.where(hm[h], qe_c, 0.0) for h in range(GLA_HEADS)], axis=0).astype(BF16)
        a = jnp.where(causal, _dot_nt(q4, ke[lo:hi]), 0.0).astype(BF16)
        o_inter = _dot_nt(q4, st.astype(BF16))
        v_c = v[lo:hi]
        heads = []
        for h in range(GLA_HEADS):
            o_h = o_inter[h * c_len:(h + 1) * c_len] + _dot(
                a[h * c_len:(h + 1) * c_len], v_c[:, h * GLA_DV:(h + 1) * GLA_DV])
            heads.append(o_h)
        o_rows.append(jnp.concatenate(heads, axis=1))
        g_last = G[hi - 1:hi]
        kd = (k[lo:hi] * jnp.exp(g_last - G[lo:hi])).astype(BF16)
        upd = _dot_tn(v_c, kd)
        st_new = jnp.exp(g_last) * st
        for h in range(GLA_HEADS):
            st_new = st_new + jnp.where(hm[h], upd[h * GLA_DV:(h + 1) * GLA_DV], 0.0)
        if chained:
            st = st_new
        else:
            new_states.append(st_new)
    if chained:
        new_states = [st]
    o = jnp.concatenate(o_rows, axis=0)

    gla = []
    for h in range(GLA_HEADS):
        o_h = o[:, h * GLA_DV:(h + 1) * GLA_DV]
        ms = jnp.mean(o_h * o_h, axis=-1, keepdims=True)
        r_h = r[:, h * GLA_DV:(h + 1) * GLA_DV]
        gla.append(o_h * lax.rsqrt(ms + EPS) * gn * (r_h * jax.nn.sigmoid(r_h)))

    u = _gelu(p[:, U0:U0 + GMLP_WIDTH].astype(F32))
    vv = _gelu(p[:, G0:G0 + GMLP_WIDTH].astype(F32))
    mu = jnp.mean(vv, axis=-1, keepdims=True)
    xc = vv - mu
    var = jnp.mean(xc * xc, axis=-1, keepdims=True)
    vn = xc * lax.rsqrt(var + EPS) * lng + lnb
    vnb = vn.astype(BF16)
    gm = []
    for gi in range(GMLP_GROUPS):
        sl = slice(gi * GMLP_GC, (gi + 1) * GMLP_GC)
        mixed = _dot(wm_ref[gi], vnb[:, sl]) + bsb[:, sl]
        gm.append(u[:, sl] * mixed)
    out = jnp.concatenate(gla + gm, axis=1)
    return out, new_states, vn


def _mixer_prompt_kernel(proj_ref, gk_ref, gn_ref, lng_ref, lnb_ref, wm_ref, bsb_ref, mix_ref, s_ref, st_scr):
    t = pl.program_id(1)
    nt = pl.num_programs(1)

    @pl.when(t == 0)
    def _():
        st_scr[...] = jnp.zeros_like(st_scr)

    n_sub = proj_ref.shape[1] // ROW_BLOCK

    def body(i, carry):
        r0 = pl.multiple_of(i * ROW_BLOCK, ROW_BLOCK)
        p = proj_ref[0, pl.ds(r0, ROW_BLOCK), :]
        gk = gk_ref[0, pl.ds(r0, ROW_BLOCK), :]
        out, sts, _ = _mixer_block(p, gk, [st_scr[...]], CHUNK, gn_ref[...], lng_ref[...], lnb_ref[...],
                                   wm_ref, bsb_ref[...])
        st_scr[...] = sts[0]
        mix_ref[0, pl.ds(r0, ROW_BLOCK), :] = out.astype(BF16)
        return carry

    lax.fori_loop(0, n_sub, body, 0)

    @pl.when(t == nt - 1)
    def _():
        s_ref[0] = st_scr[...].T.reshape(GLA_HEADS, GLA_DK, GLA_DV)


def _mixer_prompt(proj, gk, gn, lng, lnb, wm, bsb, tb):
    b, L, _ = proj.shape
    const2 = lambda s, t: (0, 0)
    return pl.pallas_call(
        _mixer_prompt_kernel,
        out_shape=(jax.ShapeDtypeStruct((b, L, 2 * GLA_WIDTH), BF16),
                   jax.ShapeDtypeStruct((b, GLA_HEADS, GLA_DK, GLA_DV), F32)),
        grid=(b, L // tb),
        in_specs=[
            pl.BlockSpec((1, tb, PROJ_COLS), lambda s, t: (s, t, 0)),
            pl.BlockSpec((1, tb, GLA_QK), lambda s, t: (s, t, 0)),
            pl.BlockSpec((1, GLA_DV), const2),
            pl.BlockSpec((1, GMLP_WIDTH), const2),
            pl.BlockSpec((1, GMLP_WIDTH), const2),
            pl.BlockSpec((GMLP_GROUPS, ROW_BLOCK, ROW_BLOCK), lambda s, t: (0, 0, 0)),
            pl.BlockSpec((ROW_BLOCK, GMLP_WIDTH), const2),
        ],
        out_specs=(pl.BlockSpec((1, tb, 2 * GLA_WIDTH), lambda s, t: (s, t, 0)),
                   pl.BlockSpec((1, GLA_HEADS, GLA_DK, GLA_DV), lambda s, t: (s, 0, 0, 0))),
        scratch_shapes=[pltpu.VMEM((GLA_DV, GLA_QK), F32)],
        compiler_params=_cparams("arbitrary", "arbitrary"),
        name="mixer_prompt",
    )(proj, gk, gn, lng, lnb, wm, bsb)


def _mixer_sample_kernel(proj_ref, gk_ref, s0_ref, gn_ref, lng_ref, lnb_ref, wm_ref, bsb_ref,
                         mix_ref, s_ref, vn_ref):
    sb, rb, _ = proj_ref.shape
    p = proj_ref[...].reshape(sb * rb, PROJ_COLS)
    gk = gk_ref[...].reshape(sb * rb, GLA_QK)
    states = [s0_ref[i].reshape(GLA_QK, GLA_DV).T for i in range(sb)]
    out, sts, vn = _mixer_block(p, gk, states, rb, gn_ref[...], lng_ref[...], lnb_ref[...], wm_ref, bsb_ref[...])
    mix_ref[...] = out.astype(BF16).reshape(sb, rb, 2 * GLA_WIDTH)
    vn_ref[...] = vn.reshape(sb, rb, GMLP_WIDTH)
    for i in range(sb):
        s_ref[i] = sts[i].T.reshape(GLA_HEADS, GLA_DK, GLA_DV)


def _mixer_sample(proj, gk, s0, gn, lng, lnb, wm, bsb):
    n, L, _ = proj.shape
    sb = ROW_BLOCK // L
    const2 = lambda s: (0, 0)
    return pl.pallas_call(
        _mixer_sample_kernel,
        out_shape=(jax.ShapeDtypeStruct((n, L, 2 * GLA_WIDTH), BF16),
                   jax.ShapeDtypeStruct((n, GLA_HEADS, GLA_DK, GLA_DV), F32),
                   jax.ShapeDtypeStruct((n, L, GMLP_WIDTH), F32)),
        grid=(n // sb,),
        in_specs=[
            pl.BlockSpec((sb, L, PROJ_COLS), lambda s: (s, 0, 0)),
            pl.BlockSpec((sb, L, GLA_QK), lambda s: (s, 0, 0)),
            pl.BlockSpec((sb, GLA_HEADS, GLA_DK, GLA_DV), lambda s: (s, 0, 0, 0)),
            pl.BlockSpec((1, GLA_DV), const2),
            pl.BlockSpec((1, GMLP_WIDTH), const2),
            pl.BlockSpec((1, GMLP_WIDTH), const2),
            pl.BlockSpec((GMLP_GROUPS, ROW_BLOCK, ROW_BLOCK), lambda s: (0, 0, 0)),
            pl.BlockSpec((ROW_BLOCK, GMLP_WIDTH), const2),
        ],
        out_specs=(pl.BlockSpec((sb, L, 2 * GLA_WIDTH), lambda s: (s, 0, 0)),
                   pl.BlockSpec((sb, GLA_HEADS, GLA_DK, GLA_DV), lambda s: (s, 0, 0, 0)),
                   pl.BlockSpec((sb, L, GMLP_WIDTH), lambda s: (s, 0, 0))),
        compiler_params=_cparams("arbitrary"),
        name="mixer_sample",
    )(proj, gk, s0, gn, lng, lnb, wm, bsb)


def _outproj_kernel(mix_ref, x_ref, g1_ref, sc_ref, sh_ref, n2_ref, wo_ref, wr_ref, br_ref, upper_ref, cnt0_ref,
                    xmid_ref, h2_ref, topi_ref, topw_ref, rank_ref, cnt_ref, carry_scr):
    sb, rb, d = x_ref.shape
    tm = sb * rb
    first = (pl.program_id(0) == 0) & (pl.program_id(1) == 0)

    @pl.when(first)
    def _():
        carry_scr[...] = cnt0_ref[...]

    y = _dot(mix_ref[...].reshape(tm, d), wo_ref[...])
    xm = x_ref[...] + g1_ref[...] * y.reshape(sb, rb, d)
    xmid_ref[...] = xm
    ms = jnp.mean(xm * xm, axis=-1, keepdims=True)
    h2 = xm * lax.rsqrt(ms + EPS) * n2_ref[...]
    h2 = (h2 * (1.0 + sc_ref[...]) + sh_ref[...]).reshape(tm, d)
    for j in range(d // LANE):
        h2_ref[pl.ds(j, tm, stride=SUBLANE), :] = h2[:, j * LANE:(j + 1) * LANE]

    logits = _dot(h2, wr_ref[...], precision=HIGHEST) + br_ref[...]
    l = logits.T[:N_EXPERTS]
    eid = lax.broadcasted_iota(jnp.int32, (N_EXPERTS, tm), 0).astype(F32)
    top_l, top_i, sel = [], [], []
    for _ in range(TOP_K):
        m = jnp.max(l, axis=0, keepdims=True)
        idx = jnp.min(jnp.where(l == m, eid, float(N_EXPERTS)), axis=0, keepdims=True)
        hit = eid == idx
        top_l.append(m)
        top_i.append(idx)
        sel.append(hit)
        l = jnp.where(hit, -jnp.inf, l)
    ex = [jnp.exp(t - top_l[0]) for t in top_l]
    den = ex[0] + ex[1] + ex[2] + ex[3]
    topw_ref[...] = jnp.concatenate([e / den for e in ex], axis=0)
    topi_ref[...] = jnp.concatenate(top_i, axis=0).astype(jnp.int32)

    chosen = (sel[0] | sel[1] | sel[2] | sel[3])
    cb = jnp.where(chosen, 1.0, 0.0)
    before = _dot(cb.astype(BF16), upper_ref[...])
    base = before + carry_scr[:, 0:1]
    rank_ref[...] = jnp.concatenate(
        [jnp.sum(jnp.where(s, base, 0.0), axis=0, keepdims=True) for s in sel], axis=0).astype(jnp.int32)
    carry_scr[...] = carry_scr[...] + jnp.sum(cb, axis=1, keepdims=True)
    cnt_ref[...] = carry_scr[...]


def _outproj(mix, x, g1, sc2, sh2, n2, wo, wr, br, upper, cnt0, sb, rb):
    nseq, L, d = x.shape
    tm = sb * rb
    nt = L // rb
    T = nseq * L
    grid = (nseq // sb, nt)
    const = lambda s, t: (0, 0)
    tok = lambda s, t: (0, s * nt + t)
    return pl.pallas_call(
        _outproj_kernel,
        out_shape=(jax.ShapeDtypeStruct((nseq, L, d), F32),
                   jax.ShapeDtypeStruct((T * SUBLANE, LANE), F32),
                   jax.ShapeDtypeStruct((TOP_K, T), jnp.int32),
                   jax.ShapeDtypeStruct((TOP_K, T), F32),
                   jax.ShapeDtypeStruct((TOP_K, T), jnp.int32),
                   jax.ShapeDtypeStruct((N_EXPERTS, LANE), F32)),
        grid=grid,
        in_specs=[
            pl.BlockSpec((sb, rb, d), lambda s, t: (s, t, 0)),
            pl.BlockSpec((sb, rb, d), lambda s, t: (s, t, 0)),
            pl.BlockSpec((sb, 1, d), lambda s, t: (s, 0, 0)),
            pl.BlockSpec((sb, 1, d), lambda s, t: (s, 0, 0)),
            pl.BlockSpec((sb, 1, d), lambda s, t: (s, 0, 0)),
            pl.BlockSpec((1, d), const),
            pl.BlockSpec((d, d), const),
            pl.BlockSpec((d, LANE), const),
            pl.BlockSpec((1, LANE), const),
            pl.BlockSpec((tm, tm), const),
            pl.BlockSpec((N_EXPERTS, LANE), const),
        ],
        out_specs=(pl.BlockSpec((sb, rb, d), lambda s, t: (s, t, 0)),
                   pl.BlockSpec((tm * SUBLANE, LANE), lambda s, t: (s * nt + t, 0)),
                   pl.BlockSpec((TOP_K, tm), tok),
                   pl.BlockSpec((TOP_K, tm), tok),
                   pl.BlockSpec((TOP_K, tm), tok),
                   pl.BlockSpec((N_EXPERTS, LANE), const)),
        scratch_shapes=[pltpu.VMEM((N_EXPERTS, LANE), F32)],
        compiler_params=_cparams("arbitrary", "arbitrary"),
        name="outproj",
    )(mix, x, g1, sc2, sh2, n2, wo, wr, br, upper, cnt0)


def _dispatch_body(pos_ref, h2_ref, xs_ref, sem):
    td = pos_ref.shape[1]
    base = pl.program_id(0) * td

    def issue(t, c):
        for k in range(TOP_K):
            pltpu.make_async_copy(h2_ref.at[base + t], xs_ref.at[pos_ref[k, t]], sem).start()
        return c

    lax.fori_loop(0, td, issue, 0)

    def drain(t, c):
        for k in range(TOP_K):
            pltpu.make_async_copy(h2_ref.at[0], xs_ref.at[0], sem).wait()
        return c

    lax.fori_loop(0, td, drain, 0)


def _dispatch_first_kernel(pos_ref, h2_ref, xs_ref, sem):
    _dispatch_body(pos_ref, h2_ref, xs_ref, sem)


def _dispatch_next_kernel(pos_ref, h2_ref, xs_in_ref, xs_ref, sem):
    del xs_in_ref
    _dispatch_body(pos_ref, h2_ref, xs_ref, sem)


def _dispatch(pos, h2, xs, n_rows, td):
    T = h2.shape[0]
    out_shape = jax.ShapeDtypeStruct((n_rows, SUBLANE, LANE), F32)
    pos_spec = pl.BlockSpec((TOP_K, td), lambda i: (0, i), memory_space=pltpu.SMEM)
    any_spec = pl.BlockSpec(memory_space=pl.ANY)
    common = dict(out_shape=out_shape, grid=(T // td,), out_specs=any_spec,
                  scratch_shapes=[pltpu.SemaphoreType.DMA],
                  compiler_params=pltpu.CompilerParams(dimension_semantics=("arbitrary",), has_side_effects=True))
    if xs is None:
        return pl.pallas_call(_dispatch_first_kernel, in_specs=[pos_spec, any_spec],
                              name="dispatch_first", **common)(pos, h2)
    return pl.pallas_call(_dispatch_next_kernel, in_specs=[pos_spec, any_spec, any_spec],
                          input_output_aliases={2: 0}, name="dispatch_next", **common)(pos, h2, xs)


def _experts_kernel(te_ref, tv_ref, xs_ref, wg_ref, bg_ref, wu_ref, bu_ref, wd_ref, bd_ref, y_ref):
    i = pl.program_id(0)
    valid = tv_ref[i]
    tm = xs_ref.shape[0] // SUBLANE
    d = wg_ref.shape[1]
    ff = wg_ref.shape[2]

    @pl.when(valid > 0)
    def _():
        x = jnp.concatenate([xs_ref[pl.ds(j, tm, stride=SUBLANE), :] for j in range(d // LANE)], axis=1)
        row = lax.broadcasted_iota(jnp.int32, (tm, 1), 0)
        xb = jnp.where(row < valid, x, 0.0).astype(BF16)
        acc = jnp.zeros((tm, d), F32)
        for c in range(ff // FF_CHUNK):
            cs = slice(c * FF_CHUNK, (c + 1) * FF_CHUNK)
            gate = jnp.minimum(_dot(xb, wg_ref[0, :, cs]) + bg_ref[0, :, cs], SWIGLU_LIMIT)
            up = jnp.clip(_dot(xb, wu_ref[0, :, cs]) + bu_ref[0, :, cs], -SWIGLU_LIMIT, SWIGLU_LIMIT)
            act = (up + 1.0) * gate * jax.nn.sigmoid(SWIGLU_ALPHA * gate)
            acc = acc + _dot(act.astype(BF16), wd_ref[0, cs, :])
        y = acc + bd_ref[0]
        for j in range(d // LANE):
            y_ref[pl.ds(j, tm, stride=SUBLANE), :] = y[:, j * LANE:(j + 1) * LANE]


def _experts(tile_expert, tile_valid, xs2d, wg, bg, wu, bu, wd, bd, tmg):
    n_tiles = tile_expert.shape[0]
    _, d, ff = wg.shape
    wspec = lambda shp: pl.BlockSpec(shp, lambda i, te, tv: (te[i], 0, 0))
    return pl.pallas_call(
        _experts_kernel,
        out_shape=jax.ShapeDtypeStruct(xs2d.shape, F32),
        grid_spec=pltpu.PrefetchScalarGridSpec(
            num_scalar_prefetch=2,
            grid=(n_tiles,),
            in_specs=[
                pl.BlockSpec((tmg * SUBLANE, LANE), lambda i, te, tv: (i, 0)),
                wspec((1, d, ff)), wspec((1, 1, ff)),
                wspec((1, d, ff)), wspec((1, 1, ff)),
                wspec((1, ff, d)), wspec((1, 1, d)),
            ],
            out_specs=pl.BlockSpec((tmg * SUBLANE, LANE), lambda i, te, tv: (i, 0)),
        ),
        compiler_params=_cparams("arbitrary"),
        name="experts",
    )(tile_expert, tile_valid, xs2d, wg, bg, wu, bu, wd, bd)


def _combine_kernel(pos_ref, y_ref, tw_ref, xmid_ref, g2_ref, nf_ref, o_ref, ybuf, sem):
    sb, rb, d = xmid_ref.shape
    tc = sb * rb

    def issue(t, c):
        for k in range(TOP_K):
            dst = pl.multiple_of((k * tc + t) * SUBLANE, SUBLANE)
            pltpu.make_async_copy(y_ref.at[pos_ref[k, t]], ybuf.at[pl.ds(dst, SUBLANE)], sem).start()
        return c

    lax.fori_loop(0, tc, issue, 0)

    def drain(t, c):
        for k in range(TOP_K):
            pltpu.make_async_copy(y_ref.at[0], ybuf.at[pl.ds(0, SUBLANE)], sem).wait()
        return c

    lax.fori_loop(0, tc, drain, 0)

    tw = tw_ref[...]
    moe = jnp.zeros((tc, d), F32)
    for k in range(TOP_K):
        yk = jnp.concatenate(
            [ybuf[pl.ds(k * tc * SUBLANE + j, tc, stride=SUBLANE), :] for j in range(d // LANE)], axis=1)
        moe = moe + tw[:, k:k + 1] * yk
    out = xmid_ref[...] + g2_ref[...] * moe.reshape(sb, rb, d)
    ms = jnp.mean(out * out, axis=-1, keepdims=True)
    o_ref[...] = out * lax.rsqrt(ms + EPS) * nf_ref[...]


def _combine(pos, y3d, tw, xmid, g2, nf, sb, rb):
    nseq, L, d = xmid.shape
    tc = sb * rb
    nt = L // rb
    return pl.pallas_call(
        _combine_kernel,
        out_shape=jax.ShapeDtypeStruct((nseq, L, d), F32),
        grid=(nseq // sb, nt),
        in_specs=[
            pl.BlockSpec((TOP_K, tc), lambda s, t: (0, s * nt + t), memory_space=pltpu.SMEM),
            pl.BlockSpec(memory_space=pl.ANY),
            pl.BlockSpec((tc, TOP_K), lambda s, t: (s * nt + t, 0)),
            pl.BlockSpec((sb, rb, d), lambda s, t: (s, t, 0)),
            pl.BlockSpec((sb, 1, d), lambda s, t: (s, 0, 0)),
            pl.BlockSpec((1, d), lambda s, t: (0, 0)),
        ],
        out_specs=pl.BlockSpec((sb, rb, d), lambda s, t: (s, t, 0)),
        scratch_shapes=[pltpu.VMEM((TOP_K * tc * SUBLANE, LANE), F32), pltpu.SemaphoreType.DMA],
        compiler_params=_cparams("arbitrary", "arbitrary"),
        name="combine",
    )(pos, y3d, tw, xmid, g2, nf)


def _tile_rows(nseq, L, tile):
    if L >= tile:
        assert L % tile == 0
        return 1, tile
    assert tile % L == 0 and nseq % (tile // L) == 0
    return tile // L, L


def kernel(x_prompt, x_sample, state_gla, c_prompt, c_sample, w_ada, b_ada, norm1, w_in, w_gk, b_gk, gla_norm,
           gmlp_ln_g, gmlp_ln_b, gmlp_w_s, gmlp_b_s, w_out, norm2, w_router, b_router, w_gate, b_gate, w_up,
           b_up, w_down, b_down, norm_f):
    depth = w_ada.shape[0]
    assert depth == 1
    bp, lp, d = x_prompt.shape
    bs, ls, _ = x_sample.shape
    tp, ts = bp * lp, bs * ls

    nc = bp + bs
    ncp = -(-nc // SUBLANE) * SUBLANE
    c_all = jnp.concatenate([c_prompt, c_sample, jnp.zeros((ncp - nc, d), F32)], axis=0)
    mod = _ada(c_all, w_ada[0], b_ada[0][None]).reshape(ncp, N_MOD, 1, d)
    mods_p = [mod[:bp, i] for i in range(N_MOD)]
    mods_s = [mod[bp:nc, i] for i in range(N_MOD)]

    wi = w_in[0]
    c_lr = 2 * GLA_QK + GLA_WIDTH
    c_r = c_lr + GLA_LOWRANK
    wm = jnp.concatenate([wi[:, :c_lr], wi[:, c_r:]], axis=1).astype(BF16)
    wlr = jnp.pad(wi[:, c_lr:c_r], ((0, 0), (0, LANE - GLA_LOWRANK))).astype(BF16)
    wgk = jnp.pad(w_gk[0], ((0, LANE - GLA_LOWRANK), (0, 0)))
    bgk = b_gk[0][None]
    n1, n2, nf = norm1[0][None], norm2[0][None], norm_f[None]
    gn, lng, lnb = gla_norm[0][None], gmlp_ln_g[0][None], gmlp_ln_b[0][None]
    ws, bsv = gmlp_w_s[0], gmlp_b_s[0]
    pos_i = jnp.arange(GMLP_BLOCK)
    cmask = (pos_i[None, :] // CHUNK) <= (pos_i[:, None] // CHUNK)
    wm_p = jnp.where(cmask[None], ws, 0.0).astype(BF16)
    bsb_p = jnp.repeat(bsv.T, GMLP_GC, axis=1)
    reps = ROW_BLOCK // ls
    eye = jnp.eye(reps, dtype=F32)
    wm_s = jnp.einsum("ab,gij->gaibj", eye, ws[:, :ls, :ls]).reshape(GMLP_GROUPS, ROW_BLOCK, ROW_BLOCK).astype(BF16)
    bsb_s = jnp.tile(jnp.repeat(bsv[:, :ls].T, GMLP_GC, axis=1), (reps, 1))
    wo = w_out[0].astype(BF16)
    wr = jnp.pad(w_router[0], ((0, 0), (0, LANE - N_EXPERTS)))
    br = jnp.concatenate([b_router[0], jnp.full((LANE - N_EXPERTS,), -1e30, F32)])[None]
    upper = (jnp.arange(TOKEN_TILE)[:, None] < jnp.arange(TOKEN_TILE)[None, :]).astype(BF16)
    wg, wu, wd = w_gate[0].astype(BF16), w_up[0].astype(BF16), w_down[0].astype(BF16)
    bg, bu, bd = b_gate[0][:, None], b_up[0][:, None], b_down[0][:, None]

    sbp, rbp = _tile_rows(bp, lp, TOKEN_TILE)
    sbs, rbs = _tile_rows(bs, ls, TOKEN_TILE)

    proj_p, gk_p = _inproj(x_prompt, mods_p[1], mods_p[0], n1, wm, wlr, wgk, bgk, sbp, rbp)
    proj_s, gk_s = _inproj(x_sample, mods_s[1], mods_s[0], n1, wm, wlr, wgk, bgk, sbs, rbs)
    mix_p, state_p = _mixer_prompt(proj_p, gk_p, gn, lng, lnb, wm_p, bsb_p, TOKEN_TILE)
    mix_s, state_s, vn_s = _mixer_sample(proj_s, gk_s, state_gla[0], gn, lng, lnb, wm_s, bsb_s)

    cnt0 = jnp.zeros((N_EXPERTS, LANE), F32)
    xmid_p, h2_p, topi_p, topw_p, rank_p, cnt_p = _outproj(
        mix_p, x_prompt, mods_p[2], mods_p[4], mods_p[3], n2, wo, wr, br, upper, cnt0, sbp, rbp)
    xmid_s, h2_s, topi_s, topw_s, rank_s, cnt_s = _outproj(
        mix_s, x_sample, mods_s[2], mods_s[4], mods_s[3], n2, wo, wr, br, upper, cnt_p, sbs, rbs)

    tmg = EXPERT_TILE
    cnt = cnt_s[:, 0].astype(jnp.int32)
    tiles_e = (cnt + tmg - 1) // tmg
    tile_end = jnp.cumsum(tiles_e)
    tile_start = tile_end - tiles_e
    row_start = tile_start * tmg
    n_tiles = (TOP_K * (tp + ts)) // tmg + N_EXPERTS
    tid = jnp.arange(n_tiles, dtype=jnp.int32)
    te = jnp.minimum(jnp.searchsorted(tile_end, tid, side="right"), N_EXPERTS - 1).astype(jnp.int32)
    tv = jnp.clip(cnt[te] - (tid - tile_start[te]) * tmg, 0, tmg)
    tv = jnp.where(tid < tile_end[-1], tv, 0).astype(jnp.int32)
    last_e = jnp.max(jnp.where(tiles_e > 0, jnp.arange(N_EXPERTS), 0)).astype(jnp.int32)
    te = jnp.where(tid < tile_end[-1], te, last_e)
    pos_p = (row_start[topi_p] + rank_p).astype(jnp.int32)
    pos_s = (row_start[topi_s] + rank_s).astype(jnp.int32)

    n_rows = n_tiles * tmg
    xs = _dispatch(pos_p, h2_p.reshape(tp, SUBLANE, LANE), None, n_rows, MOVE_TILE)
    xs = _dispatch(pos_s, h2_s.reshape(ts, SUBLANE, LANE), xs, n_rows, MOVE_TILE)
    y = _experts(te, tv, xs.reshape(n_rows * SUBLANE, LANE), wg, bg, wu, bu, wd, bd, tmg)
    y3 = y.reshape(n_rows, SUBLANE, LANE)
    sbc, rbc = _tile_rows(bp, lp, MOVE_TILE)
    y_prompt = _combine(pos_p, y3, topw_p.T, xmid_p, mods_p[5], nf, sbc, rbc)
    sbc, rbc = _tile_rows(bs, ls, MOVE_TILE)
    y_sample = _combine(pos_s, y3, topw_s.T, xmid_s, mods_s[5], nf, sbc, rbc)

    return (y_prompt, y_sample, state_p[None], state_s[None], vn_s[None])
```

```python
import functools

import jax
import jax.numpy as jnp
from jax import lax
from jax.experimental import pallas as pl
from jax.experimental.pallas import tpu as pltpu

F32 = jnp.float32
BF16 = jnp.bfloat16
HIGHEST = lax.Precision.HIGHEST

CHUNK = 64
GLA_HEADS = 4
GLA_DK = 64
GLA_DV = 128
GLA_QK = GLA_HEADS * GLA_DK
GLA_WIDTH = GLA_HEADS * GLA_DV
GLA_LOWRANK = 16
GLA_GATE_NORM = 16.0
GMLP_WIDTH = 512
GMLP_GROUPS = 4
GMLP_GC = GMLP_WIDTH // GMLP_GROUPS
GMLP_BLOCK = 128
N_EXPERTS = 32
TOP_K = 4
SWIGLU_LIMIT = 7.0
SWIGLU_ALPHA = 1.702
EPS = 1e-6
N_MOD = 6

LANE = 128
SUBLANE = 8

Q0 = 0
K0 = Q0 + GLA_QK
V0 = K0 + GLA_QK
R0 = V0 + GLA_WIDTH
U0 = R0 + GLA_WIDTH
G0 = U0 + GMLP_WIDTH
PROJ_COLS = G0 + GMLP_WIDTH

ROW_BLOCK = 128
TOKEN_TILE = 512
EXPERT_TILE = 512
MOVE_TILE = 256
FF_CHUNK = 512
VMEM_LIMIT = 48 * 1024 * 1024


def _cparams(*sem):
    return pltpu.CompilerParams(dimension_semantics=sem, vmem_limit_bytes=VMEM_LIMIT)


def _dot(a, b, **kw):
    return jnp.dot(a, b, preferred_element_type=F32, **kw)


def _dot_nt(a, b):
    return lax.dot_general(a, b, (((1,), (1,)), ((), ())), preferred_element_type=F32)


def _gelu(x):
    return 0.5 * x * (1.0 + lax.erf(x * (0.5 ** 0.5)))


def _dot_tn(a, b):
    return lax.dot_general(a, b, (((0,), (0,)), ((), ())), preferred_element_type=F32)


def _ada_kernel(c_ref, w_ref, b_ref, o_ref):
    c = c_ref[...]
    s = c * jax.nn.sigmoid(c)
    o_ref[...] = _dot(s, w_ref[...], precision=HIGHEST) + b_ref[...]


def _ada(c, w_ada, b_ada):
    n, d = c.shape
    return pl.pallas_call(
        _ada_kernel,
        out_shape=jax.ShapeDtypeStruct((n, N_MOD * d), F32),
        grid=(N_MOD,),
        in_specs=[
            pl.BlockSpec((n, d), lambda j: (0, 0)),
            pl.BlockSpec((d, d), lambda j: (0, j)),
            pl.BlockSpec((1, d), lambda j: (0, j)),
        ],
        out_specs=pl.BlockSpec((n, d), lambda j: (0, j)),
        compiler_params=_cparams("arbitrary"),
        name="ada",
    )(c, w_ada, b_ada)


def _inproj_kernel(x_ref, sc_ref, sh_ref, n1_ref, wm_ref, wlr_ref, wgk_ref, bgk_ref, proj_ref, gk_ref):
    sb, rb, d = x_ref.shape
    x = x_ref[...]
    ms = jnp.mean(x * x, axis=-1, keepdims=True)
    h = x * lax.rsqrt(ms + EPS) * n1_ref[...]
    h = h * (1.0 + sc_ref[...]) + sh_ref[...]
    hb = h.reshape(sb * rb, d).astype(BF16)
    cw = 512
    for c in range(PROJ_COLS // cw):
        p = _dot(hb, wm_ref[:, c * cw:(c + 1) * cw])
        proj_ref[:, :, c * cw:(c + 1) * cw] = p.astype(BF16).reshape(sb, rb, cw)
    lr = _dot(hb, wlr_ref[...])
    gk = _dot(lr, wgk_ref[...], precision=HIGHEST) + bgk_ref[...]
    gk_ref[...] = gk.reshape(sb, rb, GLA_QK)


def _inproj(x, sc, sh, n1, wm, wlr, wgk, bgk, sb, rb):
    nseq, L, d = x.shape
    grid = (nseq // sb, L // rb)
    const = lambda s, t: (0, 0)
    return pl.pallas_call(
        _inproj_kernel,
        out_shape=(jax.ShapeDtypeStruct((nseq, L, PROJ_COLS), BF16),
                   jax.ShapeDtypeStruct((nseq, L, GLA_QK), F32)),
        grid=grid,
        in_specs=[
            pl.BlockSpec((sb, rb, d), lambda s, t: (s, t, 0)),
            pl.BlockSpec((sb, 1, d), lambda s, t: (s, 0, 0)),
            pl.BlockSpec((sb, 1, d), lambda s, t: (s, 0, 0)),
            pl.BlockSpec((1, d), const),
            pl.BlockSpec((d, PROJ_COLS), const),
            pl.BlockSpec((d, LANE), const),
            pl.BlockSpec((LANE, GLA_QK), const),
            pl.BlockSpec((1, GLA_QK), const),
        ],
        out_specs=(pl.BlockSpec((sb, rb, PROJ_COLS), lambda s, t: (s, t, 0)),
                   pl.BlockSpec((sb, rb, GLA_QK), lambda s, t: (s, t, 0))),
        compiler_params=_cparams("arbitrary", "arbitrary"),
        name="inproj",
    )(x, sc, sh, n1, wm, wlr, wgk, bgk)


def _head_masks():
    lane = lax.broadcasted_iota(jnp.int32, (1, GLA_QK), 1)
    return [(lane // GLA_DK) == h for h in range(GLA_HEADS)]


def _mixer_block(p, gkpre, states, c_len, gn, lng, lnb, wm_ref, bsb):
    rows = p.shape[0]
    n_chunks = rows // c_len
    chained = len(states) == 1
    hm = _head_masks()

    q = p[:, Q0:Q0 + GLA_QK].astype(F32)
    k = p[:, K0:K0 + GLA_QK].astype(F32)
    v = p[:, V0:V0 + GLA_WIDTH]
    r = p[:, R0:R0 + GLA_WIDTH].astype(F32)

    g = jax.nn.log_sigmoid(gkpre) / GLA_GATE_NORM
    ri = lax.broadcasted_iota(jnp.int32, (rows, rows), 0)
    ci = lax.broadcasted_iota(jnp.int32, (rows, rows), 1)
    tri = ((ci <= ri) & ((ci // c_len) == (ri // c_len))).astype(F32)
    G = _dot(tri, g, precision=HIGHEST)
    qe = (q * (GLA_DK ** -0.5)) * jnp.exp(G)
    ke = (k * jnp.exp(-G)).astype(BF16)

    ti = lax.broadcasted_iota(jnp.int32, (GLA_HEADS * c_len, c_len), 0) % c_len
    si = lax.broadcasted_iota(jnp.int32, (GLA_HEADS * c_len, c_len), 1)
    causal = si <= ti

    new_states = []
    o_rows = []
    st = states[0]
    for c in range(n_chunks):
        lo, hi = c * c_len, (c + 1) * c_len
        if not chained:
            st = states[c]
        qe_c = qe[lo:hi]
        q4 = jnp.concatenate([jnp.where(hm[h], qe_c, 0.0) for h in range(GLA_HEADS)], axis=0).astype(BF16)
        a = jnp.where(causal, _dot_nt(q4, ke[lo:hi]), 0.0).astype(BF16)
        o_inter = _dot_nt(q4, st.astype(BF16))
        v_c = v[lo:hi]
        heads = []
        for h in range(GLA_HEADS):
            o_h = o_inter[h * c_len:(h + 1) * c_len] + _dot(
                a[h * c_len:(h + 1) * c_len], v_c[:, h * GLA_DV:(h + 1) * GLA_DV])
            heads.append(o_h)
        o_rows.append(jnp.concatenate(heads, axis=1))
        g_last = G[hi - 1:hi]
        kd = (k[lo:hi] * jnp.exp(g_last - G[lo:hi])).astype(BF16)
        upd = _dot_tn(v_c, kd)
        st_new = jnp.exp(g_last) * st
        for h in range(GLA_HEADS):
            st_new = st_new + jnp.where(hm[h], upd[h * GLA_DV:(h + 1) * GLA_DV], 0.0)
        if chained:
            st = st_new
        else:
            new_states.append(st_new)
    if chained:
        new_states = [st]
    o = jnp.concatenate(o_rows, axis=0)

    gla = []
    for h in range(GLA_HEADS):
        o_h = o[:, h * GLA_DV:(h + 1) * GLA_DV]
        ms = jnp.mean(o_h * o_h, axis=-1, keepdims=True)
        r_h = r[:, h * GLA_DV:(h + 1) * GLA_DV]
        gla.append(o_h * lax.rsqrt(ms + EPS) * gn * (r_h * jax.nn.sigmoid(r_h)))

    u = _gelu(p[:, U0:U0 + GMLP_WIDTH].astype(F32))
    vv = _gelu(p[:, G0:G0 + GMLP_WIDTH].astype(F32))
    mu = jnp.mean(vv, axis=-1, keepdims=True)
    xc = vv - mu
    var = jnp.mean(xc * xc, axis=-1, keepdims=True)
    vn = xc * lax.rsqrt(var + EPS) * lng + lnb
    vnb = vn.astype(BF16)
    gm = []
    for gi in range(GMLP_GROUPS):
        sl = slice(gi * GMLP_GC, (gi + 1) * GMLP_GC)
        mixed = _dot(wm_ref[gi], vnb[:, sl]) + bsb[:, sl]
        gm.append(u[:, sl] * mixed)
    out = jnp.concatenate(gla + gm, axis=1)
    return out, new_states, vn


def _mixer_prompt_kernel(proj_ref, gk_ref, gn_ref, lng_ref, lnb_ref, wm_ref, bsb_ref, mix_ref, s_ref, st_scr):
    t = pl.program_id(1)
    nt = pl.num_programs(1)

    @pl.when(t == 0)
    def _():
        st_scr[...] = jnp.zeros_like(st_scr)

    n_sub = proj_ref.shape[1] // ROW_BLOCK

    def body(i, carry):
        r0 = pl.multiple_of(i * ROW_BLOCK, ROW_BLOCK)
        p = proj_ref[0, pl.ds(r0, ROW_BLOCK), :]
        gk = gk_ref[0, pl.ds(r0, ROW_BLOCK), :]
        out, sts, _ = _mixer_block(p, gk, [st_scr[...]], CHUNK, gn_ref[...], lng_ref[...], lnb_ref[...],
                                   wm_ref, bsb_ref[...])
        st_scr[...] = sts[0]
        mix_ref[0, pl.ds(r0, ROW_BLOCK), :] = out.astype(BF16)
        return carry

    lax.fori_loop(0, n_sub, body, 0)

    @pl.when(t == nt - 1)
    def _():
        s_ref[0] = st_scr[...].T.reshape(GLA_HEADS, GLA_DK, GLA_DV)


def _mixer_prompt(proj, gk, gn, lng, lnb, wm, bsb, tb):
    b, L, _ = proj.shape
    const2 = lambda s, t: (0, 0)
    return pl.pallas_call(
        _mixer_prompt_kernel,
        out_shape=(jax.ShapeDtypeStruct((b, L, 2 * GLA_WIDTH), BF16),
                   jax.ShapeDtypeStruct((b, GLA_HEADS, GLA_DK, GLA_DV), F32)),
        grid=(b, L // tb),
        in_specs=[
            pl.BlockSpec((1, tb, PROJ_COLS), lambda s, t: (s, t, 0)),
            pl.BlockSpec((1, tb, GLA_QK), lambda s, t: (s, t, 0)),
            pl.BlockSpec((1, GLA_DV), const2),
            pl.BlockSpec((1, GMLP_WIDTH), const2),
            pl.BlockSpec((1, GMLP_WIDTH), const2),
            pl.BlockSpec((GMLP_GROUPS, ROW_BLOCK, ROW_BLOCK), lambda s, t: (0, 0, 0)),
            pl.BlockSpec((ROW_BLOCK, GMLP_WIDTH), const2),
        ],
        out_specs=(pl.BlockSpec((1, tb, 2 * GLA_WIDTH), lambda s, t: (s, t, 0)),
                   pl.BlockSpec((1, GLA_HEADS, GLA_DK, GLA_DV), lambda s, t: (s, 0, 0, 0))),
        scratch_shapes=[pltpu.VMEM((GLA_DV, GLA_QK), F32)],
        compiler_params=_cparams("arbitrary", "arbitrary"),
        name="mixer_prompt",
    )(proj, gk, gn, lng, lnb, wm, bsb)


def _mixer_sample_kernel(proj_ref, gk_ref, s0_ref, gn_ref, lng_ref, lnb_ref, wm_ref, bsb_ref,
                         mix_ref, s_ref, vn_ref):
    sb, rb, _ = proj_ref.shape
    p = proj_ref[...].reshape(sb * rb, PROJ_COLS)
    gk = gk_ref[...].reshape(sb * rb, GLA_QK)
    states = [s0_ref[i].reshape(GLA_QK, GLA_DV).T for i in range(sb)]
    out, sts, vn = _mixer_block(p, gk, states, rb, gn_ref[...], lng_ref[...], lnb_ref[...], wm_ref, bsb_ref[...])
    mix_ref[...] = out.astype(BF16).reshape(sb, rb, 2 * GLA_WIDTH)
    vn_ref[...] = vn.reshape(sb, rb, GMLP_WIDTH)
    for i in range(sb):
        s_ref[i] = sts[i].T.reshape(GLA_HEADS, GLA_DK, GLA_DV)


def _mixer_sample(proj, gk, s0, gn, lng, lnb, wm, bsb):
    n, L, _ = proj.shape
    sb = ROW_BLOCK // L
    const2 = lambda s: (0, 0)
    return pl.pallas_call(
        _mixer_sample_kernel,
        out_shape=(jax.ShapeDtypeStruct((n, L, 2 * GLA_WIDTH), BF16),
                   jax.ShapeDtypeStruct((n, GLA_HEADS, GLA_DK, GLA_DV), F32),
                   jax.ShapeDtypeStruct((n, L, GMLP_WIDTH), F32)),
        grid=(n // sb,),
        in_specs=[
            pl.BlockSpec((sb, L, PROJ_COLS), lambda s: (s, 0, 0)),
            pl.BlockSpec((sb, L, GLA_QK), lambda s: (s, 0, 0)),
            pl.BlockSpec((sb, GLA_HEADS, GLA_DK, GLA_DV), lambda s: (s, 0, 0, 0)),
            pl.BlockSpec((1, GLA_DV), const2),
            pl.BlockSpec((1, GMLP_WIDTH), const2),
            pl.BlockSpec((1, GMLP_WIDTH), const2),
            pl.BlockSpec((GMLP_GROUPS, ROW_BLOCK, ROW_BLOCK), lambda s: (0, 0, 0)),
            pl.BlockSpec((ROW_BLOCK, GMLP_WIDTH), const2),
        ],
        out_specs=(pl.BlockSpec((sb, L, 2 * GLA_WIDTH), lambda s: (s, 0, 0)),
                   pl.BlockSpec((sb, GLA_HEADS, GLA_DK, GLA_DV), lambda s: (s, 0, 0, 0)),
                   pl.BlockSpec((sb, L, GMLP_WIDTH), lambda s: (s, 0, 0))),
        compiler_params=_cparams("arbitrary"),
        name="mixer_sample",
    )(proj, gk, s0, gn, lng, lnb, wm, bsb)


def _outproj_kernel(mix_ref, x_ref, g1_ref, sc_ref, sh_ref, n2_ref, wo_ref, wr_ref, br_ref, upper_ref, cnt0_ref,
                    xmid_ref, h2_ref, topi_ref, topw_ref, rank_ref, cnt_ref, carry_scr):
    sb, rb, d = x_ref.shape
    tm = sb * rb
    first = (pl.program_id(0) == 0) & (pl.program_id(1) == 0)

    @pl.when(first)
    def _():
        carry_scr[...] = cnt0_ref[...]

    y = _dot(mix_ref[...].reshape(tm, d), wo_ref[...])
    xm = x_ref[...] + g1_ref[...] * y.reshape(sb, rb, d)
    xmid_ref[...] = xm
    ms = jnp.mean(xm * xm, axis=-1, keepdims=True)
    h2 = xm * lax.rsqrt(ms + EPS) * n2_ref[...]
    h2 = (h2 * (1.0 + sc_ref[...]) + sh_ref[...]).reshape(tm, d)
    for j in range(d // LANE):
        h2_ref[pl.ds(j, tm, stride=SUBLANE), :] = h2[:, j * LANE:(j + 1) * LANE]

    logits = _dot(h2, wr_ref[...], precision=HIGHEST) + br_ref[...]
    l = logits.T[:N_EXPERTS]
    eid = lax.broadcasted_iota(jnp.int32, (N_EXPERTS, tm), 0).astype(F32)
    top_l, top_i, sel = [], [], []
    for _ in range(TOP_K):
        m = jnp.max(l, axis=0, keepdims=True)
        idx = jnp.min(jnp.where(l == m, eid, float(N_EXPERTS)), axis=0, keepdims=True)
        hit = eid == idx
        top_l.append(m)
        top_i.append(idx)
        sel.append(hit)
        l = jnp.where(hit, -jnp.inf, l)
    ex = [jnp.exp(t - top_l[0]) for t in top_l]
    den = ex[0] + ex[1] + ex[2] + ex[3]
    topw_ref[...] = jnp.concatenate([e / den for e in ex], axis=0)
    topi_ref[...] = jnp.concatenate(top_i, axis=0).astype(jnp.int32)

    chosen = (sel[0] | sel[1] | sel[2] | sel[3])
    cb = jnp.where(chosen, 1.0, 0.0)
    before = _dot(cb.astype(BF16), upper_ref[...])
    base = before + carry_scr[:, 0:1]
    rank_ref[...] = jnp.concatenate(
        [jnp.sum(jnp.where(s, base, 0.0), axis=0, keepdims=True) for s in sel], axis=0).astype(jnp.int32)
    carry_scr[...] = carry_scr[...] + jnp.sum(cb, axis=1, keepdims=True)
    cnt_ref[...] = carry_scr[...]


def _outproj(mix, x, g1, sc2, sh2, n2, wo, wr, br, upper, cnt0, sb, rb):
    nseq, L, d = x.shape
    tm = sb * rb
    nt = L // rb
    T = nseq * L
    grid = (nseq // sb, nt)
    const = lambda s, t: (0, 0)
    tok = lambda s, t: (0, s * nt + t)
    return pl.pallas_call(
        _outproj_kernel,
        out_shape=(jax.ShapeDtypeStruct((nseq, L, d), F32),
                   jax.ShapeDtypeStruct((T * SUBLANE, LANE), F32),
                   jax.ShapeDtypeStruct((TOP_K, T), jnp.int32),
                   jax.ShapeDtypeStruct((TOP_K, T), F32),
                   jax.ShapeDtypeStruct((TOP_K, T), jnp.int32),
                   jax.ShapeDtypeStruct((N_EXPERTS, LANE), F32)),
        grid=grid,
        in_specs=[
            pl.BlockSpec((sb, rb, d), lambda s, t: (s, t, 0)),
            pl.BlockSpec((sb, rb, d), lambda s, t: (s, t, 0)),
            pl.BlockSpec((sb, 1, d), lambda s, t: (s, 0, 0)),
            pl.BlockSpec((sb, 1, d), lambda s, t: (s, 0, 0)),
            pl.BlockSpec((sb, 1, d), lambda s, t: (s, 0, 0)),
            pl.BlockSpec((1, d), const),
            pl.BlockSpec((d, d), const),
            pl.BlockSpec((d, LANE), const),
            pl.BlockSpec((1, LANE), const),
            pl.BlockSpec((tm, tm), const),
            pl.BlockSpec((N_EXPERTS, LANE), const),
        ],
        out_specs=(pl.BlockSpec((sb, rb, d), lambda s, t: (s, t, 0)),
                   pl.BlockSpec((tm * SUBLANE, LANE), lambda s, t: (s * nt + t, 0)),
                   pl.BlockSpec((TOP_K, tm), tok),
                   pl.BlockSpec((TOP_K, tm), tok),
                   pl.BlockSpec((TOP_K, tm), tok),
                   pl.BlockSpec((N_EXPERTS, LANE), const)),
        scratch_shapes=[pltpu.VMEM((N_EXPERTS, LANE), F32)],
        compiler_params=_cparams("arbitrary", "arbitrary"),
        name="outproj",
    )(mix, x, g1, sc2, sh2, n2, wo, wr, br, upper, cnt0)


def _dispatch_body(pos_ref, h2_ref, xs_ref, sem):
    td = pos_ref.shape[1]

    def issue(t, c):
        src = h2_ref.at[pl.ds(pl.multiple_of(t * SUBLANE, SUBLANE), SUBLANE)]
        for k in range(TOP_K):
            pltpu.make_async_copy(src, xs_ref.at[pos_ref[k, t]], sem).start()
        return c

    lax.fori_loop(0, td, issue, 0)

    def drain(t, c):
        for k in range(TOP_K):
            pltpu.make_async_copy(h2_ref.at[pl.ds(0, SUBLANE)], xs_ref.at[0], sem).wait()
        return c

    lax.fori_loop(0, td, drain, 0)


def _dispatch_first_kernel(pos_ref, h2_ref, xs_ref, sem):
    _dispatch_body(pos_ref, h2_ref, xs_ref, sem)


def _dispatch_next_kernel(pos_ref, h2_ref, xs_in_ref, xs_ref, sem):
    del xs_in_ref
    _dispatch_body(pos_ref, h2_ref, xs_ref, sem)


def _dispatch(pos, h2, xs, n_rows, td):
    T = h2.shape[0] // SUBLANE
    out_shape = jax.ShapeDtypeStruct((n_rows, SUBLANE, LANE), F32)
    pos_spec = pl.BlockSpec((TOP_K, td), lambda i: (0, i), memory_space=pltpu.SMEM)
    h2_spec = pl.BlockSpec((td * SUBLANE, LANE), lambda i: (i, 0))
    any_spec = pl.BlockSpec(memory_space=pl.ANY)
    common = dict(out_shape=out_shape, grid=(T // td,), out_specs=any_spec,
                  scratch_shapes=[pltpu.SemaphoreType.DMA],
                  compiler_params=pltpu.CompilerParams(dimension_semantics=("arbitrary",), has_side_effects=True))
    if xs is None:
        return pl.pallas_call(_dispatch_first_kernel, in_specs=[pos_spec, h2_spec],
                              name="dispatch_first", **common)(pos, h2)
    return pl.pallas_call(_dispatch_next_kernel, in_specs=[pos_spec, h2_spec, any_spec],
                          input_output_aliases={2: 0}, name="dispatch_next", **common)(pos, h2, xs)


def _experts_kernel(te_ref, tv_ref, xs_ref, wg_ref, bg_ref, wu_ref, bu_ref, wd_ref, bd_ref, y_ref):
    i = pl.program_id(0)
    valid = tv_ref[i]
    tm = xs_ref.shape[0] // SUBLANE
    d = wg_ref.shape[1]
    ff = wg_ref.shape[2]

    @pl.when(valid > 0)
    def _():
        x = jnp.concatenate([xs_ref[pl.ds(j, tm, stride=SUBLANE), :] for j in range(d // LANE)], axis=1)
        row = lax.broadcasted_iota(jnp.int32, (tm, 1), 0)
        xb = jnp.where(row < valid, x, 0.0).astype(BF16)
        acc = jnp.zeros((tm, d), F32)
        for c in range(ff // FF_CHUNK):
            cs = slice(c * FF_CHUNK, (c + 1) * FF_CHUNK)
            gate = jnp.minimum(_dot(xb, wg_ref[0, :, cs]) + bg_ref[0, :, cs], SWIGLU_LIMIT)
            up = jnp.clip(_dot(xb, wu_ref[0, :, cs]) + bu_ref[0, :, cs], -SWIGLU_LIMIT, SWIGLU_LIMIT)
            act = (up + 1.0) * gate * jax.nn.sigmoid(SWIGLU_ALPHA * gate)
            acc = acc + _dot(act.astype(BF16), wd_ref[0, cs, :])
        y = acc + bd_ref[0]
        for j in range(d // LANE):
            y_ref[pl.ds(j, tm, stride=SUBLANE), :] = y[:, j * LANE:(j + 1) * LANE]

    @pl.when(valid == 0)
    def _():
        y_ref[...] = jnp.zeros_like(y_ref)


def _experts(tile_expert, tile_valid, xs2d, wg, bg, wu, bu, wd, bd, tmg):
    n_tiles = tile_expert.shape[0]
    _, d, ff = wg.shape
    wspec = lambda shp: pl.BlockSpec(shp, lambda i, te, tv: (te[i], 0, 0))
    return pl.pallas_call(
        _experts_kernel,
        out_shape=jax.ShapeDtypeStruct(xs2d.shape, F32),
        grid_spec=pltpu.PrefetchScalarGridSpec(
            num_scalar_prefetch=2,
            grid=(n_tiles,),
            in_specs=[
                pl.BlockSpec((tmg * SUBLANE, LANE), lambda i, te, tv: (i, 0)),
                wspec((1, d, ff)), wspec((1, 1, ff)),
                wspec((1, d, ff)), wspec((1, 1, ff)),
                wspec((1, ff, d)), wspec((1, 1, d)),
            ],
            out_specs=pl.BlockSpec((tmg * SUBLANE, LANE), lambda i, te, tv: (i, 0)),
        ),
        compiler_params=_cparams("arbitrary"),
        name="experts",
    )(tile_expert, tile_valid, xs2d, wg, bg, wu, bu, wd, bd)


def _combine_kernel(pos_ref, y_ref, tw_ref, xmid_ref, g2_ref, nf_ref, o_ref, ybuf, sem):
    sb, rb, d = xmid_ref.shape
    tc = sb * rb

    def issue(t, c):
        for k in range(TOP_K):
            dst = pl.multiple_of((k * tc + t) * SUBLANE, SUBLANE)
            pltpu.make_async_copy(y_ref.at[pos_ref[k, t]], ybuf.at[pl.ds(dst, SUBLANE)], sem).start()
        return c

    lax.fori_loop(0, tc, issue, 0)

    def drain(t, c):
        for k in range(TOP_K):
            pltpu.make_async_copy(y_ref.at[0], ybuf.at[pl.ds(0, SUBLANE)], sem).wait()
        return c

    lax.fori_loop(0, tc, drain, 0)

    tw = tw_ref[...]
    moe = jnp.zeros((tc, d), F32)
    for k in range(TOP_K):
        yk = jnp.concatenate(
            [ybuf[pl.ds(k * tc * SUBLANE + j, tc, stride=SUBLANE), :] for j in range(d // LANE)], axis=1)
        moe = moe + tw[:, k:k + 1] * yk
    out = xmid_ref[...] + g2_ref[...] * moe.reshape(sb, rb, d)
    ms = jnp.mean(out * out, axis=-1, keepdims=True)
    o_ref[...] = out * lax.rsqrt(ms + EPS) * nf_ref[...]


def _combine(pos, y3d, tw, xmid, g2, nf, sb, rb):
    nseq, L, d = xmid.shape
    tc = sb * rb
    nt = L // rb
    return pl.pallas_call(
        _combine_kernel,
        out_shape=jax.ShapeDtypeStruct((nseq, L, d), F32),
        grid=(nseq // sb, nt),
        in_specs=[
            pl.BlockSpec((TOP_K, tc), lambda s, t: (0, s * nt + t), memory_space=pltpu.SMEM),
            pl.BlockSpec(memory_space=pl.ANY),
            pl.BlockSpec((tc, TOP_K), lambda s, t: (s * nt + t, 0)),
            pl.BlockSpec((sb, rb, d), lambda s, t: (s, t, 0)),
            pl.BlockSpec((sb, 1, d), lambda s, t: (s, 0, 0)),
            pl.BlockSpec((1, d), lambda s, t: (0, 0)),
        ],
        out_specs=pl.BlockSpec((sb, rb, d), lambda s, t: (s, t, 0)),
        scratch_shapes=[pltpu.VMEM((TOP_K * tc * SUBLANE, LANE), F32), pltpu.SemaphoreType.DMA],
        compiler_params=_cparams("arbitrary", "arbitrary"),
        name="combine",
    )(pos, y3d, tw, xmid, g2, nf)


def _tile_rows(nseq, L, tile):
    if L >= tile:
        assert L % tile == 0
        return 1, tile
    assert tile % L == 0 and nseq % (tile // L) == 0
    return tile // L, L


def kernel(x_prompt, x_sample, state_gla, c_prompt, c_sample, w_ada, b_ada, norm1, w_in, w_gk, b_gk, gla_norm,
           gmlp_ln_g, gmlp_ln_b, gmlp_w_s, gmlp_b_s, w_out, norm2, w_router, b_router, w_gate, b_gate, w_up,
           b_up, w_down, b_down, norm_f):
    depth = w_ada.shape[0]
    assert depth == 1
    bp, lp, d = x_prompt.shape
    bs, ls, _ = x_sample.shape
    tp, ts = bp * lp, bs * ls

    nc = bp + bs
    ncp = -(-nc // SUBLANE) * SUBLANE
    c_all = jnp.concatenate([c_prompt, c_sample, jnp.zeros((ncp - nc, d), F32)], axis=0)
    mod = _ada(c_all, w_ada[0], b_ada[0][None]).reshape(ncp, N_MOD, 1, d)
    mods_p = [mod[:bp, i] for i in range(N_MOD)]
    mods_s = [mod[bp:nc, i] for i in range(N_MOD)]

    wi = w_in[0]
    c_lr = 2 * GLA_QK + GLA_WIDTH
    c_r = c_lr + GLA_LOWRANK
    wm = jnp.concatenate([wi[:, :c_lr], wi[:, c_r:]], axis=1).astype(BF16)
    wlr = jnp.pad(wi[:, c_lr:c_r], ((0, 0), (0, LANE - GLA_LOWRANK))).astype(BF16)
    wgk = jnp.pad(w_gk[0], ((0, LANE - GLA_LOWRANK), (0, 0)))
    bgk = b_gk[0][None]
    n1, n2, nf = norm1[0][None], norm2[0][None], norm_f[None]
    gn, lng, lnb = gla_norm[0][None], gmlp_ln_g[0][None], gmlp_ln_b[0][None]
    ws, bsv = gmlp_w_s[0], gmlp_b_s[0]
    pos_i = jnp.arange(GMLP_BLOCK)
    cmask = (pos_i[None, :] // CHUNK) <= (pos_i[:, None] // CHUNK)
    wm_p = jnp.where(cmask[None], ws, 0.0).astype(BF16)
    bsb_p = jnp.repeat(bsv.T, GMLP_GC, axis=1)
    reps = ROW_BLOCK // ls
    eye = jnp.eye(reps, dtype=F32)
    wm_s = jnp.einsum("ab,gij->gaibj", eye, ws[:, :ls, :ls]).reshape(GMLP_GROUPS, ROW_BLOCK, ROW_BLOCK).astype(BF16)
    bsb_s = jnp.tile(jnp.repeat(bsv[:, :ls].T, GMLP_GC, axis=1), (reps, 1))
    wo = w_out[0].astype(BF16)
    wr = jnp.pad(w_router[0], ((0, 0), (0, LANE - N_EXPERTS)))
    br = jnp.concatenate([b_router[0], jnp.full((LANE - N_EXPERTS,), -1e30, F32)])[None]
    upper = (jnp.arange(TOKEN_TILE)[:, None] < jnp.arange(TOKEN_TILE)[None, :]).astype(BF16)
    wg, wu, wd = w_gate[0].astype(BF16), w_up[0].astype(BF16), w_down[0].astype(BF16)
    bg, bu, bd = b_gate[0][:, None], b_up[0][:, None], b_down[0][:, None]

    sbp, rbp = _tile_rows(bp, lp, TOKEN_TILE)
    sbs, rbs = _tile_rows(bs, ls, TOKEN_TILE)

    proj_p, gk_p = _inproj(x_prompt, mods_p[1], mods_p[0], n1, wm, wlr, wgk, bgk, sbp, rbp)
    proj_s, gk_s = _inproj(x_sample, mods_s[1], mods_s[0], n1, wm, wlr, wgk, bgk, sbs, rbs)
    mix_p, state_p = _mixer_prompt(proj_p, gk_p, gn, lng, lnb, wm_p, bsb_p, TOKEN_TILE)
    mix_s, state_s, vn_s = _mixer_sample(proj_s, gk_s, state_gla[0], gn, lng, lnb, wm_s, bsb_s)

    cnt0 = jnp.zeros((N_EXPERTS, LANE), F32)
    xmid_p, h2_p, topi_p, topw_p, rank_p, cnt_p = _outproj(
        mix_p, x_prompt, mods_p[2], mods_p[4], mods_p[3], n2, wo, wr, br, upper, cnt0, sbp, rbp)
    xmid_s, h2_s, topi_s, topw_s, rank_s, cnt_s = _outproj(
        mix_s, x_sample, mods_s[2], mods_s[4], mods_s[3], n2, wo, wr, br, upper, cnt_p, sbs, rbs)

    tmg = EXPERT_TILE
    cnt = cnt_s[:, 0].astype(jnp.int32)
    tiles_e = (cnt + tmg - 1) // tmg
    tile_end = jnp.cumsum(tiles_e)
    tile_start = tile_end - tiles_e
    row_start = tile_start * tmg
    n_tiles = (TOP_K * (tp + ts)) // tmg + N_EXPERTS
    tid = jnp.arange(n_tiles, dtype=jnp.int32)
    eids = jnp.arange(N_EXPERTS, dtype=jnp.int32)
    te = jnp.minimum(jnp.sum((tid[:, None] >= tile_end[None, :]).astype(jnp.int32), axis=1), N_EXPERTS - 1)
    te_hot = te[:, None] == eids[None, :]
    cnt_te = jnp.sum(jnp.where(te_hot, cnt[None, :], 0), axis=1)
    start_te = jnp.sum(jnp.where(te_hot, tile_start[None, :], 0), axis=1)
    tv = jnp.clip(cnt_te - (tid - start_te) * tmg, 0, tmg)
    tv = jnp.where(tid < tile_end[-1], tv, 0).astype(jnp.int32)
    last_e = jnp.max(jnp.where(tiles_e > 0, eids, 0)).astype(jnp.int32)
    te = jnp.where(tid < tile_end[-1], te, last_e).astype(jnp.int32)

    def slot_of(topi, rank):
        hot = topi[:, :, None] == eids[None, None, :]
        return (jnp.sum(jnp.where(hot, row_start[None, None, :], 0), axis=2) + rank).astype(jnp.int32)

    pos_p = slot_of(topi_p, rank_p)
    pos_s = slot_of(topi_s, rank_s)

    n_rows = n_tiles * tmg
    xs = _dispatch(pos_p, h2_p, None, n_rows, MOVE_TILE)
    xs = _dispatch(pos_s, h2_s, xs, n_rows, MOVE_TILE)
    y = _experts(te, tv, xs.reshape(n_rows * SUBLANE, LANE), wg, bg, wu, bu, wd, bd, tmg)
    y3 = y.reshape(n_rows, SUBLANE, LANE)
    sbc, rbc = _tile_rows(bp, lp, MOVE_TILE)
    y_prompt = _combine(pos_p, y3, topw_p.T, xmid_p, mods_p[5], nf, sbc, rbc)
    sbc, rbc = _tile_rows(bs, ls, MOVE_TILE)
    y_sample = _combine(pos_s, y3, topw_s.T, xmid_s, mods_s[5], nf, sbc, rbc)

    return (y_prompt, y_sample, state_p[None], state_s[None], vn_s[None])
```

```python
import functools

import jax
import jax.numpy as jnp
from jax import lax
from jax.experimental import pallas as pl
from jax.experimental.pallas import tpu as pltpu

F32 = jnp.float32
BF16 = jnp.bfloat16
HIGHEST = lax.Precision.HIGHEST

CHUNK = 64
GLA_HEADS = 4
GLA_DK = 64
GLA_DV = 128
GLA_QK = GLA_HEADS * GLA_DK
GLA_WIDTH = GLA_HEADS * GLA_DV
GLA_LOWRANK = 16
GLA_GATE_NORM = 16.0
GMLP_WIDTH = 512
GMLP_GROUPS = 4
GMLP_GC = GMLP_WIDTH // GMLP_GROUPS
GMLP_BLOCK = 128
N_EXPERTS = 32
TOP_K = 4
SWIGLU_LIMIT = 7.0
SWIGLU_ALPHA = 1.702
EPS = 1e-6
N_MOD = 6

LANE = 128
SUBLANE = 8

Q0 = 0
K0 = Q0 + GLA_QK
V0 = K0 + GLA_QK
R0 = V0 + GLA_WIDTH
U0 = R0 + GLA_WIDTH
G0 = U0 + GMLP_WIDTH
PROJ_COLS = G0 + GMLP_WIDTH

ROW_BLOCK = 128
TOKEN_TILE = 512
EXPERT_TILE = 512
FF_CHUNK = 512
ROW_UNIT = 16
LOCAL_ROWS = 2560
PERM_CHUNK = 256
VMEM_LIMIT = 48 * 1024 * 1024
EXPERTS_VMEM_LIMIT = 56 * 1024 * 1024


def _cparams(*sem):
    return pltpu.CompilerParams(dimension_semantics=sem, vmem_limit_bytes=VMEM_LIMIT)


def _dot(a, b, **kw):
    return jnp.dot(a, b, preferred_element_type=F32, **kw)


def _dot_nt(a, b):
    return lax.dot_general(a, b, (((1,), (1,)), ((), ())), preferred_element_type=F32)


def _gelu(x):
    return 0.5 * x * (1.0 + lax.erf(x * (0.5 ** 0.5)))


def _dot_tn(a, b):
    return lax.dot_general(a, b, (((0,), (0,)), ((), ())), preferred_element_type=F32)


def _ada_kernel(c_ref, w_ref, b_ref, o_ref):
    c = c_ref[...]
    s = c * jax.nn.sigmoid(c)
    o_ref[...] = _dot(s, w_ref[...], precision=HIGHEST) + b_ref[...]


def _ada(c, w_ada, b_ada):
    n, d = c.shape
    return pl.pallas_call(
        _ada_kernel,
        out_shape=jax.ShapeDtypeStruct((n, N_MOD * d), F32),
        grid=(N_MOD,),
        in_specs=[
            pl.BlockSpec((n, d), lambda j: (0, 0)),
            pl.BlockSpec((d, d), lambda j: (0, j)),
            pl.BlockSpec((1, d), lambda j: (0, j)),
        ],
        out_specs=pl.BlockSpec((n, d), lambda j: (0, j)),
        compiler_params=_cparams("arbitrary"),
        name="ada",
    )(c, w_ada, b_ada)


def _inproj_kernel(x_ref, sc_ref, sh_ref, n1_ref, wm_ref, wlr_ref, wgk_ref, bgk_ref, proj_ref, gk_ref):
    sb, rb, d = x_ref.shape
    x = x_ref[...]
    ms = jnp.mean(x * x, axis=-1, keepdims=True)
    h = x * lax.rsqrt(ms + EPS) * n1_ref[...]
    h = h * (1.0 + sc_ref[...]) + sh_ref[...]
    hb = h.reshape(sb * rb, d).astype(BF16)
    cw = 512
    for c in range(PROJ_COLS // cw):
        p = _dot(hb, wm_ref[:, c * cw:(c + 1) * cw])
        proj_ref[:, :, c * cw:(c + 1) * cw] = p.astype(BF16).reshape(sb, rb, cw)
    lr = _dot(hb, wlr_ref[...])
    gk = _dot(lr, wgk_ref[...], precision=HIGHEST) + bgk_ref[...]
    gk_ref[...] = gk.reshape(sb, rb, GLA_QK)


def _inproj(x, sc, sh, n1, wm, wlr, wgk, bgk, sb, rb):
    nseq, L, d = x.shape
    grid = (nseq // sb, L // rb)
    const = lambda s, t: (0, 0)
    return pl.pallas_call(
        _inproj_kernel,
        out_shape=(jax.ShapeDtypeStruct((nseq, L, PROJ_COLS), BF16),
                   jax.ShapeDtypeStruct((nseq, L, GLA_QK), F32)),
        grid=grid,
        in_specs=[
            pl.BlockSpec((sb, rb, d), lambda s, t: (s, t, 0)),
            pl.BlockSpec((sb, 1, d), lambda s, t: (s, 0, 0)),
            pl.BlockSpec((sb, 1, d), lambda s, t: (s, 0, 0)),
            pl.BlockSpec((1, d), const),
            pl.BlockSpec((d, PROJ_COLS), const),
            pl.BlockSpec((d, LANE), const),
            pl.BlockSpec((LANE, GLA_QK), const),
            pl.BlockSpec((1, GLA_QK), const),
        ],
        out_specs=(pl.BlockSpec((sb, rb, PROJ_COLS), lambda s, t: (s, t, 0)),
                   pl.BlockSpec((sb, rb, GLA_QK), lambda s, t: (s, t, 0))),
        compiler_params=_cparams("arbitrary", "arbitrary"),
        name="inproj",
    )(x, sc, sh, n1, wm, wlr, wgk, bgk)


def _head_masks():
    lane = lax.broadcasted_iota(jnp.int32, (1, GLA_QK), 1)
    return [(lane // GLA_DK) == h for h in range(GLA_HEADS)]


def _mixer_block(p, gkpre, states, c_len, gn, lng, lnb, wm_ref, bsb):
    rows = p.shape[0]
    n_chunks = rows // c_len
    chained = len(states) == 1
    hm = _head_masks()

    q = p[:, Q0:Q0 + GLA_QK].astype(F32)
    k = p[:, K0:K0 + GLA_QK].astype(F32)
    v = p[:, V0:V0 + GLA_WIDTH]
    r = p[:, R0:R0 + GLA_WIDTH].astype(F32)

    g = jax.nn.log_sigmoid(gkpre) / GLA_GATE_NORM
    ri = lax.broadcasted_iota(jnp.int32, (rows, rows), 0)
    ci = lax.broadcasted_iota(jnp.int32, (rows, rows), 1)
    tri = ((ci <= ri) & ((ci // c_len) == (ri // c_len))).astype(F32)
    G = _dot(tri, g, precision=HIGHEST)
    qe = (q * (GLA_DK ** -0.5)) * jnp.exp(G)
    ke = (k * jnp.exp(-G)).astype(BF16)

    ti = lax.broadcasted_iota(jnp.int32, (GLA_HEADS * c_len, c_len), 0) % c_len
    si = lax.broadcasted_iota(jnp.int32, (GLA_HEADS * c_len, c_len), 1)
    causal = si <= ti

    new_states = []
    o_rows = []
    st = states[0]
    for c in range(n_chunks):
        lo, hi = c * c_len, (c + 1) * c_len
        if not chained:
            st = states[c]
        qe_c = qe[lo:hi]
        q4 = jnp.concatenate([jnp.where(hm[h], qe_c, 0.0) for h in range(GLA_HEADS)], axis=0).astype(BF16)
        a = jnp.where(causal, _dot_nt(q4, ke[lo:hi]), 0.0).astype(BF16)
        o_inter = _dot_nt(q4, st.astype(BF16))
        v_c = v[lo:hi]
        heads = []
        for h in range(GLA_HEADS):
            o_h = o_inter[h * c_len:(h + 1) * c_len] + _dot(
                a[h * c_len:(h + 1) * c_len], v_c[:, h * GLA_DV:(h + 1) * GLA_DV])
            heads.append(o_h)
        o_rows.append(jnp.concatenate(heads, axis=1))
        g_last = G[hi - 1:hi]
        kd = (k[lo:hi] * jnp.exp(g_last - G[lo:hi])).astype(BF16)
        upd = _dot_tn(v_c, kd)
        st_new = jnp.exp(g_last) * st
        for h in range(GLA_HEADS):
            st_new = st_new + jnp.where(hm[h], upd[h * GLA_DV:(h + 1) * GLA_DV], 0.0)
        if chained:
            st = st_new
        else:
            new_states.append(st_new)
    if chained:
        new_states = [st]
    o = jnp.concatenate(o_rows, axis=0)

    gla = []
    for h in range(GLA_HEADS):
        o_h = o[:, h * GLA_DV:(h + 1) * GLA_DV]
        ms = jnp.mean(o_h * o_h, axis=-1, keepdims=True)
        r_h = r[:, h * GLA_DV:(h + 1) * GLA_DV]
        gla.append(o_h * lax.rsqrt(ms + EPS) * gn * (r_h * jax.nn.sigmoid(r_h)))

    u = _gelu(p[:, U0:U0 + GMLP_WIDTH].astype(F32))
    vv = _gelu(p[:, G0:G0 + GMLP_WIDTH].astype(F32))
    mu = jnp.mean(vv, axis=-1, keepdims=True)
    xc = vv - mu
    var = jnp.mean(xc * xc, axis=-1, keepdims=True)
    vn = xc * lax.rsqrt(var + EPS) * lng + lnb
    vnb = vn.astype(BF16)
    gm = []
    for gi in range(GMLP_GROUPS):
        sl = slice(gi * GMLP_GC, (gi + 1) * GMLP_GC)
        mixed = _dot(wm_ref[gi], vnb[:, sl]) + bsb[:, sl]
        gm.append(u[:, sl] * mixed)
    out = jnp.concatenate(gla + gm, axis=1)
    return out, new_states, vn


def _mixer_prompt_kernel(proj_ref, gk_ref, gn_ref, lng_ref, lnb_ref, wm_ref, bsb_ref, mix_ref, s_ref, st_scr):
    t = pl.program_id(1)
    nt = pl.num_programs(1)

    @pl.when(t == 0)
    def _():
        st_scr[...] = jnp.zeros_like(st_scr)

    n_sub = proj_ref.shape[1] // ROW_BLOCK

    def body(i, carry):
        r0 = pl.multiple_of(i * ROW_BLOCK, ROW_BLOCK)
        p = proj_ref[0, pl.ds(r0, ROW_BLOCK), :]
        gk = gk_ref[0, pl.ds(r0, ROW_BLOCK), :]
        out, sts, _ = _mixer_block(p, gk, [st_scr[...]], CHUNK, gn_ref[...], lng_ref[...], lnb_ref[...],
                                   wm_ref, bsb_ref[...])
        st_scr[...] = sts[0]
        mix_ref[0, pl.ds(r0, ROW_BLOCK), :] = out.astype(BF16)
        return carry

    lax.fori_loop(0, n_sub, body, 0)

    @pl.when(t == nt - 1)
    def _():
        s_ref[0] = st_scr[...].T.reshape(GLA_HEADS, GLA_DK, GLA_DV)


def _mixer_prompt(proj, gk, gn, lng, lnb, wm, bsb, tb):
    b, L, _ = proj.shape
    const2 = lambda s, t: (0, 0)
    return pl.pallas_call(
        _mixer_prompt_kernel,
        out_shape=(jax.ShapeDtypeStruct((b, L, 2 * GLA_WIDTH), BF16),
                   jax.ShapeDtypeStruct((b, GLA_HEADS, GLA_DK, GLA_DV), F32)),
        grid=(b, L // tb),
        in_specs=[
            pl.BlockSpec((1, tb, PROJ_COLS), lambda s, t: (s, t, 0)),
            pl.BlockSpec((1, tb, GLA_QK), lambda s, t: (s, t, 0)),
            pl.BlockSpec((1, GLA_DV), const2),
            pl.BlockSpec((1, GMLP_WIDTH), const2),
            pl.BlockSpec((1, GMLP_WIDTH), const2),
            pl.BlockSpec((GMLP_GROUPS, ROW_BLOCK, ROW_BLOCK), lambda s, t: (0, 0, 0)),
            pl.BlockSpec((ROW_BLOCK, GMLP_WIDTH), const2),
        ],
        out_specs=(pl.BlockSpec((1, tb, 2 * GLA_WIDTH), lambda s, t: (s, t, 0)),
                   pl.BlockSpec((1, GLA_HEADS, GLA_DK, GLA_DV), lambda s, t: (s, 0, 0, 0))),
        scratch_shapes=[pltpu.VMEM((GLA_DV, GLA_QK), F32)],
        compiler_params=_cparams("arbitrary", "arbitrary"),
        name="mixer_prompt",
    )(proj, gk, gn, lng, lnb, wm, bsb)


def _mixer_sample_kernel(proj_ref, gk_ref, s0_ref, gn_ref, lng_ref, lnb_ref, wm_ref, bsb_ref,
                         mix_ref, s_ref, vn_ref):
    sb, rb, _ = proj_ref.shape
    p = proj_ref[...].reshape(sb * rb, PROJ_COLS)
    gk = gk_ref[...].reshape(sb * rb, GLA_QK)
    states = [s0_ref[i].reshape(GLA_QK, GLA_DV).T for i in range(sb)]
    out, sts, vn = _mixer_block(p, gk, states, rb, gn_ref[...], lng_ref[...], lnb_ref[...], wm_ref, bsb_ref[...])
    mix_ref[...] = out.astype(BF16).reshape(sb, rb, 2 * GLA_WIDTH)
    vn_ref[...] = vn.reshape(sb, rb, GMLP_WIDTH)
    for i in range(sb):
        s_ref[i] = sts[i].T.reshape(GLA_HEADS, GLA_DK, GLA_DV)


def _mixer_sample(proj, gk, s0, gn, lng, lnb, wm, bsb):
    n, L, _ = proj.shape
    sb = ROW_BLOCK // L
    const2 = lambda s: (0, 0)
    return pl.pallas_call(
        _mixer_sample_kernel,
        out_shape=(jax.ShapeDtypeStruct((n, L, 2 * GLA_WIDTH), BF16),
                   jax.ShapeDtypeStruct((n, GLA_HEADS, GLA_DK, GLA_DV), F32),
                   jax.ShapeDtypeStruct((n, L, GMLP_WIDTH), F32)),
        grid=(n // sb,),
        in_specs=[
            pl.BlockSpec((sb, L, PROJ_COLS), lambda s: (s, 0, 0)),
            pl.BlockSpec((sb, L, GLA_QK), lambda s: (s, 0, 0)),
            pl.BlockSpec((sb, GLA_HEADS, GLA_DK, GLA_DV), lambda s: (s, 0, 0, 0)),
            pl.BlockSpec((1, GLA_DV), const2),
            pl.BlockSpec((1, GMLP_WIDTH), const2),
            pl.BlockSpec((1, GMLP_WIDTH), const2),
            pl.BlockSpec((GMLP_GROUPS, ROW_BLOCK, ROW_BLOCK), lambda s: (0, 0, 0)),
            pl.BlockSpec((ROW_BLOCK, GMLP_WIDTH), const2),
        ],
        out_specs=(pl.BlockSpec((sb, L, 2 * GLA_WIDTH), lambda s: (s, 0, 0)),
                   pl.BlockSpec((sb, GLA_HEADS, GLA_DK, GLA_DV), lambda s: (s, 0, 0, 0)),
                   pl.BlockSpec((sb, L, GMLP_WIDTH), lambda s: (s, 0, 0))),
        compiler_params=_cparams("arbitrary"),
        name="mixer_sample",
    )(proj, gk, s0, gn, lng, lnb, wm, bsb)


def _outproj_kernel(mix_ref, x_ref, g1_ref, sc_ref, sh_ref, n2_ref, wo_ref, wr_ref, br_ref, upper_ref, lower_ref,
                    xmid_ref, h2_ref, topw_ref, slot_ref, n16_ref):
    sb, rb, d = x_ref.shape
    tm = sb * rb

    y = _dot(mix_ref[...].reshape(tm, d), wo_ref[...])
    xm = x_ref[...] + g1_ref[...] * y.reshape(sb, rb, d)
    xmid_ref[...] = xm
    ms = jnp.mean(xm * xm, axis=-1, keepdims=True)
    h2 = xm * lax.rsqrt(ms + EPS) * n2_ref[...]
    h2 = (h2 * (1.0 + sc_ref[...]) + sh_ref[...]).reshape(tm, d)
    h2_ref[...] = h2.astype(BF16)

    logits = _dot(h2, wr_ref[...], precision=HIGHEST) + br_ref[...]
    l = logits.T[:N_EXPERTS]
    eid = lax.broadcasted_iota(jnp.int32, (N_EXPERTS, tm), 0).astype(F32)
    top_l, sel = [], []
    for _ in range(TOP_K):
        m = jnp.max(l, axis=0, keepdims=True)
        idx = jnp.min(jnp.where(l == m, eid, float(N_EXPERTS)), axis=0, keepdims=True)
        hit = eid == idx
        top_l.append(m)
        sel.append(hit)
        l = jnp.where(hit, -jnp.inf, l)
    ex = [jnp.exp(t - top_l[0]) for t in top_l]
    den = ex[0] + ex[1] + ex[2] + ex[3]
    topw_ref[...] = jnp.concatenate([e / den for e in ex], axis=0)

    chosen = (sel[0] | sel[1] | sel[2] | sel[3])
    cb = jnp.where(chosen, 1.0, 0.0)
    before = _dot(cb.astype(BF16), upper_ref[...])
    n = jnp.sum(cb, axis=1, keepdims=True)
    n16 = jnp.floor((n + (ROW_UNIT - 1)) * (1.0 / ROW_UNIT)) * ROW_UNIT
    n16b = jnp.broadcast_to(n16, (N_EXPERTS, LANE))
    ls = _dot(lower_ref[...], n16b, precision=HIGHEST)[:, 0:1]
    base = before + ls
    slot_ref[...] = jnp.concatenate(
        [jnp.sum(jnp.where(s, base, 0.0), axis=0, keepdims=True) for s in sel], axis=0).astype(jnp.int32)
    n16_ref[0] = n16b


def _outproj(mix, x, g1, sc2, sh2, n2, wo, wr, br, upper, lower, sb, rb):
    nseq, L, d = x.shape
    tm = sb * rb
    nt = L // rb
    T = nseq * L
    grid = (nseq // sb, nt)
    const = lambda s, t: (0, 0)
    tok = lambda s, t: (0, s * nt + t)
    return pl.pallas_call(
        _outproj_kernel,
        out_shape=(jax.ShapeDtypeStruct((nseq, L, d), F32),
                   jax.ShapeDtypeStruct((T, d), BF16),
                   jax.ShapeDtypeStruct((TOP_K, T), F32),
                   jax.ShapeDtypeStruct((TOP_K, T), jnp.int32),
                   jax.ShapeDtypeStruct((T // tm, N_EXPERTS, LANE), F32)),
        grid=grid,
        in_specs=[
            pl.BlockSpec((sb, rb, d), lambda s, t: (s, t, 0)),
            pl.BlockSpec((sb, rb, d), lambda s, t: (s, t, 0)),
            pl.BlockSpec((sb, 1, d), lambda s, t: (s, 0, 0)),
            pl.BlockSpec((sb, 1, d), lambda s, t: (s, 0, 0)),
            pl.BlockSpec((sb, 1, d), lambda s, t: (s, 0, 0)),
            pl.BlockSpec((1, d), const),
            pl.BlockSpec((d, d), const),
            pl.BlockSpec((d, LANE), const),
            pl.BlockSpec((1, LANE), const),
            pl.BlockSpec((tm, tm), const),
            pl.BlockSpec((N_EXPERTS, N_EXPERTS), const),
        ],
        out_specs=(pl.BlockSpec((sb, rb, d), lambda s, t: (s, t, 0)),
                   pl.BlockSpec((tm, d), lambda s, t: (s * nt + t, 0)),
                   pl.BlockSpec((TOP_K, tm), tok),
                   pl.BlockSpec((TOP_K, tm), tok),
                   pl.BlockSpec((1, N_EXPERTS, LANE), lambda s, t: (s * nt + t, 0, 0))),
        compiler_params=_cparams("arbitrary", "arbitrary"),
        name="outproj",
    )(mix, x, g1, sc2, sh2, n2, wo, wr, br, upper, lower)


def _unit_copy(src_ref, src_row, dst_ref, dst_row, sem):
    return pltpu.make_async_copy(src_ref.at[pl.ds(pl.multiple_of(src_row, ROW_UNIT), ROW_UNIT)],
                                 dst_ref.at[pl.ds(pl.multiple_of(dst_row, ROW_UNIT), ROW_UNIT)], sem)


def _dispatch_body(udst_ref, nun_ref, slot_ref, h2_ref, xs_ref, xloc, sem):
    i = pl.program_id(0)
    tb = slot_ref.shape[1]
    nun = nun_ref[i]
    s = slot_ref[...]
    h2 = h2_ref[...]
    def sort_chunk(c):
        r = lax.broadcasted_iota(jnp.int32, (PERM_CHUNK, tb), 0) + c * PERM_CHUNK
        hit = (s[0:1] == r) | (s[1:2] == r) | (s[2:3] == r) | (s[3:4] == r)
        p = jnp.where(hit, 1.0, 0.0).astype(BF16)
        xloc[c * PERM_CHUNK:(c + 1) * PERM_CHUNK, :] = _dot(p, h2).astype(BF16)

    n_chunks = LOCAL_ROWS // PERM_CHUNK
    for c in range(n_chunks - 1):
        sort_chunk(c)
    pl.when((n_chunks - 1) * (PERM_CHUNK // ROW_UNIT) < nun)(functools.partial(sort_chunk, n_chunks - 1))

    u0 = i * (LOCAL_ROWS // ROW_UNIT)

    def issue(u, carry):
        _unit_copy(xloc, u * ROW_UNIT, xs_ref, udst_ref[u0 + u], sem).start()
        return carry

    lax.fori_loop(0, nun, issue, 0)

    def drain(u, carry):
        _unit_copy(xloc, 0, xs_ref, 0, sem).wait()
        return carry

    lax.fori_loop(0, nun, drain, 0)


def _dispatch_first_kernel(udst_ref, nun_ref, slot_ref, h2_ref, xs_ref, xloc, sem):
    _dispatch_body(udst_ref, nun_ref, slot_ref, h2_ref, xs_ref, xloc, sem)


def _dispatch_next_kernel(udst_ref, nun_ref, slot_ref, h2_ref, xs_in_ref, xs_ref, xloc, sem):
    del xs_in_ref
    _dispatch_body(udst_ref, nun_ref, slot_ref, h2_ref, xs_ref, xloc, sem)


def _dispatch(udst, nun, slot, h2, xs, n_rows, tb):
    T, d = h2.shape
    any_spec = pl.BlockSpec(memory_space=pl.ANY)
    in_specs = [pl.BlockSpec((TOP_K, tb), lambda i, *_: (0, i)),
                pl.BlockSpec((tb, d), lambda i, *_: (i, 0))]
    aliases = {}
    body = _dispatch_first_kernel
    args = (udst, nun, slot, h2)
    if xs is not None:
        in_specs.append(any_spec)
        aliases = {4: 0}
        body = _dispatch_next_kernel
        args = args + (xs,)
    return pl.pallas_call(
        body,
        out_shape=jax.ShapeDtypeStruct((n_rows, d), BF16),
        grid_spec=pltpu.PrefetchScalarGridSpec(
            num_scalar_prefetch=2, grid=(T // tb,), in_specs=in_specs, out_specs=any_spec,
            scratch_shapes=[pltpu.VMEM((LOCAL_ROWS, d), BF16), pltpu.SemaphoreType.DMA]),
        input_output_aliases=aliases,
        compiler_params=pltpu.CompilerParams(dimension_semantics=("arbitrary",), has_side_effects=True,
                                             vmem_limit_bytes=VMEM_LIMIT),
        name="dispatch_next" if xs is not None else "dispatch_first",
    )(*args)


def _experts_kernel(te_ref, tv_ref, tf_ref, xs_ref, wg_ref, bg_ref, wu_ref, bu_ref, wd_ref, bd_ref, y_ref,
                    wgb, wub, wdb):
    i = pl.program_id(0)
    valid = tv_ref[i]
    tm, d = xs_ref.shape
    ff = wg_ref.shape[2]

    @pl.when(tf_ref[i] == 1)
    def _():
        wgb[...] = wg_ref[0].astype(BF16)
        wub[...] = wu_ref[0].astype(BF16)
        wdb[...] = wd_ref[0].astype(BF16)

    @pl.when(valid > 0)
    def _():
        row = lax.broadcasted_iota(jnp.int32, (tm, 1), 0)
        xb = jnp.where(row < valid, xs_ref[...], jnp.zeros((), BF16))
        acc = jnp.zeros((tm, d), F32)
        for c in range(ff // FF_CHUNK):
            cs = slice(c * FF_CHUNK, (c + 1) * FF_CHUNK)
            gate = jnp.minimum(_dot(xb, wgb[:, cs]) + bg_ref[0, :, cs], SWIGLU_LIMIT)
            up = jnp.clip(_dot(xb, wub[:, cs]) + bu_ref[0, :, cs], -SWIGLU_LIMIT, SWIGLU_LIMIT)
            act = (up + 1.0) * gate * jax.nn.sigmoid(SWIGLU_ALPHA * gate)
            acc = acc + _dot(act.astype(BF16), wdb[cs, :])
        y_ref[...] = (acc + bd_ref[0]).astype(BF16)

    @pl.when(valid == 0)
    def _():
        y_ref[...] = jnp.zeros_like(y_ref)


def _experts(tile_expert, tile_valid, tile_first, xs, wg, bg, wu, bu, wd, bd, tmg):
    n_tiles = tile_expert.shape[0]
    _, d, ff = wg.shape
    wspec = lambda shp: pl.BlockSpec(shp, lambda i, te, tv, tf: (te[i], 0, 0))
    return pl.pallas_call(
        _experts_kernel,
        out_shape=jax.ShapeDtypeStruct(xs.shape, BF16),
        grid_spec=pltpu.PrefetchScalarGridSpec(
            num_scalar_prefetch=3,
            grid=(n_tiles,),
            in_specs=[
                pl.BlockSpec((tmg, d), lambda i, te, tv, tf: (i, 0)),
                wspec((1, d, ff)), wspec((1, 1, ff)),
                wspec((1, d, ff)), wspec((1, 1, ff)),
                wspec((1, ff, d)), wspec((1, 1, d)),
            ],
            out_specs=pl.BlockSpec((tmg, d), lambda i, te, tv, tf: (i, 0)),
            scratch_shapes=[pltpu.VMEM((d, ff), BF16), pltpu.VMEM((d, ff), BF16), pltpu.VMEM((ff, d), BF16)],
        ),
        compiler_params=pltpu.CompilerParams(dimension_semantics=("arbitrary",), vmem_limit_bytes=EXPERTS_VMEM_LIMIT),
        name="experts",
    )(tile_expert, tile_valid, tile_first, xs, wg, bg, wu, bu, wd, bd)


def _combine_kernel(udst_ref, nun_ref, y_ref, slot_ref, tw_ref, xmid_ref, g2_ref, nf_ref, o_ref, ybuf, sem):
    sb, rb, d = xmid_ref.shape
    tb = sb * rb
    i = pl.program_id(0) * pl.num_programs(1) + pl.program_id(1)
    nun = nun_ref[i]

    @pl.when(i == 0)
    def _():
        ybuf[...] = jnp.zeros_like(ybuf)

    u0 = i * (LOCAL_ROWS // ROW_UNIT)

    def issue(u, carry):
        _unit_copy(y_ref, udst_ref[u0 + u], ybuf, u * ROW_UNIT, sem).start()
        return carry

    lax.fori_loop(0, nun, issue, 0)

    def drain(u, carry):
        _unit_copy(y_ref, 0, ybuf, 0, sem).wait()
        return carry

    lax.fori_loop(0, nun, drain, 0)

    st = slot_ref[...]
    tw = tw_ref[...]
    lane = lax.broadcasted_iota(jnp.int32, (tb, PERM_CHUNK), 1)
    sk = [jnp.broadcast_to(st[:, k:k + 1], (tb, PERM_CHUNK)) for k in range(TOP_K)]
    wk = [jnp.broadcast_to(tw[:, k:k + 1], (tb, PERM_CHUNK)) for k in range(TOP_K)]
    moe = jnp.zeros((tb, d), F32)
    for c in range(LOCAL_ROWS // PERM_CHUNK):
        r = lane + c * PERM_CHUNK
        pw = jnp.zeros((tb, PERM_CHUNK), F32)
        for k in range(TOP_K):
            pw = jnp.where(sk[k] == r, wk[k], pw)
        moe = moe + _dot(pw.astype(BF16), ybuf[c * PERM_CHUNK:(c + 1) * PERM_CHUNK, :])
    out = xmid_ref[...] + g2_ref[...] * moe.reshape(sb, rb, d)
    ms = jnp.mean(out * out, axis=-1, keepdims=True)
    o_ref[...] = out * lax.rsqrt(ms + EPS) * nf_ref[...]


def _combine(udst, nun, y, slot_t, tw, xmid, g2, nf, sb, rb):
    nseq, L, d = xmid.shape
    tb = sb * rb
    nt = L // rb
    tok = lambda s, t, *_: (s * nt + t, 0)
    return pl.pallas_call(
        _combine_kernel,
        out_shape=jax.ShapeDtypeStruct((nseq, L, d), F32),
        grid_spec=pltpu.PrefetchScalarGridSpec(
            num_scalar_prefetch=2,
            grid=(nseq // sb, nt),
            in_specs=[
                pl.BlockSpec(memory_space=pl.ANY),
                pl.BlockSpec((tb, TOP_K), tok),
                pl.BlockSpec((tb, TOP_K), tok),
                pl.BlockSpec((sb, rb, d), lambda s, t, *_: (s, t, 0)),
                pl.BlockSpec((sb, 1, d), lambda s, t, *_: (s, 0, 0)),
                pl.BlockSpec((1, d), lambda s, t, *_: (0, 0)),
            ],
            out_specs=pl.BlockSpec((sb, rb, d), lambda s, t, *_: (s, t, 0)),
            scratch_shapes=[pltpu.VMEM((LOCAL_ROWS, d), BF16), pltpu.SemaphoreType.DMA],
        ),
        compiler_params=_cparams("arbitrary", "arbitrary"),
        name="combine",
    )(udst, nun, y, slot_t, tw, xmid, g2, nf)


def _tile_rows(nseq, L, tile):
    if L >= tile:
        assert L % tile == 0
        return 1, tile
    assert tile % L == 0 and nseq % (tile // L) == 0
    return tile // L, L


def kernel(x_prompt, x_sample, state_gla, c_prompt, c_sample, w_ada, b_ada, norm1, w_in, w_gk, b_gk, gla_norm,
           gmlp_ln_g, gmlp_ln_b, gmlp_w_s, gmlp_b_s, w_out, norm2, w_router, b_router, w_gate, b_gate, w_up,
           b_up, w_down, b_down, norm_f):
    depth = w_ada.shape[0]
    assert depth == 1
    bp, lp, d = x_prompt.shape
    bs, ls, _ = x_sample.shape
    tp, ts = bp * lp, bs * ls

    nc = bp + bs
    ncp = -(-nc // SUBLANE) * SUBLANE
    c_all = jnp.concatenate([c_prompt, c_sample, jnp.zeros((ncp - nc, d), F32)], axis=0)
    mod = _ada(c_all, w_ada[0], b_ada[0][None]).reshape(ncp, N_MOD, 1, d)
    mods_p = [mod[:bp, i] for i in range(N_MOD)]
    mods_s = [mod[bp:nc, i] for i in range(N_MOD)]

    wi = w_in[0]
    c_lr = 2 * GLA_QK + GLA_WIDTH
    c_r = c_lr + GLA_LOWRANK
    wm = jnp.concatenate([wi[:, :c_lr], wi[:, c_r:]], axis=1).astype(BF16)
    wlr = jnp.pad(wi[:, c_lr:c_r], ((0, 0), (0, LANE - GLA_LOWRANK))).astype(BF16)
    wgk = jnp.pad(w_gk[0], ((0, LANE - GLA_LOWRANK), (0, 0)))
    bgk = b_gk[0][None]
    n1, n2, nf = norm1[0][None], norm2[0][None], norm_f[None]
    gn, lng, lnb = gla_norm[0][None], gmlp_ln_g[0][None], gmlp_ln_b[0][None]
    ws, bsv = gmlp_w_s[0], gmlp_b_s[0]
    pos_i = jnp.arange(GMLP_BLOCK)
    cmask = (pos_i[None, :] // CHUNK) <= (pos_i[:, None] // CHUNK)
    wm_p = jnp.where(cmask[None], ws, 0.0).astype(BF16)
    bsb_p = jnp.repeat(bsv.T, GMLP_GC, axis=1)
    reps = ROW_BLOCK // ls
    eye = jnp.eye(reps, dtype=F32)
    wm_s = jnp.einsum("ab,gij->gaibj", eye, ws[:, :ls, :ls]).reshape(GMLP_GROUPS, ROW_BLOCK, ROW_BLOCK).astype(BF16)
    bsb_s = jnp.tile(jnp.repeat(bsv[:, :ls].T, GMLP_GC, axis=1), (reps, 1))
    wo = w_out[0].astype(BF16)
    wr = jnp.pad(w_router[0], ((0, 0), (0, LANE - N_EXPERTS)))
    br = jnp.concatenate([b_router[0], jnp.full((LANE - N_EXPERTS,), -1e30, F32)])[None]
    upper = (jnp.arange(TOKEN_TILE)[:, None] < jnp.arange(TOKEN_TILE)[None, :]).astype(BF16)
    lower = (jnp.arange(N_EXPERTS)[None, :] < jnp.arange(N_EXPERTS)[:, None]).astype(F32)
    wg, wu, wd = w_gate[0], w_up[0], w_down[0]
    bg, bu, bd = b_gate[0][:, None], b_up[0][:, None], b_down[0][:, None]

    sbp, rbp = _tile_rows(bp, lp, TOKEN_TILE)
    sbs, rbs = _tile_rows(bs, ls, TOKEN_TILE)

    proj_p, gk_p = _inproj(x_prompt, mods_p[1], mods_p[0], n1, wm, wlr, wgk, bgk, sbp, rbp)
    proj_s, gk_s = _inproj(x_sample, mods_s[1], mods_s[0], n1, wm, wlr, wgk, bgk, sbs, rbs)
    mix_p, state_p = _mixer_prompt(proj_p, gk_p, gn, lng, lnb, wm_p, bsb_p, TOKEN_TILE)
    mix_s, state_s, vn_s = _mixer_sample(proj_s, gk_s, state_gla[0], gn, lng, lnb, wm_s, bsb_s)

    xmid_p, h2_p, topw_p, slot_p, n16_p = _outproj(
        mix_p, x_prompt, mods_p[2], mods_p[4], mods_p[3], n2, wo, wr, br, upper, lower, sbp, rbp)
    xmid_s, h2_s, topw_s, slot_s, n16_s = _outproj(
        mix_s, x_sample, mods_s[2], mods_s[4], mods_s[3], n2, wo, wr, br, upper, lower, sbs, rbs)

    tmg = EXPERT_TILE
    ntp = tp // TOKEN_TILE
    eids = jnp.arange(N_EXPERTS, dtype=jnp.int32)
    n16 = jnp.concatenate([n16_p[:, :, 0], n16_s[:, :, 0]], axis=0).astype(jnp.int32)
    nt_all = n16.shape[0]
    earlier = jnp.cumsum(n16, axis=0) - n16
    tot = jnp.sum(n16, axis=0)
    tiles_e = (tot + tmg - 1) // tmg
    tile_end = jnp.cumsum(tiles_e)
    tile_start = tile_end - tiles_e
    row_start = tile_start * tmg
    n_tiles = (TOP_K * (tp + ts) + nt_all * N_EXPERTS * (ROW_UNIT - 1)) // tmg + N_EXPERTS
    tid = jnp.arange(n_tiles, dtype=jnp.int32)
    te = jnp.minimum(jnp.sum((tid[:, None] >= tile_end[None, :]).astype(jnp.int32), axis=1), N_EXPERTS - 1)
    te_hot = te[:, None] == eids[None, :]
    tot_te = jnp.sum(jnp.where(te_hot, tot[None, :], 0), axis=1)
    start_te = jnp.sum(jnp.where(te_hot, tile_start[None, :], 0), axis=1)
    active = tid < tile_end[-1]
    tv = jnp.where(active, jnp.clip(tot_te - (tid - start_te) * tmg, 0, tmg), 0).astype(jnp.int32)
    last_e = jnp.max(jnp.where(tiles_e > 0, eids, 0)).astype(jnp.int32)
    te = jnp.where(active, te, last_e).astype(jnp.int32)
    te_prev = jnp.concatenate([jnp.full((1,), -1, jnp.int32), te[:-1]])
    tf = (active & (te != te_prev)).astype(jnp.int32)

    run_end = jnp.cumsum(n16, axis=1)
    run_start = run_end - n16
    unit_row = jnp.arange(LOCAL_ROWS // ROW_UNIT, dtype=jnp.int32) * ROW_UNIT
    unit_e = jnp.sum((unit_row[None, :, None] >= run_end[:, None, :]).astype(jnp.int32), axis=2)
    unit_hot = unit_e[:, :, None] == eids[None, None, :]
    run_off = row_start[None, :] + earlier - run_start
    udst = jnp.sum(jnp.where(unit_hot, run_off[:, None, :], 0), axis=2) + unit_row[None, :]
    udst = jnp.where(unit_row[None, :] < run_end[:, -1:], udst, 0).astype(jnp.int32)
    nun = (run_end[:, -1] // ROW_UNIT).astype(jnp.int32)

    n_rows = n_tiles * tmg
    udst_p, udst_s = udst[:ntp].reshape(-1), udst[ntp:].reshape(-1)
    xs = _dispatch(udst_p, nun[:ntp], slot_p, h2_p, None, n_rows, TOKEN_TILE)
    xs = _dispatch(udst_s, nun[ntp:], slot_s, h2_s, xs, n_rows, TOKEN_TILE)
    y = _experts(te, tv, tf, xs, wg, bg, wu, bu, wd, bd, tmg)
    y_prompt = _combine(udst_p, nun[:ntp], y, slot_p.T, topw_p.T, xmid_p, mods_p[5], nf, sbp, rbp)
    y_sample = _combine(udst_s, nun[ntp:], y, slot_s.T, topw_s.T, xmid_s, mods_s[5], nf, sbs, rbs)

    return (y_prompt, y_sample, state_p[None], state_s[None], vn_s[None])
```

```python
import functools

import jax
import jax.numpy as jnp
from jax import lax
from jax.experimental import pallas as pl
from jax.experimental.pallas import tpu as pltpu

F32 = jnp.float32
BF16 = jnp.bfloat16
HIGHEST = lax.Precision.HIGHEST

CHUNK = 64
GLA_HEADS = 4
GLA_DK = 64
GLA_DV = 128
GLA_QK = GLA_HEADS * GLA_DK
GLA_WIDTH = GLA_HEADS * GLA_DV
GLA_LOWRANK = 16
GLA_GATE_NORM = 16.0
GATE_SAFE_MIN = -60.0
GMLP_WIDTH = 512
GMLP_GROUPS = 4
GMLP_GC = GMLP_WIDTH // GMLP_GROUPS
GMLP_BLOCK = 128
N_EXPERTS = 32
TOP_K = 4
SWIGLU_LIMIT = 7.0
SWIGLU_ALPHA = 1.702
EPS = 1e-6
N_MOD = 6

LANE = 128
SUBLANE = 8

Q0 = 0
K0 = Q0 + GLA_QK
V0 = K0 + GLA_QK
R0 = V0 + GLA_WIDTH
U0 = R0 + GLA_WIDTH
G0 = U0 + GMLP_WIDTH
PROJ_COLS = G0 + GMLP_WIDTH

ROW_BLOCK = 128
TOKEN_TILE = 512
EXPERT_TILE = 512
FF_CHUNK = 512
ROW_UNIT = 16
LOCAL_ROWS = 2560
PERM_CHUNK = 256
EARLY_WEIGHT_COLS = 3 * PERM_CHUNK
VMEM_LIMIT = 48 * 1024 * 1024
EXPERTS_VMEM_LIMIT = 56 * 1024 * 1024


def _cparams(*sem):
    return pltpu.CompilerParams(dimension_semantics=sem, vmem_limit_bytes=VMEM_LIMIT)


def _dot(a, b, **kw):
    return jnp.dot(a, b, preferred_element_type=F32, **kw)


def _dot_nt(a, b):
    return lax.dot_general(a, b, (((1,), (1,)), ((), ())), preferred_element_type=F32)


def _gelu(x):
    return 0.5 * x * (1.0 + lax.erf(x * (0.5 ** 0.5)))


def _dot_tn(a, b):
    return lax.dot_general(a, b, (((0,), (0,)), ((), ())), preferred_element_type=F32)


def _ada_kernel(c_ref, w_ref, b_ref, o_ref):
    c = c_ref[...]
    s = c * jax.nn.sigmoid(c)
    o_ref[...] = _dot(s, w_ref[...], precision=HIGHEST) + b_ref[...]


def _ada(c, w_ada, b_ada):
    n, d = c.shape
    return pl.pallas_call(
        _ada_kernel,
        out_shape=jax.ShapeDtypeStruct((n, N_MOD * d), F32),
        grid=(N_MOD,),
        in_specs=[
            pl.BlockSpec((n, d), lambda j: (0, 0)),
            pl.BlockSpec((d, d), lambda j: (0, j)),
            pl.BlockSpec((1, d), lambda j: (0, j)),
        ],
        out_specs=pl.BlockSpec((n, d), lambda j: (0, j)),
        compiler_params=_cparams("arbitrary"),
        name="ada",
    )(c, w_ada, b_ada)


def _inproj_kernel(x_ref, sc_ref, sh_ref, n1_ref, wm_ref, wlr_ref, wgk_ref, bgk_ref, proj_ref, gk_ref):
    sb, rb, d = x_ref.shape
    x = x_ref[...]
    ms = jnp.mean(x * x, axis=-1, keepdims=True)
    h = x * lax.rsqrt(ms + EPS) * n1_ref[...]
    h = h * (1.0 + sc_ref[...]) + sh_ref[...]
    hb = h.reshape(sb * rb, d).astype(BF16)
    cw = 512
    for c in range(PROJ_COLS // cw):
        p = _dot(hb, wm_ref[:, c * cw:(c + 1) * cw])
        proj_ref[:, :, c * cw:(c + 1) * cw] = p.astype(BF16).reshape(sb, rb, cw)
    lr = _dot(hb, wlr_ref[...])
    gk = _dot(lr, wgk_ref[...], precision=HIGHEST) + bgk_ref[...]
    gk_ref[...] = gk.reshape(sb, rb, GLA_QK)


def _inproj(x, sc, sh, n1, wm, wlr, wgk, bgk, sb, rb):
    nseq, L, d = x.shape
    grid = (nseq // sb, L // rb)
    const = lambda s, t: (0, 0)
    return pl.pallas_call(
        _inproj_kernel,
        out_shape=(jax.ShapeDtypeStruct((nseq, L, PROJ_COLS), BF16),
                   jax.ShapeDtypeStruct((nseq, L, GLA_QK), F32)),
        grid=grid,
        in_specs=[
            pl.BlockSpec((sb, rb, d), lambda s, t: (s, t, 0)),
            pl.BlockSpec((sb, 1, d), lambda s, t: (s, 0, 0)),
            pl.BlockSpec((sb, 1, d), lambda s, t: (s, 0, 0)),
            pl.BlockSpec((1, d), const),
            pl.BlockSpec((d, PROJ_COLS), const),
            pl.BlockSpec((d, LANE), const),
            pl.BlockSpec((LANE, GLA_QK), const),
            pl.BlockSpec((1, GLA_QK), const),
        ],
        out_specs=(pl.BlockSpec((sb, rb, PROJ_COLS), lambda s, t: (s, t, 0)),
                   pl.BlockSpec((sb, rb, GLA_QK), lambda s, t: (s, t, 0))),
        compiler_params=_cparams("arbitrary", "arbitrary"),
        name="inproj",
    )(x, sc, sh, n1, wm, wlr, wgk, bgk)


def _head_masks():
    lane = lax.broadcasted_iota(jnp.int32, (1, GLA_QK), 1)
    return [(lane // GLA_DK) == h for h in range(GLA_HEADS)]


def _stack_heads(x, hm):
    return jnp.concatenate([jnp.where(m, x, 0.0) for m in hm], axis=0).astype(BF16)


def _chunk_pair_ids(c_len):
    ti = lax.broadcasted_iota(jnp.int32, (GLA_HEADS * c_len, c_len), 0) % c_len
    si = lax.broadcasted_iota(jnp.int32, (GLA_HEADS * c_len, c_len), 1)
    return ti, si


def _scores_factored(q4s, k, G, c_len):
    ti, si = _chunk_pair_ids(c_len)
    ke = (k * jnp.exp(-G)).astype(BF16)
    return jnp.concatenate(
        [jnp.where(si <= ti, _dot_nt(q4, ke[c * c_len:(c + 1) * c_len]), 0.0) for c, q4 in enumerate(q4s)], axis=0)


def _scores_bounded(qs, k, G, c_len, hm):
    ti, si = _chunk_pair_ids(c_len)
    t = lax.broadcasted_iota(jnp.int32, (c_len, 1), 0)
    col = lax.broadcasted_iota(jnp.int32, (c_len, c_len), 1)
    outs = []
    for c in range(qs.shape[0] // c_len):
        sl = slice(c * c_len, (c + 1) * c_len)
        q_c, k_c, g_c = qs[sl], k[sl], G[sl]
        a = jnp.where(si == ti, _dot_nt(_stack_heads(q_c, hm), k_c.astype(BF16)), 0.0)
        half = c_len // 2
        while half >= 1:
            blk = 2 * half
            sel = (col == (t // blk) * blk + (half - 1)).astype(F32)
            ref = _dot(sel, g_c, precision=HIGHEST)
            upper = (t % blk) >= half
            qh = jnp.where(upper, q_c * jnp.exp(jnp.minimum(g_c - ref, 0.0)), 0.0)
            kh = jnp.where(upper, 0.0, k_c * jnp.exp(jnp.minimum(ref - g_c, 0.0)))
            same = (ti // blk) == (si // blk)
            a = a + jnp.where(same, _dot_nt(_stack_heads(qh, hm), kh.astype(BF16)), 0.0)
            half //= 2
        outs.append(a)
    return jnp.concatenate(outs, axis=0)


def _mixer_block(p, gkpre, states, c_len, gn, lng, lnb, wm_ref, bsb):
    rows = p.shape[0]
    n_chunks = rows // c_len
    chained = len(states) == 1
    hm = _head_masks()

    q = p[:, Q0:Q0 + GLA_QK].astype(F32)
    k = p[:, K0:K0 + GLA_QK].astype(F32)
    v = p[:, V0:V0 + GLA_WIDTH]
    r = p[:, R0:R0 + GLA_WIDTH].astype(F32)

    g = jax.nn.log_sigmoid(gkpre) / GLA_GATE_NORM
    ri = lax.broadcasted_iota(jnp.int32, (rows, rows), 0)
    ci = lax.broadcasted_iota(jnp.int32, (rows, rows), 1)
    tri = ((ci <= ri) & ((ci // c_len) == (ri // c_len))).astype(F32)
    G = _dot(tri, g, precision=HIGHEST)
    qs = q * (GLA_DK ** -0.5)
    qe = qs * jnp.exp(G)
    q4s = [_stack_heads(qe[c * c_len:(c + 1) * c_len], hm) for c in range(n_chunks)]

    g_ends = jnp.concatenate([G[(c + 1) * c_len - 1:(c + 1) * c_len] for c in range(n_chunks)], axis=0)
    scores = lax.cond(jnp.min(g_ends) > GATE_SAFE_MIN,
                      lambda: _scores_factored(q4s, k, G, c_len),
                      lambda: _scores_bounded(qs, k, G, c_len, hm))
    hc = GLA_HEADS * c_len

    new_states = []
    o_rows = []
    st = states[0]
    for c in range(n_chunks):
        lo, hi = c * c_len, (c + 1) * c_len
        if not chained:
            st = states[c]
        q4 = q4s[c]
        a = scores[c * hc:(c + 1) * hc].astype(BF16)
        o_inter = _dot_nt(q4, st.astype(BF16))
        v_c = v[lo:hi]
        heads = []
        for h in range(GLA_HEADS):
            o_h = o_inter[h * c_len:(h + 1) * c_len] + _dot(
                a[h * c_len:(h + 1) * c_len], v_c[:, h * GLA_DV:(h + 1) * GLA_DV])
            heads.append(o_h)
        o_rows.append(jnp.concatenate(heads, axis=1))
        g_last = G[hi - 1:hi]
        kd = (k[lo:hi] * jnp.exp(g_last - G[lo:hi])).astype(BF16)
        upd = _dot_tn(v_c, kd)
        st_new = jnp.exp(g_last) * st
        for h in range(GLA_HEADS):
            st_new = st_new + jnp.where(hm[h], upd[h * GLA_DV:(h + 1) * GLA_DV], 0.0)
        if chained:
            st = st_new
        else:
            new_states.append(st_new)
    if chained:
        new_states = [st]
    o = jnp.concatenate(o_rows, axis=0)

    gla = []
    for h in range(GLA_HEADS):
        o_h = o[:, h * GLA_DV:(h + 1) * GLA_DV]
        ms = jnp.mean(o_h * o_h, axis=-1, keepdims=True)
        r_h = r[:, h * GLA_DV:(h + 1) * GLA_DV]
        gla.append(o_h * lax.rsqrt(ms + EPS) * gn * (r_h * jax.nn.sigmoid(r_h)))

    u = _gelu(p[:, U0:U0 + GMLP_WIDTH].astype(F32))
    vv = _gelu(p[:, G0:G0 + GMLP_WIDTH].astype(F32))
    mu = jnp.mean(vv, axis=-1, keepdims=True)
    xc = vv - mu
    var = jnp.mean(xc * xc, axis=-1, keepdims=True)
    vn = xc * lax.rsqrt(var + EPS) * lng + lnb
    vnb = vn.astype(BF16)
    gm = []
    for gi in range(GMLP_GROUPS):
        sl = slice(gi * GMLP_GC, (gi + 1) * GMLP_GC)
        mixed = _dot(wm_ref[gi], vnb[:, sl]) + bsb[:, sl]
        gm.append(u[:, sl] * mixed)
    out = jnp.concatenate(gla + gm, axis=1)
    return out, new_states, vn


def _mixer_prompt_kernel(proj_ref, gk_ref, gn_ref, lng_ref, lnb_ref, wm_ref, bsb_ref, mix_ref, s_ref, st_scr):
    t = pl.program_id(1)
    nt = pl.num_programs(1)

    @pl.when(t == 0)
    def _():
        st_scr[...] = jnp.zeros_like(st_scr)

    n_sub = proj_ref.shape[1] // ROW_BLOCK

    def body(i, carry):
        r0 = pl.multiple_of(i * ROW_BLOCK, ROW_BLOCK)
        p = proj_ref[0, pl.ds(r0, ROW_BLOCK), :]
        gk = gk_ref[0, pl.ds(r0, ROW_BLOCK), :]
        out, sts, _ = _mixer_block(p, gk, [st_scr[...]], CHUNK, gn_ref[...], lng_ref[...], lnb_ref[...],
                                   wm_ref, bsb_ref[...])
        st_scr[...] = sts[0]
        mix_ref[0, pl.ds(r0, ROW_BLOCK), :] = out.astype(BF16)
        return carry

    lax.fori_loop(0, n_sub, body, 0)

    @pl.when(t == nt - 1)
    def _():
        s_ref[0] = st_scr[...].T.reshape(GLA_HEADS, GLA_DK, GLA_DV)


def _mixer_prompt(proj, gk, gn, lng, lnb, wm, bsb, tb):
    b, L, _ = proj.shape
    const2 = lambda s, t: (0, 0)
    return pl.pallas_call(
        _mixer_prompt_kernel,
        out_shape=(jax.ShapeDtypeStruct((b, L, 2 * GLA_WIDTH), BF16),
                   jax.ShapeDtypeStruct((b, GLA_HEADS, GLA_DK, GLA_DV), F32)),
        grid=(b, L // tb),
        in_specs=[
            pl.BlockSpec((1, tb, PROJ_COLS), lambda s, t: (s, t, 0)),
            pl.BlockSpec((1, tb, GLA_QK), lambda s, t: (s, t, 0)),
            pl.BlockSpec((1, GLA_DV), const2),
            pl.BlockSpec((1, GMLP_WIDTH), const2),
            pl.BlockSpec((1, GMLP_WIDTH), const2),
            pl.BlockSpec((GMLP_GROUPS, ROW_BLOCK, ROW_BLOCK), lambda s, t: (0, 0, 0)),
            pl.BlockSpec((ROW_BLOCK, GMLP_WIDTH), const2),
        ],
        out_specs=(pl.BlockSpec((1, tb, 2 * GLA_WIDTH), lambda s, t: (s, t, 0)),
                   pl.BlockSpec((1, GLA_HEADS, GLA_DK, GLA_DV), lambda s, t: (s, 0, 0, 0))),
        scratch_shapes=[pltpu.VMEM((GLA_DV, GLA_QK), F32)],
        compiler_params=_cparams("arbitrary", "arbitrary"),
        name="mixer_prompt",
    )(proj, gk, gn, lng, lnb, wm, bsb)


def _mixer_sample_kernel(proj_ref, gk_ref, s0_ref, gn_ref, lng_ref, lnb_ref, wm_ref, bsb_ref,
                         mix_ref, s_ref, vn_ref):
    sb, rb, _ = proj_ref.shape
    p = proj_ref[...].reshape(sb * rb, PROJ_COLS)
    gk = gk_ref[...].reshape(sb * rb, GLA_QK)
    states = [s0_ref[i].reshape(GLA_QK, GLA_DV).T for i in range(sb)]
    out, sts, vn = _mixer_block(p, gk, states, rb, gn_ref[...], lng_ref[...], lnb_ref[...], wm_ref, bsb_ref[...])
    mix_ref[...] = out.astype(BF16).reshape(sb, rb, 2 * GLA_WIDTH)
    vn_ref[...] = vn.reshape(sb, rb, GMLP_WIDTH)
    for i in range(sb):
        s_ref[i] = sts[i].T.reshape(GLA_HEADS, GLA_DK, GLA_DV)


def _mixer_sample(proj, gk, s0, gn, lng, lnb, wm, bsb):
    n, L, _ = proj.shape
    sb = ROW_BLOCK // L
    const2 = lambda s: (0, 0)
    return pl.pallas_call(
        _mixer_sample_kernel,
        out_shape=(jax.ShapeDtypeStruct((n, L, 2 * GLA_WIDTH), BF16),
                   jax.ShapeDtypeStruct((n, GLA_HEADS, GLA_DK, GLA_DV), F32),
                   jax.ShapeDtypeStruct((n, L, GMLP_WIDTH), F32)),
        grid=(n // sb,),
        in_specs=[
            pl.BlockSpec((sb, L, PROJ_COLS), lambda s: (s, 0, 0)),
            pl.BlockSpec((sb, L, GLA_QK), lambda s: (s, 0, 0)),
            pl.BlockSpec((sb, GLA_HEADS, GLA_DK, GLA_DV), lambda s: (s, 0, 0, 0)),
            pl.BlockSpec((1, GLA_DV), const2),
            pl.BlockSpec((1, GMLP_WIDTH), const2),
            pl.BlockSpec((1, GMLP_WIDTH), const2),
            pl.BlockSpec((GMLP_GROUPS, ROW_BLOCK, ROW_BLOCK), lambda s: (0, 0, 0)),
            pl.BlockSpec((ROW_BLOCK, GMLP_WIDTH), const2),
        ],
        out_specs=(pl.BlockSpec((sb, L, 2 * GLA_WIDTH), lambda s: (s, 0, 0)),
                   pl.BlockSpec((sb, GLA_HEADS, GLA_DK, GLA_DV), lambda s: (s, 0, 0, 0)),
                   pl.BlockSpec((sb, L, GMLP_WIDTH), lambda s: (s, 0, 0))),
        compiler_params=_cparams("arbitrary"),
        name="mixer_sample",
    )(proj, gk, s0, gn, lng, lnb, wm, bsb)


def _outproj_kernel(mix_ref, x_ref, g1_ref, sc_ref, sh_ref, n2_ref, wo_ref, wr_ref, br_ref, upper_ref, lower_ref,
                    xmid_ref, h2_ref, topw_ref, slot_ref, n16_ref):
    sb, rb, d = x_ref.shape
    tm = sb * rb

    y = _dot(mix_ref[...].reshape(tm, d), wo_ref[...])
    xm = x_ref[...] + g1_ref[...] * y.reshape(sb, rb, d)
    xmid_ref[...] = xm
    ms = jnp.mean(xm * xm, axis=-1, keepdims=True)
    h2 = xm * lax.rsqrt(ms + EPS) * n2_ref[...]
    h2 = (h2 * (1.0 + sc_ref[...]) + sh_ref[...]).reshape(tm, d)
    h_hi = h2.astype(BF16)
    h2_ref[...] = h_hi

    h_lo = (h2 - h_hi.astype(F32)).astype(BF16)
    part = _dot(h_hi, wr_ref[...])
    logits = part[:, :LANE] + part[:, LANE:] + _dot(h_lo, wr_ref[:, :LANE]) + br_ref[...]
    l = logits.T[:N_EXPERTS]
    eid = lax.broadcasted_iota(jnp.int32, (N_EXPERTS, tm), 0).astype(F32)
    top_l, sel = [], []
    for _ in range(TOP_K):
        m = jnp.max(l, axis=0, keepdims=True)
        idx = jnp.min(jnp.where(l == m, eid, float(N_EXPERTS)), axis=0, keepdims=True)
        hit = eid == idx
        top_l.append(m)
        sel.append(hit)
        l = jnp.where(hit, -jnp.inf, l)
    ex = [jnp.exp(t - top_l[0]) for t in top_l]
    den = ex[0] + ex[1] + ex[2] + ex[3]
    topw_ref[...] = jnp.concatenate([e / den for e in ex], axis=0)

    chosen = (sel[0] | sel[1] | sel[2] | sel[3])
    cb = jnp.where(chosen, 1.0, 0.0)
    before = _dot(cb.astype(BF16), upper_ref[...])
    n = jnp.sum(cb, axis=1, keepdims=True)
    n16 = jnp.floor((n + (ROW_UNIT - 1)) * (1.0 / ROW_UNIT)) * ROW_UNIT
    n16b = jnp.broadcast_to(n16, (N_EXPERTS, LANE))
    ls = _dot(lower_ref[...], n16b, precision=HIGHEST)[:, 0:1]
    base = before + ls
    slot_ref[...] = jnp.concatenate(
        [jnp.sum(jnp.where(s, base, 0.0), axis=0, keepdims=True) for s in sel], axis=0).astype(jnp.int32)
    n16_ref[0] = n16b


def _outproj(mix, x, g1, sc2, sh2, n2, wo, wr, br, upper, lower, sb, rb):
    nseq, L, d = x.shape
    tm = sb * rb
    nt = L // rb
    T = nseq * L
    grid = (nseq // sb, nt)
    const = lambda s, t: (0, 0)
    tok = lambda s, t: (0, s * nt + t)
    return pl.pallas_call(
        _outproj_kernel,
        out_shape=(jax.ShapeDtypeStruct((nseq, L, d), F32),
                   jax.ShapeDtypeStruct((T, d), BF16),
                   jax.ShapeDtypeStruct((TOP_K, T), F32),
                   jax.ShapeDtypeStruct((TOP_K, T), jnp.int32),
                   jax.ShapeDtypeStruct((T // tm, N_EXPERTS, LANE), F32)),
        grid=grid,
        in_specs=[
            pl.BlockSpec((sb, rb, d), lambda s, t: (s, t, 0)),
            pl.BlockSpec((sb, rb, d), lambda s, t: (s, t, 0)),
            pl.BlockSpec((sb, 1, d), lambda s, t: (s, 0, 0)),
            pl.BlockSpec((sb, 1, d), lambda s, t: (s, 0, 0)),
            pl.BlockSpec((sb, 1, d), lambda s, t: (s, 0, 0)),
            pl.BlockSpec((1, d), const),
            pl.BlockSpec((d, d), const),
            pl.BlockSpec((d, 2 * LANE), const),
            pl.BlockSpec((1, LANE), const),
            pl.BlockSpec((tm, tm), const),
            pl.BlockSpec((N_EXPERTS, N_EXPERTS), const),
        ],
        out_specs=(pl.BlockSpec((sb, rb, d), lambda s, t: (s, t, 0)),
                   pl.BlockSpec((tm, d), lambda s, t: (s * nt + t, 0)),
                   pl.BlockSpec((TOP_K, tm), tok),
                   pl.BlockSpec((TOP_K, tm), tok),
                   pl.BlockSpec((1, N_EXPERTS, LANE), lambda s, t: (s * nt + t, 0, 0))),
        compiler_params=_cparams("arbitrary", "arbitrary"),
        name="outproj",
    )(mix, x, g1, sc2, sh2, n2, wo, wr, br, upper, lower)


def _unit_copy(src_ref, src_row, dst_ref, dst_row, sem):
    return pltpu.make_async_copy(src_ref.at[pl.ds(pl.multiple_of(src_row, ROW_UNIT), ROW_UNIT)],
                                 dst_ref.at[pl.ds(pl.multiple_of(dst_row, ROW_UNIT), ROW_UNIT)], sem)


def _start_unit_copies(nun, copy_of):
    def issue_pair(j, carry):
        copy_of(2 * j).start()
        copy_of(2 * j + 1).start()
        return carry

    lax.fori_loop(0, nun // 2, issue_pair, 0)

    @pl.when(nun % 2 == 1)
    def _():
        copy_of(nun - 1).start()


def _wait_unit_copies(nun, copy_of):
    def drain(u, carry):
        copy_of(0).wait()
        return carry

    lax.fori_loop(0, nun, drain, 0)


def _dispatch_body(udst_ref, nun_ref, slot_ref, h2_ref, xs_ref, xloc, sem):
    i = pl.program_id(0)
    tb = slot_ref.shape[1]
    nun = nun_ref[i]
    s = slot_ref[...]
    h2 = h2_ref[...]
    def sort_chunk(c):
        r = lax.broadcasted_iota(jnp.int32, (PERM_CHUNK, tb), 0) + c * PERM_CHUNK
        hit = (s[0:1] == r) | (s[1:2] == r) | (s[2:3] == r) | (s[3:4] == r)
        p = jnp.where(hit, 1.0, 0.0).astype(BF16)
        xloc[c * PERM_CHUNK:(c + 1) * PERM_CHUNK, :] = _dot(p, h2).astype(BF16)

    n_chunks = LOCAL_ROWS // PERM_CHUNK
    for c in range(n_chunks - 1):
        sort_chunk(c)
    pl.when((n_chunks - 1) * (PERM_CHUNK // ROW_UNIT) < nun)(functools.partial(sort_chunk, n_chunks - 1))

    u0 = i * (LOCAL_ROWS // ROW_UNIT)
    copy_of = lambda u: _unit_copy(xloc, u * ROW_UNIT, xs_ref, udst_ref[u0 + u], sem)
    _start_unit_copies(nun, copy_of)
    _wait_unit_copies(nun, copy_of)


def _dispatch_first_kernel(udst_ref, nun_ref, slot_ref, h2_ref, xs_ref, xloc, sem):
    _dispatch_body(udst_ref, nun_ref, slot_ref, h2_ref, xs_ref, xloc, sem)


def _dispatch_next_kernel(udst_ref, nun_ref, slot_ref, h2_ref, xs_in_ref, xs_ref, xloc, sem):
    del xs_in_ref
    _dispatch_body(udst_ref, nun_ref, slot_ref, h2_ref, xs_ref, xloc, sem)


def _dispatch(udst, nun, slot, h2, xs, n_rows, tb):
    T, d = h2.shape
    any_spec = pl.BlockSpec(memory_space=pl.ANY)
    in_specs = [pl.BlockSpec((TOP_K, tb), lambda i, *_: (0, i)),
                pl.BlockSpec((tb, d), lambda i, *_: (i, 0))]
    aliases = {}
    body = _dispatch_first_kernel
    args = (udst, nun, slot, h2)
    if xs is not None:
        in_specs.append(any_spec)
        aliases = {4: 0}
        body = _dispatch_next_kernel
        args = args + (xs,)
    return pl.pallas_call(
        body,
        out_shape=jax.ShapeDtypeStruct((n_rows, d), BF16),
        grid_spec=pltpu.PrefetchScalarGridSpec(
            num_scalar_prefetch=2, grid=(T // tb,), in_specs=in_specs, out_specs=any_spec,
            scratch_shapes=[pltpu.VMEM((LOCAL_ROWS, d), BF16), pltpu.SemaphoreType.DMA]),
        input_output_aliases=aliases,
        compiler_params=pltpu.CompilerParams(dimension_semantics=("arbitrary",), has_side_effects=True,
                                             vmem_limit_bytes=VMEM_LIMIT),
        name="dispatch_next" if xs is not None else "dispatch_first",
    )(*args)


def _experts_kernel(te_ref, tv_ref, tf_ref, xs_ref, wg_ref, bg_ref, wu_ref, bu_ref, wd_ref, bd_ref, y_ref,
                    wgb, wub, wdb):
    i = pl.program_id(0)
    valid = tv_ref[i]
    tm, d = xs_ref.shape
    ff = wg_ref.shape[2]

    @pl.when(tf_ref[i] == 1)
    def _():
        wgb[...] = wg_ref[0].astype(BF16)
        wub[...] = wu_ref[0].astype(BF16)
        wdb[...] = wd_ref[0].astype(BF16)

    @pl.when(valid > 0)
    def _():
        row = lax.broadcasted_iota(jnp.int32, (tm, 1), 0)
        xb = jnp.where(row < valid, xs_ref[...], jnp.zeros((), BF16))
        acc = jnp.zeros((tm, d), F32)
        for c in range(ff // FF_CHUNK):
            cs = slice(c * FF_CHUNK, (c + 1) * FF_CHUNK)
            gate = jnp.minimum(_dot(xb, wgb[:, cs]) + bg_ref[0, :, cs], SWIGLU_LIMIT)
            up = jnp.clip(_dot(xb, wub[:, cs]) + bu_ref[0, :, cs], -SWIGLU_LIMIT, SWIGLU_LIMIT)
            act = (up + 1.0) * gate * jax.nn.sigmoid(SWIGLU_ALPHA * gate)
            acc = acc + _dot(act.astype(BF16), wdb[cs, :])
        y_ref[...] = (acc + bd_ref[0]).astype(BF16)

    @pl.when(valid == 0)
    def _():
        y_ref[...] = jnp.zeros_like(y_ref)


def _experts(tile_expert, tile_valid, tile_first, xs, wg, bg, wu, bu, wd, bd, tmg):
    n_tiles = tile_expert.shape[0]
    _, d, ff = wg.shape
    wspec = lambda shp: pl.BlockSpec(shp, lambda i, te, tv, tf: (te[i], 0, 0))
    return pl.pallas_call(
        _experts_kernel,
        out_shape=jax.ShapeDtypeStruct(xs.shape, BF16),
        grid_spec=pltpu.PrefetchScalarGridSpec(
            num_scalar_prefetch=3,
            grid=(n_tiles,),
            in_specs=[
                pl.BlockSpec((tmg, d), lambda i, te, tv, tf: (i, 0)),
                wspec((1, d, ff)), wspec((1, 1, ff)),
                wspec((1, d, ff)), wspec((1, 1, ff)),
                wspec((1, ff, d)), wspec((1, 1, d)),
            ],
            out_specs=pl.BlockSpec((tmg, d), lambda i, te, tv, tf: (i, 0)),
            scratch_shapes=[pltpu.VMEM((d, ff), BF16), pltpu.VMEM((d, ff), BF16), pltpu.VMEM((ff, d), BF16)],
        ),
        compiler_params=pltpu.CompilerParams(dimension_semantics=("arbitrary",), vmem_limit_bytes=EXPERTS_VMEM_LIMIT),
        name="experts",
    )(tile_expert, tile_valid, tile_first, xs, wg, bg, wu, bu, wd, bd)


def _combine_kernel(udst_ref, nun_ref, y_ref, slot_ref, tw_ref, xmid_ref, g2_ref, nf_ref, o_ref, ybuf, pw_scr, sem):
    sb, rb, d = xmid_ref.shape
    tb = sb * rb
    i = pl.program_id(0) * pl.num_programs(1) + pl.program_id(1)
    nun = nun_ref[i]

    @pl.when(i == 0)
    def _():
        ybuf[...] = jnp.zeros_like(ybuf)

    u0 = i * (LOCAL_ROWS // ROW_UNIT)
    copy_of = lambda u: _unit_copy(y_ref, udst_ref[u0 + u], ybuf, u * ROW_UNIT, sem)
    _start_unit_copies(nun, copy_of)

    st = slot_ref[...]
    tw = tw_ref[...]
    lane = lax.broadcasted_iota(jnp.int32, (tb, PERM_CHUNK), 1)
    sk = [jnp.broadcast_to(st[:, k:k + 1], (tb, PERM_CHUNK)) for k in range(TOP_K)]
    wk = [jnp.broadcast_to(tw[:, k:k + 1], (tb, PERM_CHUNK)) for k in range(TOP_K)]
    def weights_chunk(c):
        r = lane + c * PERM_CHUNK
        pw = jnp.zeros((tb, PERM_CHUNK), F32)
        for k in range(TOP_K):
            pw = jnp.where(sk[k] == r, wk[k], pw)
        return pw.astype(BF16)

    n_chunks = LOCAL_ROWS // PERM_CHUNK
    n_early = pw_scr.shape[1] // PERM_CHUNK
    for c in range(n_early):
        pw_scr[:, c * PERM_CHUNK:(c + 1) * PERM_CHUNK] = weights_chunk(c)

    _wait_unit_copies(nun, copy_of)
    moe = _dot(pw_scr[...], ybuf[:n_early * PERM_CHUNK, :])
    for c in range(n_early, n_chunks):
        moe = moe + _dot(weights_chunk(c), ybuf[c * PERM_CHUNK:(c + 1) * PERM_CHUNK, :])
    out = xmid_ref[...] + g2_ref[...] * moe.reshape(sb, rb, d)
    ms = jnp.mean(out * out, axis=-1, keepdims=True)
    o_ref[...] = out * lax.rsqrt(ms + EPS) * nf_ref[...]


def _combine(udst, nun, y, slot_t, tw, xmid, g2, nf, sb, rb):
    nseq, L, d = xmid.shape
    tb = sb * rb
    nt = L // rb
    tok = lambda s, t, *_: (s * nt + t, 0)
    return pl.pallas_call(
        _combine_kernel,
        out_shape=jax.ShapeDtypeStruct((nseq, L, d), F32),
        grid_spec=pltpu.PrefetchScalarGridSpec(
            num_scalar_prefetch=2,
            grid=(nseq // sb, nt),
            in_specs=[
                pl.BlockSpec(memory_space=pl.ANY),
                pl.BlockSpec((tb, TOP_K), tok),
                pl.BlockSpec((tb, TOP_K), tok),
                pl.BlockSpec((sb, rb, d), lambda s, t, *_: (s, t, 0)),
                pl.BlockSpec((sb, 1, d), lambda s, t, *_: (s, 0, 0)),
                pl.BlockSpec((1, d), lambda s, t, *_: (0, 0)),
            ],
            out_specs=pl.BlockSpec((sb, rb, d), lambda s, t, *_: (s, t, 0)),
            scratch_shapes=[pltpu.VMEM((LOCAL_ROWS, d), BF16), pltpu.VMEM((tb, EARLY_WEIGHT_COLS), BF16),
                            pltpu.SemaphoreType.DMA],
        ),
        compiler_params=_cparams("arbitrary", "arbitrary"),
        name="combine",
    )(udst, nun, y, slot_t, tw, xmid, g2, nf)


def _tile_rows(nseq, L, tile):
    if L >= tile:
        assert L % tile == 0
        return 1, tile
    assert tile % L == 0 and nseq % (tile // L) == 0
    return tile // L, L


def kernel(x_prompt, x_sample, state_gla, c_prompt, c_sample, w_ada, b_ada, norm1, w_in, w_gk, b_gk, gla_norm,
           gmlp_ln_g, gmlp_ln_b, gmlp_w_s, gmlp_b_s, w_out, norm2, w_router, b_router, w_gate, b_gate, w_up,
           b_up, w_down, b_down, norm_f):
    depth = w_ada.shape[0]
    assert depth == 1
    bp, lp, d = x_prompt.shape
    bs, ls, _ = x_sample.shape
    tp, ts = bp * lp, bs * ls

    nc = bp + bs
    ncp = -(-nc // SUBLANE) * SUBLANE
    c_all = jnp.concatenate([c_prompt, c_sample, jnp.zeros((ncp - nc, d), F32)], axis=0)
    mod = _ada(c_all, w_ada[0], b_ada[0][None]).reshape(ncp, N_MOD, 1, d)
    mods_p = [mod[:bp, i] for i in range(N_MOD)]
    mods_s = [mod[bp:nc, i] for i in range(N_MOD)]

    wi = w_in[0]
    c_lr = 2 * GLA_QK + GLA_WIDTH
    c_r = c_lr + GLA_LOWRANK
    wm = jnp.concatenate([wi[:, :c_lr], wi[:, c_r:]], axis=1).astype(BF16)
    wlr = jnp.pad(wi[:, c_lr:c_r], ((0, 0), (0, LANE - GLA_LOWRANK))).astype(BF16)
    wgk = jnp.pad(w_gk[0], ((0, LANE - GLA_LOWRANK), (0, 0)))
    bgk = b_gk[0][None]
    n1, n2, nf = norm1[0][None], norm2[0][None], norm_f[None]
    gn, lng, lnb = gla_norm[0][None], gmlp_ln_g[0][None], gmlp_ln_b[0][None]
    ws, bsv = gmlp_w_s[0], gmlp_b_s[0]
    pos_i = jnp.arange(GMLP_BLOCK)
    cmask = (pos_i[None, :] // CHUNK) <= (pos_i[:, None] // CHUNK)
    wm_p = jnp.where(cmask[None], ws, 0.0).astype(BF16)
    bsb_p = jnp.repeat(bsv.T, GMLP_GC, axis=1)
    reps = ROW_BLOCK // ls
    eye = jnp.eye(reps, dtype=F32)
    wm_s = jnp.einsum("ab,gij->gaibj", eye, ws[:, :ls, :ls]).reshape(GMLP_GROUPS, ROW_BLOCK, ROW_BLOCK).astype(BF16)
    bsb_s = jnp.tile(jnp.repeat(bsv[:, :ls].T, GMLP_GC, axis=1), (reps, 1))
    wo = w_out[0].astype(BF16)
    wr_f = jnp.pad(w_router[0], ((0, 0), (0, LANE - N_EXPERTS)))
    wr_hi = wr_f.astype(BF16)
    wr = jnp.concatenate([wr_hi, (wr_f - wr_hi.astype(F32)).astype(BF16)], axis=1)
    br = jnp.concatenate([b_router[0], jnp.full((LANE - N_EXPERTS,), -1e30, F32)])[None]
    upper = (jnp.arange(TOKEN_TILE)[:, None] < jnp.arange(TOKEN_TILE)[None, :]).astype(BF16)
    lower = (jnp.arange(N_EXPERTS)[None, :] < jnp.arange(N_EXPERTS)[:, None]).astype(F32)
    wg, wu, wd = w_gate[0], w_up[0], w_down[0]
    bg, bu, bd = b_gate[0][:, None], b_up[0][:, None], b_down[0][:, None]

    sbp, rbp = _tile_rows(bp, lp, TOKEN_TILE)
    sbs, rbs = _tile_rows(bs, ls, TOKEN_TILE)

    proj_p, gk_p = _inproj(x_prompt, mods_p[1], mods_p[0], n1, wm, wlr, wgk, bgk, sbp, rbp)
    proj_s, gk_s = _inproj(x_sample, mods_s[1], mods_s[0], n1, wm, wlr, wgk, bgk, sbs, rbs)
    mix_p, state_p = _mixer_prompt(proj_p, gk_p, gn, lng, lnb, wm_p, bsb_p, TOKEN_TILE)
    mix_s, state_s, vn_s = _mixer_sample(proj_s, gk_s, state_gla[0], gn, lng, lnb, wm_s, bsb_s)

    xmid_p, h2_p, topw_p, slot_p, n16_p = _outproj(
        mix_p, x_prompt, mods_p[2], mods_p[4], mods_p[3], n2, wo, wr, br, upper, lower, sbp, rbp)
    xmid_s, h2_s, topw_s, slot_s, n16_s = _outproj(
        mix_s, x_sample, mods_s[2], mods_s[4], mods_s[3], n2, wo, wr, br, upper, lower, sbs, rbs)

    tmg = EXPERT_TILE
    ntp = tp // TOKEN_TILE
    eids = jnp.arange(N_EXPERTS, dtype=jnp.int32)
    n16 = jnp.concatenate([n16_p[:, :, 0], n16_s[:, :, 0]], axis=0).astype(jnp.int32)
    nt_all = n16.shape[0]
    earlier = jnp.cumsum(n16, axis=0) - n16
    tot = jnp.sum(n16, axis=0)
    tiles_e = (tot + tmg - 1) // tmg
    tile_end = jnp.cumsum(tiles_e)
    tile_start = tile_end - tiles_e
    row_start = tile_start * tmg
    n_tiles = (TOP_K * (tp + ts) + nt_all * N_EXPERTS * (ROW_UNIT - 1)) // tmg + N_EXPERTS
    tid = jnp.arange(n_tiles, dtype=jnp.int32)
    te = jnp.minimum(jnp.sum((tid[:, None] >= tile_end[None, :]).astype(jnp.int32), axis=1), N_EXPERTS - 1)
    te_hot = te[:, None] == eids[None, :]
    tot_te = jnp.sum(jnp.where(te_hot, tot[None, :], 0), axis=1)
    start_te = jnp.sum(jnp.where(te_hot, tile_start[None, :], 0), axis=1)
    active = tid < tile_end[-1]
    tv = jnp.where(active, jnp.clip(tot_te - (tid - start_te) * tmg, 0, tmg), 0).astype(jnp.int32)
    last_e = jnp.max(jnp.where(tiles_e > 0, eids, 0)).astype(jnp.int32)
    te = jnp.where(active, te, last_e).astype(jnp.int32)
    te_prev = jnp.concatenate([jnp.full((1,), -1, jnp.int32), te[:-1]])
    tf = (active & (te != te_prev)).astype(jnp.int32)

    run_end = jnp.cumsum(n16, axis=1)
    run_start = run_end - n16
    unit_row = jnp.arange(LOCAL_ROWS // ROW_UNIT, dtype=jnp.int32) * ROW_UNIT
    unit_e = jnp.sum((unit_row[None, :, None] >= run_end[:, None, :]).astype(jnp.int32), axis=2)
    unit_hot = unit_e[:, :, None] == eids[None, None, :]
    run_off = row_start[None, :] + earlier - run_start
    udst = jnp.sum(jnp.where(unit_hot, run_off[:, None, :], 0), axis=2) + unit_row[None, :]
    udst = jnp.where(unit_row[None, :] < run_end[:, -1:], udst, 0).astype(jnp.int32)
    nun = (run_end[:, -1] // ROW_UNIT).astype(jnp.int32)

    n_rows = n_tiles * tmg
    udst_p, udst_s = udst[:ntp].reshape(-1), udst[ntp:].reshape(-1)
    xs = _dispatch(udst_p, nun[:ntp], slot_p, h2_p, None, n_rows, TOKEN_TILE)
    xs = _dispatch(udst_s, nun[ntp:], slot_s, h2_s, xs, n_rows, TOKEN_TILE)
    y = _experts(te, tv, tf, xs, wg, bg, wu, bu, wd, bd, tmg)
    y_prompt = _combine(udst_p, nun[:ntp], y, slot_p.T, topw_p.T, xmid_p, mods_p[5], nf, sbp, rbp)
    y_sample = _combine(udst_s, nun[ntp:], y, slot_s.T, topw_s.T, xmid_s, mods_s[5], nf, sbs, rbs)

    return (y_prompt, y_sample, state_p[None], state_s[None], vn_s[None])
```

```python
import functools

import jax
import jax.numpy as jnp
from jax import lax
from jax.experimental import pallas as pl
from jax.experimental.pallas import tpu as pltpu

F32 = jnp.float32
BF16 = jnp.bfloat16
HIGHEST = lax.Precision.HIGHEST

CHUNK = 64
GLA_HEADS = 4
GLA_DK = 64
GLA_DV = 128
GLA_QK = GLA_HEADS * GLA_DK
GLA_WIDTH = GLA_HEADS * GLA_DV
GLA_LOWRANK = 16
GLA_GATE_NORM = 16.0
GATE_SAFE_MIN = -60.0
GMLP_WIDTH = 512
GMLP_GROUPS = 4
GMLP_GC = GMLP_WIDTH // GMLP_GROUPS
GMLP_BLOCK = 128
N_EXPERTS = 32
TOP_K = 4
SWIGLU_LIMIT = 7.0
SWIGLU_ALPHA = 1.702
EPS = 1e-6
N_MOD = 6

LANE = 128
SUBLANE = 8

Q0 = 0
K0 = Q0 + GLA_QK
V0 = K0 + GLA_QK
R0 = V0 + GLA_WIDTH
U0 = R0 + GLA_WIDTH
G0 = U0 + GMLP_WIDTH
PROJ_COLS = G0 + GMLP_WIDTH

ROW_BLOCK = 128
TOKEN_TILE = 512
EXPERT_TILE = 512
FF_CHUNK = 512
ROW_UNIT = 16
LOCAL_ROWS = 2560
SORT_CHUNK = 128
PERM_CHUNK = 256
EARLY_WEIGHT_COLS = 3 * PERM_CHUNK
VMEM_LIMIT = 48 * 1024 * 1024
EXPERTS_VMEM_LIMIT = 56 * 1024 * 1024


def _cparams(*sem):
    return pltpu.CompilerParams(dimension_semantics=sem, vmem_limit_bytes=VMEM_LIMIT)


def _dot(a, b, **kw):
    return jnp.dot(a, b, preferred_element_type=F32, **kw)


def _dot_nt(a, b):
    return lax.dot_general(a, b, (((1,), (1,)), ((), ())), preferred_element_type=F32)


def _gelu(x):
    return 0.5 * x * (1.0 + lax.erf(x * (0.5 ** 0.5)))


def _dot_tn(a, b):
    return lax.dot_general(a, b, (((0,), (0,)), ((), ())), preferred_element_type=F32)


def _ada_kernel(c_ref, w_ref, b_ref, o_ref):
    c = c_ref[...]
    s = c * jax.nn.sigmoid(c)
    o_ref[...] = _dot(s, w_ref[...], precision=HIGHEST) + b_ref[...]


def _ada(c, w_ada, b_ada):
    n, d = c.shape
    return pl.pallas_call(
        _ada_kernel,
        out_shape=jax.ShapeDtypeStruct((n, N_MOD * d), F32),
        grid=(N_MOD,),
        in_specs=[
            pl.BlockSpec((n, d), lambda j: (0, 0)),
            pl.BlockSpec((d, d), lambda j: (0, j)),
            pl.BlockSpec((1, d), lambda j: (0, j)),
        ],
        out_specs=pl.BlockSpec((n, d), lambda j: (0, j)),
        compiler_params=_cparams("arbitrary"),
        name="ada",
    )(c, w_ada, b_ada)


def _inproj_kernel(x_ref, sc_ref, sh_ref, n1_ref, wm_ref, wlr_ref, wgk_ref, bgk_ref, proj_ref, gk_ref):
    sb, rb, d = x_ref.shape
    x = x_ref[...]
    ms = jnp.mean(x * x, axis=-1, keepdims=True)
    h = x * lax.rsqrt(ms + EPS) * n1_ref[...]
    h = h * (1.0 + sc_ref[...]) + sh_ref[...]
    hb = h.reshape(sb * rb, d).astype(BF16)
    cw = 512
    for c in range(PROJ_COLS // cw):
        p = _dot(hb, wm_ref[:, c * cw:(c + 1) * cw])
        proj_ref[:, :, c * cw:(c + 1) * cw] = p.astype(BF16).reshape(sb, rb, cw)
    lr = _dot(hb, wlr_ref[...])
    lr_hi = lr.astype(BF16)
    lr_lo = (lr - lr_hi.astype(F32)).astype(BF16)
    part = _dot(lr_hi, wgk_ref[...])
    gk = part[:, :GLA_QK] + part[:, GLA_QK:] + _dot(lr_lo, wgk_ref[:, :GLA_QK]) + bgk_ref[...]
    gk_ref[...] = gk.reshape(sb, rb, GLA_QK)


def _inproj(x, sc, sh, n1, wm, wlr, wgk, bgk, sb, rb):
    nseq, L, d = x.shape
    grid = (nseq // sb, L // rb)
    const = lambda s, t: (0, 0)
    return pl.pallas_call(
        _inproj_kernel,
        out_shape=(jax.ShapeDtypeStruct((nseq, L, PROJ_COLS), BF16),
                   jax.ShapeDtypeStruct((nseq, L, GLA_QK), F32)),
        grid=grid,
        in_specs=[
            pl.BlockSpec((sb, rb, d), lambda s, t: (s, t, 0)),
            pl.BlockSpec((sb, 1, d), lambda s, t: (s, 0, 0)),
            pl.BlockSpec((sb, 1, d), lambda s, t: (s, 0, 0)),
            pl.BlockSpec((1, d), const),
            pl.BlockSpec((d, PROJ_COLS), const),
            pl.BlockSpec((d, LANE), const),
            pl.BlockSpec((LANE, 2 * GLA_QK), const),
            pl.BlockSpec((1, GLA_QK), const),
        ],
        out_specs=(pl.BlockSpec((sb, rb, PROJ_COLS), lambda s, t: (s, t, 0)),
                   pl.BlockSpec((sb, rb, GLA_QK), lambda s, t: (s, t, 0))),
        compiler_params=_cparams("arbitrary", "arbitrary"),
        name="inproj",
    )(x, sc, sh, n1, wm, wlr, wgk, bgk)


def _head_masks():
    lane = lax.broadcasted_iota(jnp.int32, (1, GLA_QK), 1)
    return [(lane // GLA_DK) == h for h in range(GLA_HEADS)]


def _stack_heads(x, hm):
    return jnp.concatenate([jnp.where(m, x, 0.0) for m in hm], axis=0).astype(BF16)


def _chunk_pair_ids(c_len):
    ti = lax.broadcasted_iota(jnp.int32, (GLA_HEADS * c_len, c_len), 0) % c_len
    si = lax.broadcasted_iota(jnp.int32, (GLA_HEADS * c_len, c_len), 1)
    return ti, si


def _scores_factored(q4s, k, G, c_len):
    ti, si = _chunk_pair_ids(c_len)
    ke = (k * jnp.exp(-G)).astype(BF16)
    return jnp.concatenate(
        [jnp.where(si <= ti, _dot_nt(q4, ke[c * c_len:(c + 1) * c_len]), 0.0) for c, q4 in enumerate(q4s)], axis=0)


def _scores_bounded(qs, k, G, c_len, hm):
    ti, si = _chunk_pair_ids(c_len)
    t = lax.broadcasted_iota(jnp.int32, (c_len, 1), 0)
    col = lax.broadcasted_iota(jnp.int32, (c_len, c_len), 1)
    outs = []
    for c in range(qs.shape[0] // c_len):
        sl = slice(c * c_len, (c + 1) * c_len)
        q_c, k_c, g_c = qs[sl], k[sl], G[sl]
        a = jnp.where(si == ti, _dot_nt(_stack_heads(q_c, hm), k_c.astype(BF16)), 0.0)
        half = c_len // 2
        while half >= 1:
            blk = 2 * half
            sel = (col == (t // blk) * blk + (half - 1)).astype(F32)
            ref = _dot(sel, g_c, precision=HIGHEST)
            upper = (t % blk) >= half
            qh = jnp.where(upper, q_c * jnp.exp(jnp.minimum(g_c - ref, 0.0)), 0.0)
            kh = jnp.where(upper, 0.0, k_c * jnp.exp(jnp.minimum(ref - g_c, 0.0)))
            same = (ti // blk) == (si // blk)
            a = a + jnp.where(same, _dot_nt(_stack_heads(qh, hm), kh.astype(BF16)), 0.0)
            half //= 2
        outs.append(a)
    return jnp.concatenate(outs, axis=0)


def _cum_log_gates(gkpre, c_len):
    rows = gkpre.shape[0]
    g = jax.nn.log_sigmoid(gkpre) / GLA_GATE_NORM
    ri = lax.broadcasted_iota(jnp.int32, (rows, rows), 0)
    ci = lax.broadcasted_iota(jnp.int32, (rows, rows), 1)
    tri = jnp.where((ci <= ri) & ((ci // c_len) == (ri // c_len)), 1.0, 0.0).astype(BF16)
    g_hi = g.astype(BF16)
    r1 = g - g_hi.astype(F32)
    g_mid = r1.astype(BF16)
    g_lo = (r1 - g_mid.astype(F32)).astype(BF16)
    parts = _dot(tri, jnp.concatenate([g_hi, g_mid, g_lo], axis=1))
    return parts[:, :GLA_QK] + parts[:, GLA_QK:2 * GLA_QK] + parts[:, 2 * GLA_QK:]


def _factored_is_safe(G, c_len):
    ends = [G[(c + 1) * c_len - 1:(c + 1) * c_len] for c in range(G.shape[0] // c_len)]
    return jnp.min(jnp.concatenate(ends, axis=0)) > GATE_SAFE_MIN


def _mixer_block(p, G, states, c_len, gn, lng, lnb, wm_ref, bsb, bounded):
    rows = p.shape[0]
    n_chunks = rows // c_len
    chained = len(states) == 1
    hm = _head_masks()

    q = p[:, Q0:Q0 + GLA_QK].astype(F32)
    k = p[:, K0:K0 + GLA_QK].astype(F32)
    v = p[:, V0:V0 + GLA_WIDTH]
    r = p[:, R0:R0 + GLA_WIDTH].astype(F32)

    qs = q * (GLA_DK ** -0.5)
    qe = qs * jnp.exp(G)
    q4s = [_stack_heads(qe[c * c_len:(c + 1) * c_len], hm) for c in range(n_chunks)]
    scores = _scores_bounded(qs, k, G, c_len, hm) if bounded else _scores_factored(q4s, k, G, c_len)
    hc = GLA_HEADS * c_len

    new_states = []
    o_rows = []
    st = states[0]
    for c in range(n_chunks):
        lo, hi = c * c_len, (c + 1) * c_len
        if not chained:
            st = states[c]
        q4 = q4s[c]
        a = scores[c * hc:(c + 1) * hc].astype(BF16)
        o_inter = _dot_nt(q4, st.astype(BF16))
        v_c = v[lo:hi]
        heads = []
        for h in range(GLA_HEADS):
            o_h = o_inter[h * c_len:(h + 1) * c_len] + _dot(
                a[h * c_len:(h + 1) * c_len], v_c[:, h * GLA_DV:(h + 1) * GLA_DV])
            heads.append(o_h)
        o_rows.append(jnp.concatenate(heads, axis=1))
        g_last = G[hi - 1:hi]
        kd = (k[lo:hi] * jnp.exp(g_last - G[lo:hi])).astype(BF16)
        upd = _dot_tn(v_c, kd)
        st_new = jnp.exp(g_last) * st
        for h in range(GLA_HEADS):
            st_new = st_new + jnp.where(hm[h], upd[h * GLA_DV:(h + 1) * GLA_DV], 0.0)
        if chained:
            st = st_new
        else:
            new_states.append(st_new)
    if chained:
        new_states = [st]
    o = jnp.concatenate(o_rows, axis=0)

    gla = []
    for h in range(GLA_HEADS):
        o_h = o[:, h * GLA_DV:(h + 1) * GLA_DV]
        ms = jnp.mean(o_h * o_h, axis=-1, keepdims=True)
        r_h = r[:, h * GLA_DV:(h + 1) * GLA_DV]
        gla.append(o_h * lax.rsqrt(ms + EPS) * gn * (r_h * jax.nn.sigmoid(r_h)))

    u = _gelu(p[:, U0:U0 + GMLP_WIDTH].astype(F32))
    vv = _gelu(p[:, G0:G0 + GMLP_WIDTH].astype(F32))
    mu = jnp.mean(vv, axis=-1, keepdims=True)
    xc = vv - mu
    var = jnp.mean(xc * xc, axis=-1, keepdims=True)
    vn = xc * lax.rsqrt(var + EPS) * lng + lnb
    vnb = vn.astype(BF16)
    gm = []
    for gi in range(GMLP_GROUPS):
        sl = slice(gi * GMLP_GC, (gi + 1) * GMLP_GC)
        mixed = _dot(wm_ref[gi], vnb[:, sl]) + bsb[:, sl]
        gm.append(u[:, sl] * mixed)
    out = jnp.concatenate(gla + gm, axis=1)
    return out, new_states, vn


def _mixer_prompt_kernel(proj_ref, gk_ref, gn_ref, lng_ref, lnb_ref, wm_ref, bsb_ref, mix_ref, s_ref, st_scr):
    t = pl.program_id(1)
    nt = pl.num_programs(1)

    @pl.when(t == 0)
    def _():
        st_scr[...] = jnp.zeros_like(st_scr)

    n_sub = proj_ref.shape[1] // ROW_BLOCK
    Gs = [_cum_log_gates(gk_ref[0, j * ROW_BLOCK:(j + 1) * ROW_BLOCK, :], CHUNK) for j in range(n_sub)]
    safe = _factored_is_safe(jnp.concatenate(Gs, axis=0), CHUNK)

    def run(bounded):
        st = st_scr[...]
        for j in range(n_sub):
            rows = slice(j * ROW_BLOCK, (j + 1) * ROW_BLOCK)
            out, sts, _ = _mixer_block(proj_ref[0, rows, :], Gs[j], [st], CHUNK, gn_ref[...], lng_ref[...],
                                       lnb_ref[...], wm_ref, bsb_ref[...], bounded)
            st = sts[0]
            mix_ref[0, rows, :] = out.astype(BF16)
        st_scr[...] = st

    pl.when(safe)(functools.partial(run, False))
    pl.when(jnp.logical_not(safe))(functools.partial(run, True))

    @pl.when(t == nt - 1)
    def _():
        s_ref[0] = st_scr[...].T.reshape(GLA_HEADS, GLA_DK, GLA_DV)


def _mixer_prompt(proj, gk, gn, lng, lnb, wm, bsb, tb):
    b, L, _ = proj.shape
    const2 = lambda s, t: (0, 0)
    return pl.pallas_call(
        _mixer_prompt_kernel,
        out_shape=(jax.ShapeDtypeStruct((b, L, 2 * GLA_WIDTH), BF16),
                   jax.ShapeDtypeStruct((b, GLA_HEADS, GLA_DK, GLA_DV), F32)),
        grid=(b, L // tb),
        in_specs=[
            pl.BlockSpec((1, tb, PROJ_COLS), lambda s, t: (s, t, 0)),
            pl.BlockSpec((1, tb, GLA_QK), lambda s, t: (s, t, 0)),
            pl.BlockSpec((1, GLA_DV), const2),
            pl.BlockSpec((1, GMLP_WIDTH), const2),
            pl.BlockSpec((1, GMLP_WIDTH), const2),
            pl.BlockSpec((GMLP_GROUPS, ROW_BLOCK, ROW_BLOCK), lambda s, t: (0, 0, 0)),
            pl.BlockSpec((ROW_BLOCK, GMLP_WIDTH), const2),
        ],
        out_specs=(pl.BlockSpec((1, tb, 2 * GLA_WIDTH), lambda s, t: (s, t, 0)),
                   pl.BlockSpec((1, GLA_HEADS, GLA_DK, GLA_DV), lambda s, t: (s, 0, 0, 0))),
        scratch_shapes=[pltpu.VMEM((GLA_DV, GLA_QK), F32)],
        compiler_params=_cparams("arbitrary", "arbitrary"),
        name="mixer_prompt",
    )(proj, gk, gn, lng, lnb, wm, bsb)


def _mixer_sample_kernel(proj_ref, gk_ref, s0_ref, gn_ref, lng_ref, lnb_ref, wm_ref, bsb_ref,
                         mix_ref, s_ref, vn_ref):
    sb, rb, _ = proj_ref.shape
    G = _cum_log_gates(gk_ref[...].reshape(sb * rb, GLA_QK), rb)
    safe = _factored_is_safe(G, rb)

    def run(bounded):
        p = proj_ref[...].reshape(sb * rb, PROJ_COLS)
        states = [s0_ref[i].reshape(GLA_QK, GLA_DV).T for i in range(sb)]
        out, sts, vn = _mixer_block(p, G, states, rb, gn_ref[...], lng_ref[...], lnb_ref[...], wm_ref,
                                    bsb_ref[...], bounded)
        mix_ref[...] = out.astype(BF16).reshape(sb, rb, 2 * GLA_WIDTH)
        vn_ref[...] = vn.reshape(sb, rb, GMLP_WIDTH)
        for i in range(sb):
            s_ref[i] = sts[i].T.reshape(GLA_HEADS, GLA_DK, GLA_DV)

    pl.when(safe)(functools.partial(run, False))
    pl.when(jnp.logical_not(safe))(functools.partial(run, True))


def _mixer_sample(proj, gk, s0, gn, lng, lnb, wm, bsb):
    n, L, _ = proj.shape
    sb = ROW_BLOCK // L
    const2 = lambda s: (0, 0)
    return pl.pallas_call(
        _mixer_sample_kernel,
        out_shape=(jax.ShapeDtypeStruct((n, L, 2 * GLA_WIDTH), BF16),
                   jax.ShapeDtypeStruct((n, GLA_HEADS, GLA_DK, GLA_DV), F32),
                   jax.ShapeDtypeStruct((n, L, GMLP_WIDTH), F32)),
        grid=(n // sb,),
        in_specs=[
            pl.BlockSpec((sb, L, PROJ_COLS), lambda s: (s, 0, 0)),
            pl.BlockSpec((sb, L, GLA_QK), lambda s: (s, 0, 0)),
            pl.BlockSpec((sb, GLA_HEADS, GLA_DK, GLA_DV), lambda s: (s, 0, 0, 0)),
            pl.BlockSpec((1, GLA_DV), const2),
            pl.BlockSpec((1, GMLP_WIDTH), const2),
            pl.BlockSpec((1, GMLP_WIDTH), const2),
            pl.BlockSpec((GMLP_GROUPS, ROW_BLOCK, ROW_BLOCK), lambda s: (0, 0, 0)),
            pl.BlockSpec((ROW_BLOCK, GMLP_WIDTH), const2),
        ],
        out_specs=(pl.BlockSpec((sb, L, 2 * GLA_WIDTH), lambda s: (s, 0, 0)),
                   pl.BlockSpec((sb, GLA_HEADS, GLA_DK, GLA_DV), lambda s: (s, 0, 0, 0)),
                   pl.BlockSpec((sb, L, GMLP_WIDTH), lambda s: (s, 0, 0))),
        compiler_params=_cparams("arbitrary"),
        name="mixer_sample",
    )(proj, gk, s0, gn, lng, lnb, wm, bsb)


def _outproj_kernel(mix_ref, x_ref, g1_ref, sc_ref, sh_ref, n2_ref, wo_ref, wr_ref, br_ref, upper_ref, lower_ref,
                    xmid_ref, h2_ref, topw_ref, slot_ref, n16_ref):
    sb, rb, d = x_ref.shape
    tm = sb * rb

    y = _dot(mix_ref[...].reshape(tm, d), wo_ref[...])
    xm = x_ref[...] + g1_ref[...] * y.reshape(sb, rb, d)
    xmid_ref[...] = xm
    ms = jnp.mean(xm * xm, axis=-1, keepdims=True)
    h2 = xm * lax.rsqrt(ms + EPS) * n2_ref[...]
    h2 = (h2 * (1.0 + sc_ref[...]) + sh_ref[...]).reshape(tm, d)
    h_hi = h2.astype(BF16)
    h2_ref[...] = h_hi

    h_lo = (h2 - h_hi.astype(F32)).astype(BF16)
    part = _dot(h_hi, wr_ref[...])
    logits = part[:, :LANE] + part[:, LANE:] + _dot(h_lo, wr_ref[:, :LANE]) + br_ref[...]
    l = logits.T[:N_EXPERTS]
    eid = lax.broadcasted_iota(jnp.int32, (N_EXPERTS, tm), 0).astype(F32)
    top_l, sel = [], []
    for _ in range(TOP_K):
        m = jnp.max(l, axis=0, keepdims=True)
        idx = jnp.min(jnp.where(l == m, eid, float(N_EXPERTS)), axis=0, keepdims=True)
        hit = eid == idx
        top_l.append(m)
        sel.append(hit)
        l = jnp.where(hit, -jnp.inf, l)
    ex = [jnp.exp(t - top_l[0]) for t in top_l]
    den = ex[0] + ex[1] + ex[2] + ex[3]
    topw_ref[...] = jnp.concatenate([e / den for e in ex], axis=0)

    chosen = (sel[0] | sel[1] | sel[2] | sel[3])
    cb = jnp.where(chosen, 1.0, 0.0)
    before = _dot(cb.astype(BF16), upper_ref[...])
    n = jnp.sum(cb, axis=1, keepdims=True)
    n16 = jnp.floor((n + (ROW_UNIT - 1)) * (1.0 / ROW_UNIT)) * ROW_UNIT
    n16b = jnp.broadcast_to(n16, (N_EXPERTS, LANE))
    ls = _dot(lower_ref[...], n16b, precision=HIGHEST)[:, 0:1]
    base = before + ls
    slot_ref[...] = jnp.concatenate(
        [jnp.sum(jnp.where(s, base, 0.0), axis=0, keepdims=True) for s in sel], axis=0).astype(jnp.int32)
    n16_ref[0] = n16b


def _outproj(mix, x, g1, sc2, sh2, n2, wo, wr, br, upper, lower, sb, rb):
    nseq, L, d = x.shape
    tm = sb * rb
    nt = L // rb
    T = nseq * L
    grid = (nseq // sb, nt)
    const = lambda s, t: (0, 0)
    tok = lambda s, t: (0, s * nt + t)
    return pl.pallas_call(
        _outproj_kernel,
        out_shape=(jax.ShapeDtypeStruct((nseq, L, d), F32),
                   jax.ShapeDtypeStruct((T, d), BF16),
                   jax.ShapeDtypeStruct((TOP_K, T), F32),
                   jax.ShapeDtypeStruct((TOP_K, T), jnp.int32),
                   jax.ShapeDtypeStruct((T // tm, N_EXPERTS, LANE), F32)),
        grid=grid,
        in_specs=[
            pl.BlockSpec((sb, rb, d), lambda s, t: (s, t, 0)),
            pl.BlockSpec((sb, rb, d), lambda s, t: (s, t, 0)),
            pl.BlockSpec((sb, 1, d), lambda s, t: (s, 0, 0)),
            pl.BlockSpec((sb, 1, d), lambda s, t: (s, 0, 0)),
            pl.BlockSpec((sb, 1, d), lambda s, t: (s, 0, 0)),
            pl.BlockSpec((1, d), const),
            pl.BlockSpec((d, d), const),
            pl.BlockSpec((d, 2 * LANE), const),
            pl.BlockSpec((1, LANE), const),
            pl.BlockSpec((tm, tm), const),
            pl.BlockSpec((N_EXPERTS, N_EXPERTS), const),
        ],
        out_specs=(pl.BlockSpec((sb, rb, d), lambda s, t: (s, t, 0)),
                   pl.BlockSpec((tm, d), lambda s, t: (s * nt + t, 0)),
                   pl.BlockSpec((TOP_K, tm), tok),
                   pl.BlockSpec((TOP_K, tm), tok),
                   pl.BlockSpec((1, N_EXPERTS, LANE), lambda s, t: (s * nt + t, 0, 0))),
        compiler_params=_cparams("arbitrary", "arbitrary"),
        name="outproj",
    )(mix, x, g1, sc2, sh2, n2, wo, wr, br, upper, lower)


def _unit_copy(src_ref, src_row, dst_ref, dst_row, sem):
    return pltpu.make_async_copy(src_ref.at[pl.ds(pl.multiple_of(src_row, ROW_UNIT), ROW_UNIT)],
                                 dst_ref.at[pl.ds(pl.multiple_of(dst_row, ROW_UNIT), ROW_UNIT)], sem)


def _start_unit_copies(nun, copy_of):
    def issue_pair(j, carry):
        copy_of(2 * j).start()
        copy_of(2 * j + 1).start()
        return carry

    lax.fori_loop(0, nun // 2, issue_pair, 0)

    @pl.when(nun % 2 == 1)
    def _():
        copy_of(nun - 1).start()


def _wait_unit_copies(nun, copy_of):
    def drain(u, carry):
        copy_of(0).wait()
        return carry

    lax.fori_loop(0, nun, drain, 0)


def _dispatch_body(udst_ref, nun_ref, slot_ref, h2_ref, xs_ref, xloc, sem):
    i = pl.program_id(0)
    tb = slot_ref.shape[1]
    nun = nun_ref[i]
    s = slot_ref[...]
    h2 = h2_ref[...]
    def sort_chunk(c):
        r = lax.broadcasted_iota(jnp.int32, (SORT_CHUNK, tb), 0) + c * SORT_CHUNK
        hit = (s[0:1] == r) | (s[1:2] == r) | (s[2:3] == r) | (s[3:4] == r)
        p = jnp.where(hit, 1.0, 0.0).astype(BF16)
        xloc[c * SORT_CHUNK:(c + 1) * SORT_CHUNK, :] = _dot(p, h2).astype(BF16)

    typical_rows = TOP_K * tb + N_EXPERTS * ROW_UNIT // 2
    for c in range(LOCAL_ROWS // SORT_CHUNK):
        if (c + 1) * SORT_CHUNK <= typical_rows:
            sort_chunk(c)
        else:
            pl.when(c * (SORT_CHUNK // ROW_UNIT) < nun)(functools.partial(sort_chunk, c))

    u0 = i * (LOCAL_ROWS // ROW_UNIT)
    copy_of = lambda u: _unit_copy(xloc, u * ROW_UNIT, xs_ref, udst_ref[u0 + u], sem)
    _start_unit_copies(nun, copy_of)
    _wait_unit_copies(nun, copy_of)


def _dispatch_first_kernel(udst_ref, nun_ref, slot_ref, h2_ref, xs_ref, xloc, sem):
    _dispatch_body(udst_ref, nun_ref, slot_ref, h2_ref, xs_ref, xloc, sem)


def _dispatch_next_kernel(udst_ref, nun_ref, slot_ref, h2_ref, xs_in_ref, xs_ref, xloc, sem):
    del xs_in_ref
    _dispatch_body(udst_ref, nun_ref, slot_ref, h2_ref, xs_ref, xloc, sem)


def _dispatch(udst, nun, slot, h2, xs, n_rows, tb):
    T, d = h2.shape
    any_spec = pl.BlockSpec(memory_space=pl.ANY)
    in_specs = [pl.BlockSpec((TOP_K, tb), lambda i, *_: (0, i)),
                pl.BlockSpec((tb, d), lambda i, *_: (i, 0))]
    aliases = {}
    body = _dispatch_first_kernel
    args = (udst, nun, slot, h2)
    if xs is not None:
        in_specs.append(any_spec)
        aliases = {4: 0}
        body = _dispatch_next_kernel
        args = args + (xs,)
    return pl.pallas_call(
        body,
        out_shape=jax.ShapeDtypeStruct((n_rows, d), BF16),
        grid_spec=pltpu.PrefetchScalarGridSpec(
            num_scalar_prefetch=2, grid=(T // tb,), in_specs=in_specs, out_specs=any_spec,
            scratch_shapes=[pltpu.VMEM((LOCAL_ROWS, d), BF16), pltpu.SemaphoreType.DMA]),
        input_output_aliases=aliases,
        compiler_params=pltpu.CompilerParams(dimension_semantics=("arbitrary",), has_side_effects=True,
                                             vmem_limit_bytes=VMEM_LIMIT),
        name="dispatch_next" if xs is not None else "dispatch_first",
    )(*args)


def _experts_kernel(te_ref, tv_ref, tf_ref, tn_ref, xs_ref, wg_hbm, bg_ref, wu_hbm, bu_ref, wd_hbm, bd_ref, y_ref,
                    stage_g, stage_u, stage_d, wgb, wub, wdb, sem):
    i = pl.program_id(0)
    valid = tv_ref[i]
    tm, d = xs_ref.shape
    ff = wgb.shape[1]

    def weight_copies(e):
        return [pltpu.make_async_copy(src.at[e], dst, sem.at[j])
                for j, (src, dst) in enumerate(((wg_hbm, stage_g), (wu_hbm, stage_u), (wd_hbm, stage_d)))]

    @pl.when(i == 0)
    def _():
        for cp in weight_copies(te_ref[0]):
            cp.start()

    @pl.when(tf_ref[i] == 1)
    def _():
        for cp in weight_copies(te_ref[i]):
            cp.wait()
        wgb[...] = stage_g[...].astype(BF16)
        wub[...] = stage_u[...].astype(BF16)
        wdb[...] = stage_d[...].astype(BF16)

        @pl.when(tn_ref[i] >= 0)
        def _():
            for cp in weight_copies(tn_ref[i]):
                cp.start()

    @pl.when(valid > 0)
    def _():
        row = lax.broadcasted_iota(jnp.int32, (tm, 1), 0)
        xb = jnp.where(row < valid, xs_ref[...], jnp.zeros((), BF16))
        acc = jnp.zeros((tm, d), F32)
        for c in range(ff // FF_CHUNK):
            cs = slice(c * FF_CHUNK, (c + 1) * FF_CHUNK)
            gate = jnp.minimum(_dot(xb, wgb[:, cs]) + bg_ref[0, :, cs], SWIGLU_LIMIT)
            up = jnp.clip(_dot(xb, wub[:, cs]) + bu_ref[0, :, cs], -SWIGLU_LIMIT, SWIGLU_LIMIT)
            act = (up + 1.0) * gate * jax.nn.sigmoid(SWIGLU_ALPHA * gate)
            acc = acc + _dot(act.astype(BF16), wdb[cs, :])
        y_ref[...] = (acc + bd_ref[0]).astype(BF16)

    @pl.when(valid == 0)
    def _():
        y_ref[...] = jnp.zeros_like(y_ref)


def _experts(tile_expert, tile_valid, tile_first, tile_next, xs, wg, bg, wu, bu, wd, bd, tmg):
    n_tiles = tile_expert.shape[0]
    _, d, ff = wg.shape
    bspec = lambda shp: pl.BlockSpec(shp, lambda i, te, *_: (te[i], 0, 0))
    any_spec = pl.BlockSpec(memory_space=pl.ANY)
    return pl.pallas_call(
        _experts_kernel,
        out_shape=jax.ShapeDtypeStruct(xs.shape, BF16),
        grid_spec=pltpu.PrefetchScalarGridSpec(
            num_scalar_prefetch=4,
            grid=(n_tiles,),
            in_specs=[
                pl.BlockSpec((tmg, d), lambda i, *_: (i, 0)),
                any_spec, bspec((1, 1, ff)),
                any_spec, bspec((1, 1, ff)),
                any_spec, bspec((1, 1, d)),
            ],
            out_specs=pl.BlockSpec((tmg, d), lambda i, *_: (i, 0)),
            scratch_shapes=[pltpu.VMEM((d, ff), F32), pltpu.VMEM((d, ff), F32), pltpu.VMEM((ff, d), F32),
                            pltpu.VMEM((d, ff), BF16), pltpu.VMEM((d, ff), BF16), pltpu.VMEM((ff, d), BF16),
                            pltpu.SemaphoreType.DMA((3,))],
        ),
        compiler_params=pltpu.CompilerParams(dimension_semantics=("arbitrary",), vmem_limit_bytes=EXPERTS_VMEM_LIMIT),
        name="experts",
    )(tile_expert, tile_valid, tile_first, tile_next, xs, wg, bg, wu, bu, wd, bd)


def _combine_kernel(udst_ref, nun_ref, y_ref, slot_ref, tw_ref, xmid_ref, g2_ref, nf_ref, o_ref, ybuf, pw_scr, sem):
    sb, rb, d = xmid_ref.shape
    tb = sb * rb
    i = pl.program_id(0) * pl.num_programs(1) + pl.program_id(1)
    nun = nun_ref[i]

    @pl.when(i == 0)
    def _():
        ybuf[...] = jnp.zeros_like(ybuf)

    u0 = i * (LOCAL_ROWS // ROW_UNIT)
    copy_of = lambda u: _unit_copy(y_ref, udst_ref[u0 + u], ybuf, u * ROW_UNIT, sem)
    _start_unit_copies(nun, copy_of)

    st = slot_ref[...]
    tw = tw_ref[...]
    lane = lax.broadcasted_iota(jnp.int32, (tb, PERM_CHUNK), 1)
    sk = [jnp.broadcast_to(st[:, k:k + 1], (tb, PERM_CHUNK)) for k in range(TOP_K)]
    wk = [jnp.broadcast_to(tw[:, k:k + 1], (tb, PERM_CHUNK)) for k in range(TOP_K)]
    def weights_chunk(c):
        r = lane + c * PERM_CHUNK
        pw = jnp.zeros((tb, PERM_CHUNK), F32)
        for k in range(TOP_K):
            pw = jnp.where(sk[k] == r, wk[k], pw)
        return pw.astype(BF16)

    n_chunks = LOCAL_ROWS // PERM_CHUNK
    n_early = pw_scr.shape[1] // PERM_CHUNK
    for c in range(n_early):
        pw_scr[:, c * PERM_CHUNK:(c + 1) * PERM_CHUNK] = weights_chunk(c)

    _wait_unit_copies(nun, copy_of)
    moe = _dot(pw_scr[...], ybuf[:n_early * PERM_CHUNK, :])
    for c in range(n_early, n_chunks):
        moe = moe + _dot(weights_chunk(c), ybuf[c * PERM_CHUNK:(c + 1) * PERM_CHUNK, :])
    out = xmid_ref[...] + g2_ref[...] * moe.reshape(sb, rb, d)
    ms = jnp.mean(out * out, axis=-1, keepdims=True)
    o_ref[...] = out * lax.rsqrt(ms + EPS) * nf_ref[...]


def _combine(udst, nun, y, slot_t, tw, xmid, g2, nf, sb, rb):
    nseq, L, d = xmid.shape
    tb = sb * rb
    nt = L // rb
    tok = lambda s, t, *_: (s * nt + t, 0)
    return pl.pallas_call(
        _combine_kernel,
        out_shape=jax.ShapeDtypeStruct((nseq, L, d), F32),
        grid_spec=pltpu.PrefetchScalarGridSpec(
            num_scalar_prefetch=2,
            grid=(nseq // sb, nt),
            in_specs=[
                pl.BlockSpec(memory_space=pl.ANY),
                pl.BlockSpec((tb, TOP_K), tok),
                pl.BlockSpec((tb, TOP_K), tok),
                pl.BlockSpec((sb, rb, d), lambda s, t, *_: (s, t, 0)),
                pl.BlockSpec((sb, 1, d), lambda s, t, *_: (s, 0, 0)),
                pl.BlockSpec((1, d), lambda s, t, *_: (0, 0)),
            ],
            out_specs=pl.BlockSpec((sb, rb, d), lambda s, t, *_: (s, t, 0)),
            scratch_shapes=[pltpu.VMEM((LOCAL_ROWS, d), BF16), pltpu.VMEM((tb, EARLY_WEIGHT_COLS), BF16),
                            pltpu.SemaphoreType.DMA],
        ),
        compiler_params=_cparams("arbitrary", "arbitrary"),
        name="combine",
    )(udst, nun, y, slot_t, tw, xmid, g2, nf)


def _tile_rows(nseq, L, tile):
    if L >= tile:
        assert L % tile == 0
        return 1, tile
    assert tile % L == 0 and nseq % (tile // L) == 0
    return tile // L, L


def kernel(x_prompt, x_sample, state_gla, c_prompt, c_sample, w_ada, b_ada, norm1, w_in, w_gk, b_gk, gla_norm,
           gmlp_ln_g, gmlp_ln_b, gmlp_w_s, gmlp_b_s, w_out, norm2, w_router, b_router, w_gate, b_gate, w_up,
           b_up, w_down, b_down, norm_f):
    depth = w_ada.shape[0]
    assert depth == 1
    bp, lp, d = x_prompt.shape
    bs, ls, _ = x_sample.shape
    tp, ts = bp * lp, bs * ls

    nc = bp + bs
    ncp = -(-nc // SUBLANE) * SUBLANE
    c_all = jnp.concatenate([c_prompt, c_sample, jnp.zeros((ncp - nc, d), F32)], axis=0)
    mod = _ada(c_all, w_ada[0], b_ada[0][None]).reshape(ncp, N_MOD, 1, d)
    mods_p = [mod[:bp, i] for i in range(N_MOD)]
    mods_s = [mod[bp:nc, i] for i in range(N_MOD)]

    wi = w_in[0]
    c_lr = 2 * GLA_QK + GLA_WIDTH
    c_r = c_lr + GLA_LOWRANK
    wm = jnp.concatenate([wi[:, :c_lr], wi[:, c_r:]], axis=1).astype(BF16)
    wlr = jnp.pad(wi[:, c_lr:c_r], ((0, 0), (0, LANE - GLA_LOWRANK))).astype(BF16)
    wgk_f = jnp.pad(w_gk[0], ((0, LANE - GLA_LOWRANK), (0, 0)))
    wgk_hi = wgk_f.astype(BF16)
    wgk = jnp.concatenate([wgk_hi, (wgk_f - wgk_hi.astype(F32)).astype(BF16)], axis=1)
    bgk = b_gk[0][None]
    n1, n2, nf = norm1[0][None], norm2[0][None], norm_f[None]
    gn, lng, lnb = gla_norm[0][None], gmlp_ln_g[0][None], gmlp_ln_b[0][None]
    ws, bsv = gmlp_w_s[0], gmlp_b_s[0]
    pos_i = jnp.arange(GMLP_BLOCK)
    cmask = (pos_i[None, :] // CHUNK) <= (pos_i[:, None] // CHUNK)
    wm_p = jnp.where(cmask[None], ws, 0.0).astype(BF16)
    bsb_p = jnp.repeat(bsv.T, GMLP_GC, axis=1)
    reps = ROW_BLOCK // ls
    eye = jnp.eye(reps, dtype=F32)
    wm_s = jnp.einsum("ab,gij->gaibj", eye, ws[:, :ls, :ls]).reshape(GMLP_GROUPS, ROW_BLOCK, ROW_BLOCK).astype(BF16)
    bsb_s = jnp.tile(jnp.repeat(bsv[:, :ls].T, GMLP_GC, axis=1), (reps, 1))
    wo = w_out[0].astype(BF16)
    wr_f = jnp.pad(w_router[0], ((0, 0), (0, LANE - N_EXPERTS)))
    wr_hi = wr_f.astype(BF16)
    wr = jnp.concatenate([wr_hi, (wr_f - wr_hi.astype(F32)).astype(BF16)], axis=1)
    br = jnp.concatenate([b_router[0], jnp.full((LANE - N_EXPERTS,), -1e30, F32)])[None]
    upper = (jnp.arange(TOKEN_TILE)[:, None] < jnp.arange(TOKEN_TILE)[None, :]).astype(BF16)
    lower = (jnp.arange(N_EXPERTS)[None, :] < jnp.arange(N_EXPERTS)[:, None]).astype(F32)
    wg, wu, wd = w_gate[0], w_up[0], w_down[0]
    bg, bu, bd = b_gate[0][:, None], b_up[0][:, None], b_down[0][:, None]

    sbp, rbp = _tile_rows(bp, lp, TOKEN_TILE)
    sbs, rbs = _tile_rows(bs, ls, TOKEN_TILE)

    proj_p, gk_p = _inproj(x_prompt, mods_p[1], mods_p[0], n1, wm, wlr, wgk, bgk, sbp, rbp)
    proj_s, gk_s = _inproj(x_sample, mods_s[1], mods_s[0], n1, wm, wlr, wgk, bgk, sbs, rbs)
    mix_p, state_p = _mixer_prompt(proj_p, gk_p, gn, lng, lnb, wm_p, bsb_p, TOKEN_TILE)
    mix_s, state_s, vn_s = _mixer_sample(proj_s, gk_s, state_gla[0], gn, lng, lnb, wm_s, bsb_s)

    xmid_p, h2_p, topw_p, slot_p, n16_p = _outproj(
        mix_p, x_prompt, mods_p[2], mods_p[4], mods_p[3], n2, wo, wr, br, upper, lower, sbp, rbp)
    xmid_s, h2_s, topw_s, slot_s, n16_s = _outproj(
        mix_s, x_sample, mods_s[2], mods_s[4], mods_s[3], n2, wo, wr, br, upper, lower, sbs, rbs)

    tmg = EXPERT_TILE
    ntp = tp // TOKEN_TILE
    eids = jnp.arange(N_EXPERTS, dtype=jnp.int32)
    n16 = jnp.concatenate([n16_p[:, :, 0], n16_s[:, :, 0]], axis=0).astype(jnp.int32)
    nt_all = n16.shape[0]
    earlier = jnp.cumsum(n16, axis=0) - n16
    tot = jnp.sum(n16, axis=0)
    tiles_e = (tot + tmg - 1) // tmg
    tile_end = jnp.cumsum(tiles_e)
    tile_start = tile_end - tiles_e
    row_start = tile_start * tmg
    n_tiles = (TOP_K * (tp + ts) + nt_all * N_EXPERTS * (ROW_UNIT - 1)) // tmg + N_EXPERTS
    tid = jnp.arange(n_tiles, dtype=jnp.int32)
    te = jnp.minimum(jnp.sum((tid[:, None] >= tile_end[None, :]).astype(jnp.int32), axis=1), N_EXPERTS - 1)
    te_hot = te[:, None] == eids[None, :]
    tot_te = jnp.sum(jnp.where(te_hot, tot[None, :], 0), axis=1)
    start_te = jnp.sum(jnp.where(te_hot, tile_start[None, :], 0), axis=1)
    active = tid < tile_end[-1]
    tv = jnp.where(active, jnp.clip(tot_te - (tid - start_te) * tmg, 0, tmg), 0).astype(jnp.int32)
    last_e = jnp.max(jnp.where(tiles_e > 0, eids, 0)).astype(jnp.int32)
    te = jnp.where(active, te, last_e).astype(jnp.int32)
    te_prev = jnp.concatenate([jnp.full((1,), -1, jnp.int32), te[:-1]])
    tf = (active & (te != te_prev)).astype(jnp.int32)
    later_used = (eids[None, :] > te[:, None]) & (tiles_e[None, :] > 0)
    tn = jnp.min(jnp.where(later_used, eids[None, :], N_EXPERTS), axis=1)
    tn = jnp.where(tn < N_EXPERTS, tn, -1).astype(jnp.int32)

    run_end = jnp.cumsum(n16, axis=1)
    run_start = run_end - n16
    unit_row = jnp.arange(LOCAL_ROWS // ROW_UNIT, dtype=jnp.int32) * ROW_UNIT
    unit_e = jnp.sum((unit_row[None, :, None] >= run_end[:, None, :]).astype(jnp.int32), axis=2)
    unit_hot = unit_e[:, :, None] == eids[None, None, :]
    run_off = row_start[None, :] + earlier - run_start
    udst = jnp.sum(jnp.where(unit_hot, run_off[:, None, :], 0), axis=2) + unit_row[None, :]
    udst = jnp.where(unit_row[None, :] < run_end[:, -1:], udst, 0).astype(jnp.int32)
    nun = (run_end[:, -1] // ROW_UNIT).astype(jnp.int32)

    n_rows = n_tiles * tmg
    udst_p, udst_s = udst[:ntp].reshape(-1), udst[ntp:].reshape(-1)
    xs = _dispatch(udst_p, nun[:ntp], slot_p, h2_p, None, n_rows, TOKEN_TILE)
    xs = _dispatch(udst_s, nun[ntp:], slot_s, h2_s, xs, n_rows, TOKEN_TILE)
    y = _experts(te, tv, tf, tn, xs, wg, bg, wu, bu, wd, bd, tmg)
    y_prompt = _combine(udst_p, nun[:ntp], y, slot_p.T, topw_p.T, xmid_p, mods_p[5], nf, sbp, rbp)
    y_sample = _combine(udst_s, nun[ntp:], y, slot_s.T, topw_s.T, xmid_s, mods_s[5], nf, sbs, rbs)

    return (y_prompt, y_sample, state_p[None], state_s[None], vn_s[None])
```

```python
import functools

import jax
import jax.numpy as jnp
from jax import lax
from jax.experimental import pallas as pl
from jax.experimental.pallas import tpu as pltpu

F32 = jnp.float32
BF16 = jnp.bfloat16
HIGHEST = lax.Precision.HIGHEST

CHUNK = 64
GLA_HEADS = 4
GLA_DK = 64
GLA_DV = 128
GLA_QK = GLA_HEADS * GLA_DK
GLA_WIDTH = GLA_HEADS * GLA_DV
GLA_LOWRANK = 16
GLA_GATE_NORM = 16.0
GATE_SAFE_MIN = -60.0
GMLP_WIDTH = 512
GMLP_GROUPS = 4
GMLP_GC = GMLP_WIDTH // GMLP_GROUPS
GMLP_BLOCK = 128
N_EXPERTS = 32
TOP_K = 4
SWIGLU_LIMIT = 7.0
SWIGLU_ALPHA = 1.702
EPS = 1e-6
N_MOD = 6

LANE = 128
SUBLANE = 8

Q0 = 0
K0 = Q0 + GLA_QK
V0 = K0 + GLA_QK
R0 = V0 + GLA_WIDTH
U0 = R0 + GLA_WIDTH
G0 = U0 + GMLP_WIDTH
PROJ_COLS = G0 + GMLP_WIDTH

ROW_BLOCK = 128
TOKEN_TILE = 512
EXPERT_TILE = 512
FF_CHUNK = 512
ROW_UNIT = 16
LOCAL_ROWS = 2560
SORT_CHUNK = 128
PERM_CHUNK = 256
VMEM_LIMIT = 48 * 1024 * 1024
EXPERTS_VMEM_LIMIT = 56 * 1024 * 1024


def _cparams(*sem):
    return pltpu.CompilerParams(dimension_semantics=sem, vmem_limit_bytes=VMEM_LIMIT)


def _dot(a, b, **kw):
    return jnp.dot(a, b, preferred_element_type=F32, **kw)


def _dot_nt(a, b):
    return lax.dot_general(a, b, (((1,), (1,)), ((), ())), preferred_element_type=F32)


def _gelu(x):
    return 0.5 * x * (1.0 + lax.erf(x * (0.5 ** 0.5)))


def _dot_tn(a, b):
    return lax.dot_general(a, b, (((0,), (0,)), ((), ())), preferred_element_type=F32)


def _ada_kernel(c_ref, w_ref, b_ref, o_ref):
    c = c_ref[...]
    s = c * jax.nn.sigmoid(c)
    o_ref[...] = _dot(s, w_ref[...], precision=HIGHEST) + b_ref[...]


def _ada(c, w_ada, b_ada):
    n, d = c.shape
    return pl.pallas_call(
        _ada_kernel,
        out_shape=jax.ShapeDtypeStruct((n, N_MOD * d), F32),
        grid=(N_MOD,),
        in_specs=[
            pl.BlockSpec((n, d), lambda j: (0, 0)),
            pl.BlockSpec((d, d), lambda j: (0, j)),
            pl.BlockSpec((1, d), lambda j: (0, j)),
        ],
        out_specs=pl.BlockSpec((n, d), lambda j: (0, j)),
        compiler_params=_cparams("arbitrary"),
        name="ada",
    )(c, w_ada, b_ada)


def _inproj_kernel(x_ref, sc_ref, sh_ref, n1_ref, wm_ref, wlr_ref, wgk_ref, bgk_ref, proj_ref, gk_ref):
    sb, rb, d = x_ref.shape
    x = x_ref[...]
    ms = jnp.mean(x * x, axis=-1, keepdims=True)
    h = x * lax.rsqrt(ms + EPS) * n1_ref[...]
    h = h * (1.0 + sc_ref[...]) + sh_ref[...]
    hb = h.reshape(sb * rb, d).astype(BF16)
    cw = 512
    for c in range(PROJ_COLS // cw):
        p = _dot(hb, wm_ref[:, c * cw:(c + 1) * cw])
        proj_ref[:, :, c * cw:(c + 1) * cw] = p.astype(BF16).reshape(sb, rb, cw)
    lr = _dot(hb, wlr_ref[...])
    lr_hi = lr.astype(BF16)
    lr_lo = (lr - lr_hi.astype(F32)).astype(BF16)
    part = _dot(lr_hi, wgk_ref[...])
    gk = part[:, :GLA_QK] + part[:, GLA_QK:] + _dot(lr_lo, wgk_ref[:, :GLA_QK]) + bgk_ref[...]
    gk_ref[...] = gk.reshape(sb, rb, GLA_QK)


def _inproj(x, sc, sh, n1, wm, wlr, wgk, bgk, sb, rb):
    nseq, L, d = x.shape
    grid = (nseq // sb, L // rb)
    const = lambda s, t: (0, 0)
    return pl.pallas_call(
        _inproj_kernel,
        out_shape=(jax.ShapeDtypeStruct((nseq, L, PROJ_COLS), BF16),
                   jax.ShapeDtypeStruct((nseq, L, GLA_QK), F32)),
        grid=grid,
        in_specs=[
            pl.BlockSpec((sb, rb, d), lambda s, t: (s, t, 0)),
            pl.BlockSpec((sb, 1, d), lambda s, t: (s, 0, 0)),
            pl.BlockSpec((sb, 1, d), lambda s, t: (s, 0, 0)),
            pl.BlockSpec((1, d), const),
            pl.BlockSpec((d, PROJ_COLS), const),
            pl.BlockSpec((d, LANE), const),
            pl.BlockSpec((LANE, 2 * GLA_QK), const),
            pl.BlockSpec((1, GLA_QK), const),
        ],
        out_specs=(pl.BlockSpec((sb, rb, PROJ_COLS), lambda s, t: (s, t, 0)),
                   pl.BlockSpec((sb, rb, GLA_QK), lambda s, t: (s, t, 0))),
        compiler_params=_cparams("arbitrary", "arbitrary"),
        name="inproj",
    )(x, sc, sh, n1, wm, wlr, wgk, bgk)


def _head_masks():
    lane = lax.broadcasted_iota(jnp.int32, (1, GLA_QK), 1)
    return [(lane // GLA_DK) == h for h in range(GLA_HEADS)]


def _stack_heads(x, hm):
    return jnp.concatenate([jnp.where(m, x, 0.0) for m in hm], axis=0).astype(BF16)


def _chunk_pair_ids(c_len):
    ti = lax.broadcasted_iota(jnp.int32, (GLA_HEADS * c_len, c_len), 0) % c_len
    si = lax.broadcasted_iota(jnp.int32, (GLA_HEADS * c_len, c_len), 1)
    return ti, si


def _scores_factored(q4s, k, G, c_len):
    ti, si = _chunk_pair_ids(c_len)
    ke = (k * jnp.exp(-G)).astype(BF16)
    return jnp.concatenate(
        [jnp.where(si <= ti, _dot_nt(q4, ke[c * c_len:(c + 1) * c_len]), 0.0) for c, q4 in enumerate(q4s)], axis=0)


def _scores_bounded(qs, k, G, c_len, hm):
    ti, si = _chunk_pair_ids(c_len)
    t = lax.broadcasted_iota(jnp.int32, (c_len, 1), 0)
    col = lax.broadcasted_iota(jnp.int32, (c_len, c_len), 1)
    outs = []
    for c in range(qs.shape[0] // c_len):
        sl = slice(c * c_len, (c + 1) * c_len)
        q_c, k_c, g_c = qs[sl], k[sl], G[sl]
        a = jnp.where(si == ti, _dot_nt(_stack_heads(q_c, hm), k_c.astype(BF16)), 0.0)
        half = c_len // 2
        while half >= 1:
            blk = 2 * half
            sel = (col == (t // blk) * blk + (half - 1)).astype(F32)
            ref = _dot(sel, g_c, precision=HIGHEST)
            upper = (t % blk) >= half
            qh = jnp.where(upper, q_c * jnp.exp(jnp.minimum(g_c - ref, 0.0)), 0.0)
            kh = jnp.where(upper, 0.0, k_c * jnp.exp(jnp.minimum(ref - g_c, 0.0)))
            same = (ti // blk) == (si // blk)
            a = a + jnp.where(same, _dot_nt(_stack_heads(qh, hm), kh.astype(BF16)), 0.0)
            half //= 2
        outs.append(a)
    return jnp.concatenate(outs, axis=0)


def _cum_log_gates(gkpre, c_len):
    rows = gkpre.shape[0]
    g = jax.nn.log_sigmoid(gkpre) / GLA_GATE_NORM
    ri = lax.broadcasted_iota(jnp.int32, (rows, rows), 0)
    ci = lax.broadcasted_iota(jnp.int32, (rows, rows), 1)
    tri = jnp.where((ci <= ri) & ((ci // c_len) == (ri // c_len)), 1.0, 0.0).astype(BF16)
    g_hi = g.astype(BF16)
    r1 = g - g_hi.astype(F32)
    g_mid = r1.astype(BF16)
    g_lo = (r1 - g_mid.astype(F32)).astype(BF16)
    parts = _dot(tri, jnp.concatenate([g_hi, g_mid, g_lo], axis=1))
    return parts[:, :GLA_QK] + parts[:, GLA_QK:2 * GLA_QK] + parts[:, 2 * GLA_QK:]


def _factored_is_safe(G, c_len):
    ends = [G[(c + 1) * c_len - 1:(c + 1) * c_len] for c in range(G.shape[0] // c_len)]
    return jnp.min(jnp.concatenate(ends, axis=0)) > GATE_SAFE_MIN


def _mixer_block(p, G, states, c_len, gn, lng, lnb, wm_ref, bsb, bounded):
    rows = p.shape[0]
    n_chunks = rows // c_len
    chained = len(states) == 1
    hm = _head_masks()

    q = p[:, Q0:Q0 + GLA_QK].astype(F32)
    k = p[:, K0:K0 + GLA_QK].astype(F32)
    v = p[:, V0:V0 + GLA_WIDTH]
    r = p[:, R0:R0 + GLA_WIDTH].astype(F32)

    qs = q * (GLA_DK ** -0.5)
    qe = qs * jnp.exp(G)
    q4s = [_stack_heads(qe[c * c_len:(c + 1) * c_len], hm) for c in range(n_chunks)]
    scores = _scores_bounded(qs, k, G, c_len, hm) if bounded else _scores_factored(q4s, k, G, c_len)
    hc = GLA_HEADS * c_len

    new_states = []
    o_rows = []
    st = states[0]
    for c in range(n_chunks):
        lo, hi = c * c_len, (c + 1) * c_len
        if not chained:
            st = states[c]
        q4 = q4s[c]
        a = scores[c * hc:(c + 1) * hc].astype(BF16)
        o_inter = _dot_nt(q4, st.astype(BF16))
        v_c = v[lo:hi]
        heads = []
        for h in range(GLA_HEADS):
            o_h = o_inter[h * c_len:(h + 1) * c_len] + _dot(
                a[h * c_len:(h + 1) * c_len], v_c[:, h * GLA_DV:(h + 1) * GLA_DV])
            heads.append(o_h)
        o_rows.append(jnp.concatenate(heads, axis=1))
        g_last = G[hi - 1:hi]
        kd = (k[lo:hi] * jnp.exp(g_last - G[lo:hi])).astype(BF16)
        upd = _dot_tn(v_c, kd)
        st_new = jnp.exp(g_last) * st
        for h in range(GLA_HEADS):
            st_new = st_new + jnp.where(hm[h], upd[h * GLA_DV:(h + 1) * GLA_DV], 0.0)
        if chained:
            st = st_new
        else:
            new_states.append(st_new)
    if chained:
        new_states = [st]
    o = jnp.concatenate(o_rows, axis=0)

    gla = []
    for h in range(GLA_HEADS):
        o_h = o[:, h * GLA_DV:(h + 1) * GLA_DV]
        ms = jnp.mean(o_h * o_h, axis=-1, keepdims=True)
        r_h = r[:, h * GLA_DV:(h + 1) * GLA_DV]
        gla.append(o_h * lax.rsqrt(ms + EPS) * gn * (r_h * jax.nn.sigmoid(r_h)))

    u = _gelu(p[:, U0:U0 + GMLP_WIDTH].astype(F32))
    vv = _gelu(p[:, G0:G0 + GMLP_WIDTH].astype(F32))
    mu = jnp.mean(vv, axis=-1, keepdims=True)
    xc = vv - mu
    var = jnp.mean(xc * xc, axis=-1, keepdims=True)
    vn = xc * lax.rsqrt(var + EPS) * lng + lnb
    vnb = vn.astype(BF16)
    gm = []
    for gi in range(GMLP_GROUPS):
        sl = slice(gi * GMLP_GC, (gi + 1) * GMLP_GC)
        mixed = _dot(wm_ref[gi], vnb[:, sl]) + bsb[:, sl]
        gm.append(u[:, sl] * mixed)
    out = jnp.concatenate(gla + gm, axis=1)
    return out, new_states, vn


def _mixer_prompt_kernel(proj_ref, gk_ref, gn_ref, lng_ref, lnb_ref, wm_ref, bsb_ref, mix_ref, s_ref, st_scr):
    t = pl.program_id(1)
    nt = pl.num_programs(1)

    @pl.when(t == 0)
    def _():
        st_scr[...] = jnp.zeros_like(st_scr)

    n_sub = proj_ref.shape[1] // ROW_BLOCK
    Gs = [_cum_log_gates(gk_ref[0, j * ROW_BLOCK:(j + 1) * ROW_BLOCK, :], CHUNK) for j in range(n_sub)]
    safe = _factored_is_safe(jnp.concatenate(Gs, axis=0), CHUNK)

    def run(bounded):
        st = st_scr[...]
        for j in range(n_sub):
            rows = slice(j * ROW_BLOCK, (j + 1) * ROW_BLOCK)
            out, sts, _ = _mixer_block(proj_ref[0, rows, :], Gs[j], [st], CHUNK, gn_ref[...], lng_ref[...],
                                       lnb_ref[...], wm_ref, bsb_ref[...], bounded)
            st = sts[0]
            mix_ref[0, rows, :] = out.astype(BF16)
        st_scr[...] = st

    pl.when(safe)(functools.partial(run, False))
    pl.when(jnp.logical_not(safe))(functools.partial(run, True))

    @pl.when(t == nt - 1)
    def _():
        s_ref[0] = st_scr[...].T.reshape(GLA_HEADS, GLA_DK, GLA_DV)


def _mixer_prompt(proj, gk, gn, lng, lnb, wm, bsb, tb):
    b, L, _ = proj.shape
    const2 = lambda s, t: (0, 0)
    return pl.pallas_call(
        _mixer_prompt_kernel,
        out_shape=(jax.ShapeDtypeStruct((b, L, 2 * GLA_WIDTH), BF16),
                   jax.ShapeDtypeStruct((b, GLA_HEADS, GLA_DK, GLA_DV), F32)),
        grid=(b, L // tb),
        in_specs=[
            pl.BlockSpec((1, tb, PROJ_COLS), lambda s, t: (s, t, 0)),
            pl.BlockSpec((1, tb, GLA_QK), lambda s, t: (s, t, 0)),
            pl.BlockSpec((1, GLA_DV), const2),
            pl.BlockSpec((1, GMLP_WIDTH), const2),
            pl.BlockSpec((1, GMLP_WIDTH), const2),
            pl.BlockSpec((GMLP_GROUPS, ROW_BLOCK, ROW_BLOCK), lambda s, t: (0, 0, 0)),
            pl.BlockSpec((ROW_BLOCK, GMLP_WIDTH), const2),
        ],
        out_specs=(pl.BlockSpec((1, tb, 2 * GLA_WIDTH), lambda s, t: (s, t, 0)),
                   pl.BlockSpec((1, GLA_HEADS, GLA_DK, GLA_DV), lambda s, t: (s, 0, 0, 0))),
        scratch_shapes=[pltpu.VMEM((GLA_DV, GLA_QK), F32)],
        compiler_params=_cparams("arbitrary", "arbitrary"),
        name="mixer_prompt",
    )(proj, gk, gn, lng, lnb, wm, bsb)


def _mixer_sample_kernel(proj_ref, gk_ref, s0_ref, gn_ref, lng_ref, lnb_ref, wm_ref, bsb_ref,
                         mix_ref, s_ref, vn_ref):
    sb, rb, _ = proj_ref.shape
    G = _cum_log_gates(gk_ref[...].reshape(sb * rb, GLA_QK), rb)
    safe = _factored_is_safe(G, rb)

    def run(bounded):
        p = proj_ref[...].reshape(sb * rb, PROJ_COLS)
        states = [s0_ref[i].reshape(GLA_QK, GLA_DV).T for i in range(sb)]
        out, sts, vn = _mixer_block(p, G, states, rb, gn_ref[...], lng_ref[...], lnb_ref[...], wm_ref,
                                    bsb_ref[...], bounded)
        mix_ref[...] = out.astype(BF16).reshape(sb, rb, 2 * GLA_WIDTH)
        vn_ref[...] = vn.reshape(sb, rb, GMLP_WIDTH)
        for i in range(sb):
            s_ref[i] = sts[i].T.reshape(GLA_HEADS, GLA_DK, GLA_DV)

    pl.when(safe)(functools.partial(run, False))
    pl.when(jnp.logical_not(safe))(functools.partial(run, True))


def _mixer_sample(proj, gk, s0, gn, lng, lnb, wm, bsb):
    n, L, _ = proj.shape
    sb = ROW_BLOCK // L
    const2 = lambda s: (0, 0)
    return pl.pallas_call(
        _mixer_sample_kernel,
        out_shape=(jax.ShapeDtypeStruct((n, L, 2 * GLA_WIDTH), BF16),
                   jax.ShapeDtypeStruct((n, GLA_HEADS, GLA_DK, GLA_DV), F32),
                   jax.ShapeDtypeStruct((n, L, GMLP_WIDTH), F32)),
        grid=(n // sb,),
        in_specs=[
            pl.BlockSpec((sb, L, PROJ_COLS), lambda s: (s, 0, 0)),
            pl.BlockSpec((sb, L, GLA_QK), lambda s: (s, 0, 0)),
            pl.BlockSpec((sb, GLA_HEADS, GLA_DK, GLA_DV), lambda s: (s, 0, 0, 0)),
            pl.BlockSpec((1, GLA_DV), const2),
            pl.BlockSpec((1, GMLP_WIDTH), const2),
            pl.BlockSpec((1, GMLP_WIDTH), const2),
            pl.BlockSpec((GMLP_GROUPS, ROW_BLOCK, ROW_BLOCK), lambda s: (0, 0, 0)),
            pl.BlockSpec((ROW_BLOCK, GMLP_WIDTH), const2),
        ],
        out_specs=(pl.BlockSpec((sb, L, 2 * GLA_WIDTH), lambda s: (s, 0, 0)),
                   pl.BlockSpec((sb, GLA_HEADS, GLA_DK, GLA_DV), lambda s: (s, 0, 0, 0)),
                   pl.BlockSpec((sb, L, GMLP_WIDTH), lambda s: (s, 0, 0))),
        compiler_params=_cparams("arbitrary"),
        name="mixer_sample",
    )(proj, gk, s0, gn, lng, lnb, wm, bsb)


def _outproj_kernel(mix_ref, x_ref, g1_ref, sc_ref, sh_ref, n2_ref, wo_ref, wr_ref, br_ref, upper_ref, lower_ref,
                    xmid_ref, h2_ref, route_ref, slot_ref, n16_ref):
    sb, rb, d = x_ref.shape
    tm = sb * rb

    y = _dot(mix_ref[...].reshape(tm, d), wo_ref[...])
    xm = x_ref[...] + g1_ref[...] * y.reshape(sb, rb, d)
    xmid_ref[...] = xm
    ms = jnp.mean(xm * xm, axis=-1, keepdims=True)
    h2 = xm * lax.rsqrt(ms + EPS) * n2_ref[...]
    h2 = (h2 * (1.0 + sc_ref[...]) + sh_ref[...]).reshape(tm, d)
    h_hi = h2.astype(BF16)
    h2_ref[...] = h_hi

    h_lo = (h2 - h_hi.astype(F32)).astype(BF16)
    part = _dot(h_hi, wr_ref[...])
    logits = part[:, :LANE] + part[:, LANE:] + _dot(h_lo, wr_ref[:, :LANE]) + br_ref[...]
    l = logits.T[:N_EXPERTS]
    eid = lax.broadcasted_iota(jnp.int32, (N_EXPERTS, tm), 0).astype(F32)
    top_l, sel = [], []
    for _ in range(TOP_K):
        m = jnp.max(l, axis=0, keepdims=True)
        idx = jnp.min(jnp.where(l == m, eid, float(N_EXPERTS)), axis=0, keepdims=True)
        hit = eid == idx
        top_l.append(m)
        sel.append(hit)
        l = jnp.where(hit, -jnp.inf, l)
    ex = [jnp.exp(t - top_l[0]) for t in top_l]
    den = ex[0] + ex[1] + ex[2] + ex[3]
    top_w = [e / den for e in ex]

    chosen = (sel[0] | sel[1] | sel[2] | sel[3])
    cb = jnp.where(chosen, 1.0, 0.0)
    before = _dot(cb.astype(BF16), upper_ref[...])
    n = jnp.sum(cb, axis=1, keepdims=True)
    n16 = jnp.floor((n + (ROW_UNIT - 1)) * (1.0 / ROW_UNIT)) * ROW_UNIT
    n16b = jnp.broadcast_to(n16, (N_EXPERTS, LANE))
    ls = _dot(lower_ref[...], n16b, precision=HIGHEST)[:, 0:1]
    base = before + ls
    slots = [jnp.sum(jnp.where(s, base, 0.0), axis=0, keepdims=True) for s in sel]
    slot_ref[...] = jnp.concatenate(slots, axis=0).astype(jnp.int32)
    n16_ref[0] = n16b
    rows = jnp.concatenate(top_w + slots + [jnp.zeros((LANE - 2 * TOP_K, tm), F32)], axis=0)
    route_ref[...] = rows.T[:, :2 * TOP_K]


def _outproj(mix, x, g1, sc2, sh2, n2, wo, wr, br, upper, lower, sb, rb):
    nseq, L, d = x.shape
    tm = sb * rb
    nt = L // rb
    T = nseq * L
    grid = (nseq // sb, nt)
    const = lambda s, t: (0, 0)
    tok = lambda s, t: (0, s * nt + t)
    return pl.pallas_call(
        _outproj_kernel,
        out_shape=(jax.ShapeDtypeStruct((nseq, L, d), F32),
                   jax.ShapeDtypeStruct((T, d), BF16),
                   jax.ShapeDtypeStruct((T, 2 * TOP_K), F32),
                   jax.ShapeDtypeStruct((TOP_K, T), jnp.int32),
                   jax.ShapeDtypeStruct((T // tm, N_EXPERTS, LANE), F32)),
        grid=grid,
        in_specs=[
            pl.BlockSpec((sb, rb, d), lambda s, t: (s, t, 0)),
            pl.BlockSpec((sb, rb, d), lambda s, t: (s, t, 0)),
            pl.BlockSpec((sb, 1, d), lambda s, t: (s, 0, 0)),
            pl.BlockSpec((sb, 1, d), lambda s, t: (s, 0, 0)),
            pl.BlockSpec((sb, 1, d), lambda s, t: (s, 0, 0)),
            pl.BlockSpec((1, d), const),
            pl.BlockSpec((d, d), const),
            pl.BlockSpec((d, 2 * LANE), const),
            pl.BlockSpec((1, LANE), const),
            pl.BlockSpec((tm, tm), const),
            pl.BlockSpec((N_EXPERTS, N_EXPERTS), const),
        ],
        out_specs=(pl.BlockSpec((sb, rb, d), lambda s, t: (s, t, 0)),
                   pl.BlockSpec((tm, d), lambda s, t: (s * nt + t, 0)),
                   pl.BlockSpec((tm, 2 * TOP_K), lambda s, t: (s * nt + t, 0)),
                   pl.BlockSpec((TOP_K, tm), tok),
                   pl.BlockSpec((1, N_EXPERTS, LANE), lambda s, t: (s * nt + t, 0, 0))),
        compiler_params=_cparams("arbitrary", "arbitrary"),
        name="outproj",
    )(mix, x, g1, sc2, sh2, n2, wo, wr, br, upper, lower)


def _unit_copy(src_ref, src_unit, dst_ref, dst_unit, sem):
    return pltpu.make_async_copy(src_ref.at[src_unit], dst_ref.at[dst_unit], sem)


def _start_unit_copies(nun, copy_of):
    def issue_pair(j, carry):
        copy_of(2 * j).start()
        copy_of(2 * j + 1).start()
        return carry

    lax.fori_loop(0, nun // 2, issue_pair, 0)

    @pl.when(nun % 2 == 1)
    def _():
        copy_of(nun - 1).start()


def _wait_unit_copies(nun, copy_of):
    def drain(u, carry):
        copy_of(0).wait()
        return carry

    lax.fori_loop(0, nun, drain, 0)


def _dispatch_body(udst_ref, nun_ref, slot_ref, h2_ref, xs_ref, xloc, sem):
    i = pl.program_id(0)
    last = pl.num_programs(0) - 1
    tb = slot_ref.shape[1]
    nun = nun_ref[i]
    s = slot_ref[...]
    h2 = h2_ref[...]
    buf = i % 2
    units_per_tile = LOCAL_ROWS // ROW_UNIT

    def copies(tile, b):
        return lambda u: _unit_copy(xloc.at[b], u, xs_ref, udst_ref[tile * units_per_tile + u], sem.at[b])

    def sort_chunk(c):
        r = lax.broadcasted_iota(jnp.int32, (SORT_CHUNK, tb), 0) + c * SORT_CHUNK
        hit = (s[0:1] == r) | (s[1:2] == r) | (s[2:3] == r) | (s[3:4] == r)
        p = jnp.where(hit, 1.0, 0.0).astype(BF16)
        units = SORT_CHUNK // ROW_UNIT
        xloc[buf, c * units:(c + 1) * units] = _dot(p, h2).astype(BF16).reshape(units, ROW_UNIT, h2.shape[1])

    typical_rows = TOP_K * tb + N_EXPERTS * ROW_UNIT // 2
    for c in range(LOCAL_ROWS // SORT_CHUNK):
        if (c + 1) * SORT_CHUNK <= typical_rows:
            sort_chunk(c)
        else:
            pl.when(c * (SORT_CHUNK // ROW_UNIT) < nun)(functools.partial(sort_chunk, c))

    _start_unit_copies(nun, copies(i, buf))

    @pl.when(i > 0)
    def _():
        _wait_unit_copies(nun_ref[i - 1], copies(i - 1, 1 - buf))

    @pl.when(i == last)
    def _():
        _wait_unit_copies(nun, copies(i, buf))


def _dispatch_first_kernel(udst_ref, nun_ref, slot_ref, h2_ref, xs_ref, xloc, sem):
    _dispatch_body(udst_ref, nun_ref, slot_ref, h2_ref, xs_ref, xloc, sem)


def _dispatch_next_kernel(udst_ref, nun_ref, slot_ref, h2_ref, xs_in_ref, xs_ref, xloc, sem):
    del xs_in_ref
    _dispatch_body(udst_ref, nun_ref, slot_ref, h2_ref, xs_ref, xloc, sem)


def _dispatch(udst, nun, slot, h2, xs, n_rows, tb):
    T, d = h2.shape
    any_spec = pl.BlockSpec(memory_space=pl.ANY)
    in_specs = [pl.BlockSpec((TOP_K, tb), lambda i, *_: (0, i)),
                pl.BlockSpec((tb, d), lambda i, *_: (i, 0))]
    aliases = {}
    body = _dispatch_first_kernel
    args = (udst, nun, slot, h2)
    if xs is not None:
        in_specs.append(any_spec)
        aliases = {4: 0}
        body = _dispatch_next_kernel
        args = args + (xs,)
    return pl.pallas_call(
        body,
        out_shape=jax.ShapeDtypeStruct((n_rows // ROW_UNIT, ROW_UNIT, d), BF16),
        grid_spec=pltpu.PrefetchScalarGridSpec(
            num_scalar_prefetch=2, grid=(T // tb,), in_specs=in_specs, out_specs=any_spec,
            scratch_shapes=[pltpu.VMEM((2, LOCAL_ROWS // ROW_UNIT, ROW_UNIT, d), BF16),
                            pltpu.SemaphoreType.DMA((2,))]),
        input_output_aliases=aliases,
        compiler_params=pltpu.CompilerParams(dimension_semantics=("arbitrary",), has_side_effects=True,
                                             vmem_limit_bytes=VMEM_LIMIT),
        name="dispatch_next" if xs is not None else "dispatch_first",
    )(*args)


def _experts_kernel(te_ref, tv_ref, tf_ref, tn_ref, xs_ref, wg_hbm, bg_ref, wu_hbm, bu_ref, wd_hbm, bd_ref, y_ref,
                    stage_g, stage_u, stage_d, wgb, wub, wdb, sem):
    i = pl.program_id(0)
    valid = tv_ref[i]
    tm, d = xs_ref.shape
    ff = wgb.shape[1]

    def weight_copies(e):
        return [pltpu.make_async_copy(src.at[e], dst, sem.at[j])
                for j, (src, dst) in enumerate(((wg_hbm, stage_g), (wu_hbm, stage_u), (wd_hbm, stage_d)))]

    @pl.when(i == 0)
    def _():
        for cp in weight_copies(te_ref[0]):
            cp.start()

    @pl.when(tf_ref[i] == 1)
    def _():
        for cp in weight_copies(te_ref[i]):
            cp.wait()
        wgb[...] = stage_g[...].astype(BF16)
        wub[...] = stage_u[...].astype(BF16)
        wdb[...] = stage_d[...].astype(BF16)

        @pl.when(tn_ref[i] >= 0)
        def _():
            for cp in weight_copies(tn_ref[i]):
                cp.start()

    @pl.when(valid > 0)
    def _():
        row = lax.broadcasted_iota(jnp.int32, (tm, 1), 0)
        xb = jnp.where(row < valid, xs_ref[...], jnp.zeros((), BF16))
        acc = jnp.zeros((tm, d), F32)
        for c in range(ff // FF_CHUNK):
            cs = slice(c * FF_CHUNK, (c + 1) * FF_CHUNK)
            gate = jnp.minimum(_dot(xb, wgb[:, cs]) + bg_ref[0, :, cs], SWIGLU_LIMIT)
            up = jnp.clip(_dot(xb, wub[:, cs]) + bu_ref[0, :, cs], -SWIGLU_LIMIT, SWIGLU_LIMIT)
            act = (up + 1.0) * gate * jax.nn.sigmoid(SWIGLU_ALPHA * gate)
            acc = acc + _dot(act.astype(BF16), wdb[cs, :])
        y_ref[...] = (acc + bd_ref[0]).astype(BF16)

    @pl.when(valid == 0)
    def _():
        y_ref[...] = jnp.zeros_like(y_ref)


def _experts(tile_expert, tile_valid, tile_first, tile_next, xs, wg, bg, wu, bu, wd, bd, tmg):
    n_tiles = tile_expert.shape[0]
    _, d, ff = wg.shape
    bspec = lambda shp: pl.BlockSpec(shp, lambda i, te, *_: (te[i], 0, 0))
    any_spec = pl.BlockSpec(memory_space=pl.ANY)
    return pl.pallas_call(
        _experts_kernel,
        out_shape=jax.ShapeDtypeStruct(xs.shape, BF16),
        grid_spec=pltpu.PrefetchScalarGridSpec(
            num_scalar_prefetch=4,
            grid=(n_tiles,),
            in_specs=[
                pl.BlockSpec((tmg, d), lambda i, *_: (i, 0)),
                any_spec, bspec((1, 1, ff)),
                any_spec, bspec((1, 1, ff)),
                any_spec, bspec((1, 1, d)),
            ],
            out_specs=pl.BlockSpec((tmg, d), lambda i, *_: (i, 0)),
            scratch_shapes=[pltpu.VMEM((d, ff), F32), pltpu.VMEM((d, ff), F32), pltpu.VMEM((ff, d), F32),
                            pltpu.VMEM((d, ff), BF16), pltpu.VMEM((d, ff), BF16), pltpu.VMEM((ff, d), BF16),
                            pltpu.SemaphoreType.DMA((3,))],
        ),
        compiler_params=pltpu.CompilerParams(dimension_semantics=("arbitrary",), vmem_limit_bytes=EXPERTS_VMEM_LIMIT),
        name="experts",
    )(tile_expert, tile_valid, tile_first, tile_next, xs, wg, bg, wu, bu, wd, bd)


def _combine_kernel(udst_ref, nun_ref, y_ref, route_ref, xmid_ref, g2_ref, nf_ref, o_ref, ybuf, sem):
    sb, rb, d = xmid_ref.shape
    tb = sb * rb
    i = pl.program_id(0) * pl.num_programs(1) + pl.program_id(1)
    n_steps = pl.num_programs(0) * pl.num_programs(1)
    buf = i % 2
    units_per_tile = LOCAL_ROWS // ROW_UNIT

    def copies(tile, b):
        return lambda u: _unit_copy(y_ref, udst_ref[tile * units_per_tile + u], ybuf.at[b], u, sem.at[b])

    @pl.when(i == 0)
    def _():
        ybuf[...] = jnp.zeros_like(ybuf)
        _start_unit_copies(nun_ref[0], copies(0, 0))

    _wait_unit_copies(nun_ref[i], copies(i, buf))

    @pl.when(i + 1 < n_steps)
    def _():
        _start_unit_copies(nun_ref[i + 1], copies(i + 1, 1 - buf))

    chunk_units = PERM_CHUNK // ROW_UNIT

    def sorted_rows(c):
        return ybuf[buf, c * chunk_units:(c + 1) * chunk_units].reshape(PERM_CHUNK, d)

    route = route_ref[...]
    lane = lax.broadcasted_iota(jnp.int32, (tb, PERM_CHUNK), 1).astype(F32)
    wk = [jnp.broadcast_to(route[:, k:k + 1], (tb, PERM_CHUNK)) for k in range(TOP_K)]
    sk = [jnp.broadcast_to(route[:, TOP_K + k:TOP_K + k + 1], (tb, PERM_CHUNK)) for k in range(TOP_K)]

    def weights_chunk(c):
        r = lane + float(c * PERM_CHUNK)
        pw = jnp.zeros((tb, PERM_CHUNK), F32)
        for k in range(TOP_K):
            pw = jnp.where(sk[k] == r, wk[k], pw)
        return pw.astype(BF16)

    def contribution(c):
        return _dot(weights_chunk(c), sorted_rows(c))

    moe = contribution(0)
    for c in range(1, LOCAL_ROWS // PERM_CHUNK):
        moe = moe + contribution(c)
    out = xmid_ref[...] + g2_ref[...] * moe.reshape(sb, rb, d)
    ms = jnp.mean(out * out, axis=-1, keepdims=True)
    o_ref[...] = out * lax.rsqrt(ms + EPS) * nf_ref[...]


def _combine(udst, nun, y, route, xmid, g2, nf, sb, rb):
    nseq, L, d = xmid.shape
    tb = sb * rb
    nt = L // rb
    tok = lambda s, t, *_: (s * nt + t, 0)
    return pl.pallas_call(
        _combine_kernel,
        out_shape=jax.ShapeDtypeStruct((nseq, L, d), F32),
        grid_spec=pltpu.PrefetchScalarGridSpec(
            num_scalar_prefetch=2,
            grid=(nseq // sb, nt),
            in_specs=[
                pl.BlockSpec(memory_space=pl.ANY),
                pl.BlockSpec((tb, 2 * TOP_K), tok),
                pl.BlockSpec((sb, rb, d), lambda s, t, *_: (s, t, 0)),
                pl.BlockSpec((sb, 1, d), lambda s, t, *_: (s, 0, 0)),
                pl.BlockSpec((1, d), lambda s, t, *_: (0, 0)),
            ],
            out_specs=pl.BlockSpec((sb, rb, d), lambda s, t, *_: (s, t, 0)),
            scratch_shapes=[pltpu.VMEM((2, LOCAL_ROWS // ROW_UNIT, ROW_UNIT, d), BF16),
                            pltpu.SemaphoreType.DMA((2,))],
        ),
        compiler_params=_cparams("arbitrary", "arbitrary"),
        name="combine",
    )(udst, nun, y, route, xmid, g2, nf)


def _tile_rows(nseq, L, tile):
    if L >= tile:
        assert L % tile == 0
        return 1, tile
    assert tile % L == 0 and nseq % (tile // L) == 0
    return tile // L, L


def kernel(x_prompt, x_sample, state_gla, c_prompt, c_sample, w_ada, b_ada, norm1, w_in, w_gk, b_gk, gla_norm,
           gmlp_ln_g, gmlp_ln_b, gmlp_w_s, gmlp_b_s, w_out, norm2, w_router, b_router, w_gate, b_gate, w_up,
           b_up, w_down, b_down, norm_f):
    depth = w_ada.shape[0]
    assert depth == 1
    bp, lp, d = x_prompt.shape
    bs, ls, _ = x_sample.shape
    tp, ts = bp * lp, bs * ls

    nc = bp + bs
    ncp = -(-nc // SUBLANE) * SUBLANE
    c_all = jnp.concatenate([c_prompt, c_sample, jnp.zeros((ncp - nc, d), F32)], axis=0)
    mod = _ada(c_all, w_ada[0], b_ada[0][None]).reshape(ncp, N_MOD, 1, d)
    mods_p = [mod[:bp, i] for i in range(N_MOD)]
    mods_s = [mod[bp:nc, i] for i in range(N_MOD)]

    wi = w_in[0]
    c_lr = 2 * GLA_QK + GLA_WIDTH
    c_r = c_lr + GLA_LOWRANK
    wm = jnp.concatenate([wi[:, :c_lr], wi[:, c_r:]], axis=1).astype(BF16)
    wlr = jnp.pad(wi[:, c_lr:c_r], ((0, 0), (0, LANE - GLA_LOWRANK))).astype(BF16)
    wgk_f = jnp.pad(w_gk[0], ((0, LANE - GLA_LOWRANK), (0, 0)))
    wgk_hi = wgk_f.astype(BF16)
    wgk = jnp.concatenate([wgk_hi, (wgk_f - wgk_hi.astype(F32)).astype(BF16)], axis=1)
    bgk = b_gk[0][None]
    n1, n2, nf = norm1[0][None], norm2[0][None], norm_f[None]
    gn, lng, lnb = gla_norm[0][None], gmlp_ln_g[0][None], gmlp_ln_b[0][None]
    ws, bsv = gmlp_w_s[0], gmlp_b_s[0]
    pos_i = jnp.arange(GMLP_BLOCK)
    cmask = (pos_i[None, :] // CHUNK) <= (pos_i[:, None] // CHUNK)
    wm_p = jnp.where(cmask[None], ws, 0.0).astype(BF16)
    bsb_p = jnp.repeat(bsv.T, GMLP_GC, axis=1)
    reps = ROW_BLOCK // ls
    eye = jnp.eye(reps, dtype=F32)
    wm_s = jnp.einsum("ab,gij->gaibj", eye, ws[:, :ls, :ls]).reshape(GMLP_GROUPS, ROW_BLOCK, ROW_BLOCK).astype(BF16)
    bsb_s = jnp.tile(jnp.repeat(bsv[:, :ls].T, GMLP_GC, axis=1), (reps, 1))
    wo = w_out[0].astype(BF16)
    wr_f = jnp.pad(w_router[0], ((0, 0), (0, LANE - N_EXPERTS)))
    wr_hi = wr_f.astype(BF16)
    wr = jnp.concatenate([wr_hi, (wr_f - wr_hi.astype(F32)).astype(BF16)], axis=1)
    br = jnp.concatenate([b_router[0], jnp.full((LANE - N_EXPERTS,), -1e30, F32)])[None]
    upper = (jnp.arange(TOKEN_TILE)[:, None] < jnp.arange(TOKEN_TILE)[None, :]).astype(BF16)
    lower = (jnp.arange(N_EXPERTS)[None, :] < jnp.arange(N_EXPERTS)[:, None]).astype(F32)
    wg, wu, wd = w_gate[0], w_up[0], w_down[0]
    bg, bu, bd = b_gate[0][:, None], b_up[0][:, None], b_down[0][:, None]

    sbp, rbp = _tile_rows(bp, lp, TOKEN_TILE)
    sbs, rbs = _tile_rows(bs, ls, TOKEN_TILE)

    proj_p, gk_p = _inproj(x_prompt, mods_p[1], mods_p[0], n1, wm, wlr, wgk, bgk, sbp, rbp)
    proj_s, gk_s = _inproj(x_sample, mods_s[1], mods_s[0], n1, wm, wlr, wgk, bgk, sbs, rbs)
    mix_p, state_p = _mixer_prompt(proj_p, gk_p, gn, lng, lnb, wm_p, bsb_p, TOKEN_TILE)
    mix_s, state_s, vn_s = _mixer_sample(proj_s, gk_s, state_gla[0], gn, lng, lnb, wm_s, bsb_s)

    xmid_p, h2_p, route_p, slot_p, n16_p = _outproj(
        mix_p, x_prompt, mods_p[2], mods_p[4], mods_p[3], n2, wo, wr, br, upper, lower, sbp, rbp)
    xmid_s, h2_s, route_s, slot_s, n16_s = _outproj(
        mix_s, x_sample, mods_s[2], mods_s[4], mods_s[3], n2, wo, wr, br, upper, lower, sbs, rbs)

    tmg = EXPERT_TILE
    ntp = tp // TOKEN_TILE
    eids = jnp.arange(N_EXPERTS, dtype=jnp.int32)
    n16 = jnp.concatenate([n16_p[:, :, 0], n16_s[:, :, 0]], axis=0).astype(jnp.int32)
    nt_all = n16.shape[0]
    earlier = jnp.cumsum(n16, axis=0) - n16
    tot = jnp.sum(n16, axis=0)
    tiles_e = (tot + tmg - 1) // tmg
    tile_end = jnp.cumsum(tiles_e)
    tile_start = tile_end - tiles_e
    row_start = tile_start * tmg
    n_tiles = (TOP_K * (tp + ts) + nt_all * N_EXPERTS * (ROW_UNIT - 1)) // tmg + N_EXPERTS
    tid = jnp.arange(n_tiles, dtype=jnp.int32)
    te = jnp.minimum(jnp.sum((tid[:, None] >= tile_end[None, :]).astype(jnp.int32), axis=1), N_EXPERTS - 1)
    te_hot = te[:, None] == eids[None, :]
    tot_te = jnp.sum(jnp.where(te_hot, tot[None, :], 0), axis=1)
    start_te = jnp.sum(jnp.where(te_hot, tile_start[None, :], 0), axis=1)
    active = tid < tile_end[-1]
    tv = jnp.where(active, jnp.clip(tot_te - (tid - start_te) * tmg, 0, tmg), 0).astype(jnp.int32)
    last_e = jnp.max(jnp.where(tiles_e > 0, eids, 0)).astype(jnp.int32)
    te = jnp.where(active, te, last_e).astype(jnp.int32)
    te_prev = jnp.concatenate([jnp.full((1,), -1, jnp.int32), te[:-1]])
    tf = (active & (te != te_prev)).astype(jnp.int32)
    later_used = (eids[None, :] > te[:, None]) & (tiles_e[None, :] > 0)
    tn = jnp.min(jnp.where(later_used, eids[None, :], N_EXPERTS), axis=1)
    tn = jnp.where(tn < N_EXPERTS, tn, -1).astype(jnp.int32)

    run_end = jnp.cumsum(n16, axis=1)
    run_start = run_end - n16
    unit_row = jnp.arange(LOCAL_ROWS // ROW_UNIT, dtype=jnp.int32) * ROW_UNIT
    unit_e = jnp.sum((unit_row[None, :, None] >= run_end[:, None, :]).astype(jnp.int32), axis=2)
    unit_hot = unit_e[:, :, None] == eids[None, None, :]
    run_off = row_start[None, :] + earlier - run_start
    udst = jnp.sum(jnp.where(unit_hot, run_off[:, None, :], 0), axis=2) + unit_row[None, :]
    udst = (jnp.where(unit_row[None, :] < run_end[:, -1:], udst, 0) // ROW_UNIT).astype(jnp.int32)
    nun = (run_end[:, -1] // ROW_UNIT).astype(jnp.int32)

    n_rows = n_tiles * tmg
    udst_p, udst_s = udst[:ntp].reshape(-1), udst[ntp:].reshape(-1)
    xs = _dispatch(udst_p, nun[:ntp], slot_p, h2_p, None, n_rows, TOKEN_TILE)
    xs = _dispatch(udst_s, nun[ntp:], slot_s, h2_s, xs, n_rows, TOKEN_TILE)
    y = _experts(te, tv, tf, tn, xs.reshape(n_rows, d), wg, bg, wu, bu, wd, bd, tmg)
    y = y.reshape(n_rows // ROW_UNIT, ROW_UNIT, d)
    y_prompt = _combine(udst_p, nun[:ntp], y, route_p, xmid_p, mods_p[5], nf, sbp, rbp)
    y_sample = _combine(udst_s, nun[ntp:], y, route_s, xmid_s, mods_s[5], nf, sbs, rbs)

    return (y_prompt, y_sample, state_p[None], state_s[None], vn_s[None])
```

```python
import functools

import jax
import jax.numpy as jnp
from jax import lax
from jax.experimental import pallas as pl
from jax.experimental.pallas import tpu as pltpu

F32 = jnp.float32
BF16 = jnp.bfloat16
HIGHEST = lax.Precision.HIGHEST

CHUNK = 64
GLA_HEADS = 4
GLA_DK = 64
GLA_DV = 128
GLA_QK = GLA_HEADS * GLA_DK
GLA_WIDTH = GLA_HEADS * GLA_DV
GLA_LOWRANK = 16
GLA_GATE_NORM = 16.0
GATE_SAFE_MIN = -60.0
GMLP_WIDTH = 512
GMLP_GROUPS = 4
GMLP_GC = GMLP_WIDTH // GMLP_GROUPS
GMLP_BLOCK = 128
N_EXPERTS = 32
TOP_K = 4
SWIGLU_LIMIT = 7.0
SWIGLU_ALPHA = 1.702
EPS = 1e-6
N_MOD = 6

LANE = 128
SUBLANE = 8

Q0 = 0
K0 = Q0 + GLA_QK
V0 = K0 + GLA_QK
R0 = V0 + GLA_WIDTH
U0 = R0 + GLA_WIDTH
G0 = U0 + GMLP_WIDTH
PROJ_COLS = G0 + GMLP_WIDTH

ROW_BLOCK = 128
TOKEN_TILE = 512
EXPERT_TILE = 512
FF_CHUNK = 512
ROW_UNIT = 16
LOCAL_ROWS = 2560
SORT_CHUNK = 128
PERM_CHUNK = 256
VMEM_LIMIT = 48 * 1024 * 1024
EXPERTS_VMEM_LIMIT = 56 * 1024 * 1024


def _cparams(*sem):
    return pltpu.CompilerParams(dimension_semantics=sem, vmem_limit_bytes=VMEM_LIMIT)


def _dot(a, b, **kw):
    return jnp.dot(a, b, preferred_element_type=F32, **kw)


def _dot_nt(a, b):
    return lax.dot_general(a, b, (((1,), (1,)), ((), ())), preferred_element_type=F32)


def _gelu(x):
    return 0.5 * x * (1.0 + lax.erf(x * (0.5 ** 0.5)))


def _dot_tn(a, b):
    return lax.dot_general(a, b, (((0,), (0,)), ((), ())), preferred_element_type=F32)


def _ada_kernel(c_ref, w_ref, b_ref, o_ref):
    c = c_ref[...]
    s = c * jax.nn.sigmoid(c)
    o_ref[...] = _dot(s, w_ref[...], precision=HIGHEST) + b_ref[...]


def _ada(c, w_ada, b_ada):
    n, d = c.shape
    return pl.pallas_call(
        _ada_kernel,
        out_shape=jax.ShapeDtypeStruct((n, N_MOD * d), F32),
        grid=(N_MOD,),
        in_specs=[
            pl.BlockSpec((n, d), lambda j: (0, 0)),
            pl.BlockSpec((d, d), lambda j: (0, j)),
            pl.BlockSpec((1, d), lambda j: (0, j)),
        ],
        out_specs=pl.BlockSpec((n, d), lambda j: (0, j)),
        compiler_params=_cparams("arbitrary"),
        name="ada",
    )(c, w_ada, b_ada)


PROJ_COL_CHUNK = 512


def _modulated_norm(x_ref, sc_ref, sh_ref, n1_ref):
    sb, rb, d = x_ref.shape
    x = x_ref[...]
    ms = jnp.mean(x * x, axis=-1, keepdims=True)
    h = x * lax.rsqrt(ms + EPS) * n1_ref[...]
    h = h * (1.0 + sc_ref[...]) + sh_ref[...]
    return h.reshape(sb * rb, d).astype(BF16)


def _gate_preact(hb, wlr_ref, wgk_ref, bgk_ref):
    lr = _dot(hb, wlr_ref[...])
    lr_hi = lr.astype(BF16)
    lr_lo = (lr - lr_hi.astype(F32)).astype(BF16)
    part = _dot(lr_hi, wgk_ref[...])
    return part[:, :GLA_QK] + part[:, GLA_QK:] + _dot(lr_lo, wgk_ref[:, :GLA_QK]) + bgk_ref[...]


def _inproj_kernel(x_ref, sc_ref, sh_ref, n1_ref, wm_ref, wlr_ref, wgk_ref, bgk_ref, proj_ref, gk_ref):
    sb, rb, _ = x_ref.shape
    hb = _modulated_norm(x_ref, sc_ref, sh_ref, n1_ref)
    cw = PROJ_COL_CHUNK
    for c in range(PROJ_COLS // cw):
        p = _dot(hb, wm_ref[:, c * cw:(c + 1) * cw])
        proj_ref[:, :, c * cw:(c + 1) * cw] = p.astype(BF16).reshape(sb, rb, cw)
    gk_ref[...] = _gate_preact(hb, wlr_ref, wgk_ref, bgk_ref).reshape(sb, rb, GLA_QK)


def _inproj(x, sc, sh, n1, wm, wlr, wgk, bgk, sb, rb):
    nseq, L, d = x.shape
    grid = (nseq // sb, L // rb)
    const = lambda s, t: (0, 0)
    return pl.pallas_call(
        _inproj_kernel,
        out_shape=(jax.ShapeDtypeStruct((nseq, L, PROJ_COLS), BF16),
                   jax.ShapeDtypeStruct((nseq, L, GLA_QK), F32)),
        grid=grid,
        in_specs=[
            pl.BlockSpec((sb, rb, d), lambda s, t: (s, t, 0)),
            pl.BlockSpec((sb, 1, d), lambda s, t: (s, 0, 0)),
            pl.BlockSpec((sb, 1, d), lambda s, t: (s, 0, 0)),
            pl.BlockSpec((1, d), const),
            pl.BlockSpec((d, PROJ_COLS), const),
            pl.BlockSpec((d, LANE), const),
            pl.BlockSpec((LANE, 2 * GLA_QK), const),
            pl.BlockSpec((1, GLA_QK), const),
        ],
        out_specs=(pl.BlockSpec((sb, rb, PROJ_COLS), lambda s, t: (s, t, 0)),
                   pl.BlockSpec((sb, rb, GLA_QK), lambda s, t: (s, t, 0))),
        compiler_params=_cparams("arbitrary", "arbitrary"),
        name="inproj",
    )(x, sc, sh, n1, wm, wlr, wgk, bgk)


def _head_masks():
    lane = lax.broadcasted_iota(jnp.int32, (1, GLA_QK), 1)
    return [(lane // GLA_DK) == h for h in range(GLA_HEADS)]


def _stack_heads(x, hm):
    return jnp.concatenate([jnp.where(m, x, 0.0) for m in hm], axis=0).astype(BF16)


def _chunk_pair_ids(c_len):
    ti = lax.broadcasted_iota(jnp.int32, (GLA_HEADS * c_len, c_len), 0) % c_len
    si = lax.broadcasted_iota(jnp.int32, (GLA_HEADS * c_len, c_len), 1)
    return ti, si


def _scores_factored(q4s, k, G, c_len):
    ti, si = _chunk_pair_ids(c_len)
    ke = (k * jnp.exp(-G)).astype(BF16)
    return jnp.concatenate(
        [jnp.where(si <= ti, _dot_nt(q4, ke[c * c_len:(c + 1) * c_len]), 0.0) for c, q4 in enumerate(q4s)], axis=0)


def _scores_bounded(qs, k, G, c_len, hm):
    ti, si = _chunk_pair_ids(c_len)
    t = lax.broadcasted_iota(jnp.int32, (c_len, 1), 0)
    col = lax.broadcasted_iota(jnp.int32, (c_len, c_len), 1)
    outs = []
    for c in range(qs.shape[0] // c_len):
        sl = slice(c * c_len, (c + 1) * c_len)
        q_c, k_c, g_c = qs[sl], k[sl], G[sl]
        a = jnp.where(si == ti, _dot_nt(_stack_heads(q_c, hm), k_c.astype(BF16)), 0.0)
        half = c_len // 2
        while half >= 1:
            blk = 2 * half
            sel = (col == (t // blk) * blk + (half - 1)).astype(F32)
            ref = _dot(sel, g_c, precision=HIGHEST)
            upper = (t % blk) >= half
            qh = jnp.where(upper, q_c * jnp.exp(jnp.minimum(g_c - ref, 0.0)), 0.0)
            kh = jnp.where(upper, 0.0, k_c * jnp.exp(jnp.minimum(ref - g_c, 0.0)))
            same = (ti // blk) == (si // blk)
            a = a + jnp.where(same, _dot_nt(_stack_heads(qh, hm), kh.astype(BF16)), 0.0)
            half //= 2
        outs.append(a)
    return jnp.concatenate(outs, axis=0)


def _cum_log_gates(gkpre, c_len):
    rows = gkpre.shape[0]
    g = jax.nn.log_sigmoid(gkpre) / GLA_GATE_NORM
    ri = lax.broadcasted_iota(jnp.int32, (rows, rows), 0)
    ci = lax.broadcasted_iota(jnp.int32, (rows, rows), 1)
    tri = jnp.where((ci <= ri) & ((ci // c_len) == (ri // c_len)), 1.0, 0.0).astype(BF16)
    g_hi = g.astype(BF16)
    r1 = g - g_hi.astype(F32)
    g_mid = r1.astype(BF16)
    g_lo = (r1 - g_mid.astype(F32)).astype(BF16)
    parts = _dot(tri, jnp.concatenate([g_hi, g_mid, g_lo], axis=1))
    return parts[:, :GLA_QK] + parts[:, GLA_QK:2 * GLA_QK] + parts[:, 2 * GLA_QK:]


def _factored_is_safe(G, c_len):
    ends = [G[(c + 1) * c_len - 1:(c + 1) * c_len] for c in range(G.shape[0] // c_len)]
    return jnp.min(jnp.concatenate(ends, axis=0)) > GATE_SAFE_MIN


def _mixer_block(p, G, states, c_len, gn, lng, lnb, wm_ref, bsb, bounded, side=()):
    rows = p.shape[0]
    n_chunks = rows // c_len
    chained = len(states) == 1
    hm = _head_masks()
    side = list(side)

    def side_step():
        if side:
            side.pop(0)()

    q = p[:, Q0:Q0 + GLA_QK].astype(F32)
    k = p[:, K0:K0 + GLA_QK].astype(F32)
    v = p[:, V0:V0 + GLA_WIDTH]
    r = p[:, R0:R0 + GLA_WIDTH].astype(F32)

    qs = q * (GLA_DK ** -0.5)
    qe = qs * jnp.exp(G)
    q4s = [_stack_heads(qe[c * c_len:(c + 1) * c_len], hm) for c in range(n_chunks)]
    scores = _scores_bounded(qs, k, G, c_len, hm) if bounded else _scores_factored(q4s, k, G, c_len)
    hc = GLA_HEADS * c_len

    new_states = []
    o_rows = []
    st = states[0]
    for c in range(n_chunks):
        lo, hi = c * c_len, (c + 1) * c_len
        if not chained:
            st = states[c]
        q4 = q4s[c]
        a = scores[c * hc:(c + 1) * hc].astype(BF16)
        o_inter = _dot_nt(q4, st.astype(BF16))
        v_c = v[lo:hi]
        heads = []
        for h in range(GLA_HEADS):
            o_h = o_inter[h * c_len:(h + 1) * c_len] + _dot(
                a[h * c_len:(h + 1) * c_len], v_c[:, h * GLA_DV:(h + 1) * GLA_DV])
            heads.append(o_h)
        o_rows.append(jnp.concatenate(heads, axis=1))
        g_last = G[hi - 1:hi]
        kd = (k[lo:hi] * jnp.exp(g_last - G[lo:hi])).astype(BF16)
        upd = _dot_tn(v_c, kd)
        st_new = jnp.exp(g_last) * st
        for h in range(GLA_HEADS):
            st_new = st_new + jnp.where(hm[h], upd[h * GLA_DV:(h + 1) * GLA_DV], 0.0)
        if chained:
            st = st_new
        else:
            new_states.append(st_new)
        side_step()
    if chained:
        new_states = [st]
    o = jnp.concatenate(o_rows, axis=0)

    gla = []
    for h in range(GLA_HEADS):
        o_h = o[:, h * GLA_DV:(h + 1) * GLA_DV]
        ms = jnp.mean(o_h * o_h, axis=-1, keepdims=True)
        r_h = r[:, h * GLA_DV:(h + 1) * GLA_DV]
        gla.append(o_h * lax.rsqrt(ms + EPS) * gn * (r_h * jax.nn.sigmoid(r_h)))
    side_step()

    u = _gelu(p[:, U0:U0 + GMLP_WIDTH].astype(F32))
    side_step()
    vv = _gelu(p[:, G0:G0 + GMLP_WIDTH].astype(F32))
    side_step()
    mu = jnp.mean(vv, axis=-1, keepdims=True)
    xc = vv - mu
    var = jnp.mean(xc * xc, axis=-1, keepdims=True)
    vn = xc * lax.rsqrt(var + EPS) * lng + lnb
    vnb = vn.astype(BF16)
    gm = []
    for gi in range(GMLP_GROUPS):
        sl = slice(gi * GMLP_GC, (gi + 1) * GMLP_GC)
        mixed = _dot(wm_ref[gi], vnb[:, sl]) + bsb[:, sl]
        gm.append(u[:, sl] * mixed)
    out = jnp.concatenate(gla + gm, axis=1)
    while side:
        side_step()
    return out, new_states, vn


def _mixer_prompt_kernel(x_ref, sc_ref, sh_ref, n1_ref, wm_ref, wlr_ref, wgk_ref, bgk_ref,
                         gn_ref, lng_ref, lnb_ref, ws_ref, bsb_ref, mix_ref, s_ref, st_scr):
    t = pl.program_id(1)
    nt = pl.num_programs(1)

    @pl.when(t == 0)
    def _():
        st_scr[...] = jnp.zeros_like(st_scr)

    tb = x_ref.shape[1]
    n_sub = tb // ROW_BLOCK
    cw = PROJ_COL_CHUNK
    hb = _modulated_norm(x_ref, sc_ref, sh_ref, n1_ref)

    def proj_steps(j, parts):
        hb_j = hb[j * ROW_BLOCK:(j + 1) * ROW_BLOCK]
        return [functools.partial(
            lambda c: parts.append(_dot(hb_j, wm_ref[:, c * cw:(c + 1) * cw]).astype(BF16)), c)
            for c in range(PROJ_COLS // cw)]

    gk = _gate_preact(hb, wlr_ref, wgk_ref, bgk_ref)
    first = []
    for step in proj_steps(0, first):
        step()
    Gs = [_cum_log_gates(gk[j * ROW_BLOCK:(j + 1) * ROW_BLOCK], CHUNK) for j in range(n_sub)]
    safe = _factored_is_safe(jnp.concatenate(Gs, axis=0), CHUNK)

    def run(bounded):
        st = st_scr[...]
        parts = first
        for j in range(n_sub):
            p_j = jnp.concatenate(parts, axis=1)
            parts = []
            side = proj_steps(j + 1, parts) if j + 1 < n_sub else []
            out, sts, _ = _mixer_block(p_j, Gs[j], [st], CHUNK, gn_ref[...], lng_ref[...], lnb_ref[...], ws_ref,
                                       bsb_ref[...], bounded, side)
            st = sts[0]
            mix_ref[0, j * ROW_BLOCK:(j + 1) * ROW_BLOCK, :] = out.astype(BF16)
        st_scr[...] = st

    pl.when(safe)(functools.partial(run, False))
    pl.when(jnp.logical_not(safe))(functools.partial(run, True))

    @pl.when(t == nt - 1)
    def _():
        s_ref[0] = st_scr[...].T.reshape(GLA_HEADS, GLA_DK, GLA_DV)


def _mixer_prompt(x, sc, sh, n1, wm, wlr, wgk, bgk, gn, lng, lnb, ws, bsb, tb):
    b, L, d = x.shape
    const2 = lambda s, t: (0, 0)
    return pl.pallas_call(
        _mixer_prompt_kernel,
        out_shape=(jax.ShapeDtypeStruct((b, L, 2 * GLA_WIDTH), BF16),
                   jax.ShapeDtypeStruct((b, GLA_HEADS, GLA_DK, GLA_DV), F32)),
        grid=(b, L // tb),
        in_specs=[
            pl.BlockSpec((1, tb, d), lambda s, t: (s, t, 0)),
            pl.BlockSpec((1, 1, d), lambda s, t: (s, 0, 0)),
            pl.BlockSpec((1, 1, d), lambda s, t: (s, 0, 0)),
            pl.BlockSpec((1, d), const2),
            pl.BlockSpec((d, PROJ_COLS), const2),
            pl.BlockSpec((d, LANE), const2),
            pl.BlockSpec((LANE, 2 * GLA_QK), const2),
            pl.BlockSpec((1, GLA_QK), const2),
            pl.BlockSpec((1, GLA_DV), const2),
            pl.BlockSpec((1, GMLP_WIDTH), const2),
            pl.BlockSpec((1, GMLP_WIDTH), const2),
            pl.BlockSpec((GMLP_GROUPS, ROW_BLOCK, ROW_BLOCK), lambda s, t: (0, 0, 0)),
            pl.BlockSpec((ROW_BLOCK, GMLP_WIDTH), const2),
        ],
        out_specs=(pl.BlockSpec((1, tb, 2 * GLA_WIDTH), lambda s, t: (s, t, 0)),
                   pl.BlockSpec((1, GLA_HEADS, GLA_DK, GLA_DV), lambda s, t: (s, 0, 0, 0))),
        scratch_shapes=[pltpu.VMEM((GLA_DV, GLA_QK), F32)],
        compiler_params=_cparams("arbitrary", "arbitrary"),
        name="mixer_prompt",
    )(x, sc, sh, n1, wm, wlr, wgk, bgk, gn, lng, lnb, ws, bsb)


def _mixer_sample_kernel(proj_ref, gk_ref, s0_ref, gn_ref, lng_ref, lnb_ref, wm_ref, bsb_ref,
                         mix_ref, s_ref, vn_ref):
    sb, rb, _ = proj_ref.shape
    G = _cum_log_gates(gk_ref[...].reshape(sb * rb, GLA_QK), rb)
    safe = _factored_is_safe(G, rb)

    def run(bounded):
        p = proj_ref[...].reshape(sb * rb, PROJ_COLS)
        states = [s0_ref[i].reshape(GLA_QK, GLA_DV).T for i in range(sb)]
        out, sts, vn = _mixer_block(p, G, states, rb, gn_ref[...], lng_ref[...], lnb_ref[...], wm_ref,
                                    bsb_ref[...], bounded)
        mix_ref[...] = out.astype(BF16).reshape(sb, rb, 2 * GLA_WIDTH)
        vn_ref[...] = vn.reshape(sb, rb, GMLP_WIDTH)
        for i in range(sb):
            s_ref[i] = sts[i].T.reshape(GLA_HEADS, GLA_DK, GLA_DV)

    pl.when(safe)(functools.partial(run, False))
    pl.when(jnp.logical_not(safe))(functools.partial(run, True))


def _mixer_sample(proj, gk, s0, gn, lng, lnb, wm, bsb):
    n, L, _ = proj.shape
    sb = ROW_BLOCK // L
    const2 = lambda s: (0, 0)
    return pl.pallas_call(
        _mixer_sample_kernel,
        out_shape=(jax.ShapeDtypeStruct((n, L, 2 * GLA_WIDTH), BF16),
                   jax.ShapeDtypeStruct((n, GLA_HEADS, GLA_DK, GLA_DV), F32),
                   jax.ShapeDtypeStruct((n, L, GMLP_WIDTH), F32)),
        grid=(n // sb,),
        in_specs=[
            pl.BlockSpec((sb, L, PROJ_COLS), lambda s: (s, 0, 0)),
            pl.BlockSpec((sb, L, GLA_QK), lambda s: (s, 0, 0)),
            pl.BlockSpec((sb, GLA_HEADS, GLA_DK, GLA_DV), lambda s: (s, 0, 0, 0)),
            pl.BlockSpec((1, GLA_DV), const2),
            pl.BlockSpec((1, GMLP_WIDTH), const2),
            pl.BlockSpec((1, GMLP_WIDTH), const2),
            pl.BlockSpec((GMLP_GROUPS, ROW_BLOCK, ROW_BLOCK), lambda s: (0, 0, 0)),
            pl.BlockSpec((ROW_BLOCK, GMLP_WIDTH), const2),
        ],
        out_specs=(pl.BlockSpec((sb, L, 2 * GLA_WIDTH), lambda s: (s, 0, 0)),
                   pl.BlockSpec((sb, GLA_HEADS, GLA_DK, GLA_DV), lambda s: (s, 0, 0, 0)),
                   pl.BlockSpec((sb, L, GMLP_WIDTH), lambda s: (s, 0, 0))),
        compiler_params=_cparams("arbitrary"),
        name="mixer_sample",
    )(proj, gk, s0, gn, lng, lnb, wm, bsb)


def _outproj_kernel(mix_ref, x_ref, g1_ref, sc_ref, sh_ref, n2_ref, wo_ref, wr_ref, br_ref, upper_ref, lower_ref,
                    xmid_ref, h2_ref, route_ref, slot_ref, n16_ref):
    sb, rb, d = x_ref.shape
    tm = sb * rb

    y = _dot(mix_ref[...].reshape(tm, d), wo_ref[...])
    xm = x_ref[...] + g1_ref[...] * y.reshape(sb, rb, d)
    xmid_ref[...] = xm
    ms = jnp.mean(xm * xm, axis=-1, keepdims=True)
    h2 = xm * lax.rsqrt(ms + EPS) * n2_ref[...]
    h2 = (h2 * (1.0 + sc_ref[...]) + sh_ref[...]).reshape(tm, d)
    h_hi = h2.astype(BF16)
    h2_ref[...] = h_hi

    h_lo = (h2 - h_hi.astype(F32)).astype(BF16)
    part = _dot(h_hi, wr_ref[...])
    logits = part[:, :LANE] + part[:, LANE:] + _dot(h_lo, wr_ref[:, :LANE]) + br_ref[...]
    l = logits.T[:N_EXPERTS]
    eid = lax.broadcasted_iota(jnp.int32, (N_EXPERTS, tm), 0).astype(F32)
    top_l, sel = [], []
    for _ in range(TOP_K):
        m = jnp.max(l, axis=0, keepdims=True)
        idx = jnp.min(jnp.where(l == m, eid, float(N_EXPERTS)), axis=0, keepdims=True)
        hit = eid == idx
        top_l.append(m)
        sel.append(hit)
        l = jnp.where(hit, -jnp.inf, l)
    ex = [jnp.exp(t - top_l[0]) for t in top_l]
    den = ex[0] + ex[1] + ex[2] + ex[3]
    top_w = [e / den for e in ex]

    chosen = (sel[0] | sel[1] | sel[2] | sel[3])
    cb = jnp.where(chosen, 1.0, 0.0)
    before = _dot(cb.astype(BF16), upper_ref[...])
    n = jnp.sum(cb, axis=1, keepdims=True)
    n16 = jnp.floor((n + (ROW_UNIT - 1)) * (1.0 / ROW_UNIT)) * ROW_UNIT
    n16b = jnp.broadcast_to(n16, (N_EXPERTS, LANE))
    ls = _dot(lower_ref[...], n16b, precision=HIGHEST)[:, 0:1]
    base = before + ls
    slots = [jnp.sum(jnp.where(s, base, 0.0), axis=0, keepdims=True) for s in sel]
    slot_ref[...] = jnp.concatenate(slots, axis=0).astype(jnp.int32)
    n16_ref[0] = n16b
    rows = jnp.concatenate(top_w + slots + [jnp.zeros((LANE - 2 * TOP_K, tm), F32)], axis=0)
    route_ref[...] = rows.T[:, :2 * TOP_K]


def _outproj(mix, x, g1, sc2, sh2, n2, wo, wr, br, upper, lower, sb, rb):
    nseq, L, d = x.shape
    tm = sb * rb
    nt = L // rb
    T = nseq * L
    grid = (nseq // sb, nt)
    const = lambda s, t: (0, 0)
    tok = lambda s, t: (0, s * nt + t)
    return pl.pallas_call(
        _outproj_kernel,
        out_shape=(jax.ShapeDtypeStruct((nseq, L, d), F32),
                   jax.ShapeDtypeStruct((T, d), BF16),
                   jax.ShapeDtypeStruct((T, 2 * TOP_K), F32),
                   jax.ShapeDtypeStruct((TOP_K, T), jnp.int32),
                   jax.ShapeDtypeStruct((T // tm, N_EXPERTS, LANE), F32)),
        grid=grid,
        in_specs=[
            pl.BlockSpec((sb, rb, d), lambda s, t: (s, t, 0)),
            pl.BlockSpec((sb, rb, d), lambda s, t: (s, t, 0)),
            pl.BlockSpec((sb, 1, d), lambda s, t: (s, 0, 0)),
            pl.BlockSpec((sb, 1, d), lambda s, t: (s, 0, 0)),
            pl.BlockSpec((sb, 1, d), lambda s, t: (s, 0, 0)),
            pl.BlockSpec((1, d), const),
            pl.BlockSpec((d, d), const),
            pl.BlockSpec((d, 2 * LANE), const),
            pl.BlockSpec((1, LANE), const),
            pl.BlockSpec((tm, tm), const),
            pl.BlockSpec((N_EXPERTS, N_EXPERTS), const),
        ],
        out_specs=(pl.BlockSpec((sb, rb, d), lambda s, t: (s, t, 0)),
                   pl.BlockSpec((tm, d), lambda s, t: (s * nt + t, 0)),
                   pl.BlockSpec((tm, 2 * TOP_K), lambda s, t: (s * nt + t, 0)),
                   pl.BlockSpec((TOP_K, tm), tok),
                   pl.BlockSpec((1, N_EXPERTS, LANE), lambda s, t: (s * nt + t, 0, 0))),
        compiler_params=_cparams("arbitrary", "arbitrary"),
        name="outproj",
    )(mix, x, g1, sc2, sh2, n2, wo, wr, br, upper, lower)


def _unit_copy(src_ref, src_unit, dst_ref, dst_unit, sem):
    return pltpu.make_async_copy(src_ref.at[src_unit], dst_ref.at[dst_unit], sem)


def _start_unit_copies(nun, copy_of):
    def issue_pair(j, carry):
        copy_of(2 * j).start()
        copy_of(2 * j + 1).start()
        return carry

    lax.fori_loop(0, nun // 2, issue_pair, 0)

    @pl.when(nun % 2 == 1)
    def _():
        copy_of(nun - 1).start()


def _wait_unit_copies(nun, copy_of):
    def drain(u, carry):
        copy_of(0).wait()
        return carry

    lax.fori_loop(0, nun, drain, 0)


def _dispatch_body(udst_ref, nun_ref, slot_ref, h2_ref, xs_ref, xloc, sem):
    i = pl.program_id(0)
    last = pl.num_programs(0) - 1
    tb = slot_ref.shape[1]
    nun = nun_ref[i]
    s = slot_ref[...]
    h2 = h2_ref[...]
    buf = i % 2
    units_per_tile = LOCAL_ROWS // ROW_UNIT

    def copies(tile, b):
        return lambda u: _unit_copy(xloc.at[b], u, xs_ref, udst_ref[tile * units_per_tile + u], sem.at[b])

    def sort_chunk(c):
        r = lax.broadcasted_iota(jnp.int32, (SORT_CHUNK, tb), 0) + c * SORT_CHUNK
        hit = (s[0:1] == r) | (s[1:2] == r) | (s[2:3] == r) | (s[3:4] == r)
        p = jnp.where(hit, 1.0, 0.0).astype(BF16)
        units = SORT_CHUNK // ROW_UNIT
        xloc[buf, c * units:(c + 1) * units] = _dot(p, h2).astype(BF16).reshape(units, ROW_UNIT, h2.shape[1])

    typical_rows = TOP_K * tb + N_EXPERTS * ROW_UNIT // 2
    for c in range(LOCAL_ROWS // SORT_CHUNK):
        if (c + 1) * SORT_CHUNK <= typical_rows:
            sort_chunk(c)
        else:
            pl.when(c * (SORT_CHUNK // ROW_UNIT) < nun)(functools.partial(sort_chunk, c))

    _start_unit_copies(nun, copies(i, buf))

    @pl.when(i > 0)
    def _():
        _wait_unit_copies(nun_ref[i - 1], copies(i - 1, 1 - buf))

    @pl.when(i == last)
    def _():
        _wait_unit_copies(nun, copies(i, buf))


def _dispatch_first_kernel(udst_ref, nun_ref, slot_ref, h2_ref, xs_ref, xloc, sem):
    _dispatch_body(udst_ref, nun_ref, slot_ref, h2_ref, xs_ref, xloc, sem)


def _dispatch_next_kernel(udst_ref, nun_ref, slot_ref, h2_ref, xs_in_ref, xs_ref, xloc, sem):
    del xs_in_ref
    _dispatch_body(udst_ref, nun_ref, slot_ref, h2_ref, xs_ref, xloc, sem)


def _dispatch(udst, nun, slot, h2, xs, n_rows, tb):
    T, d = h2.shape
    any_spec = pl.BlockSpec(memory_space=pl.ANY)
    in_specs = [pl.BlockSpec((TOP_K, tb), lambda i, *_: (0, i)),
                pl.BlockSpec((tb, d), lambda i, *_: (i, 0))]
    aliases = {}
    body = _dispatch_first_kernel
    args = (udst, nun, slot, h2)
    if xs is not None:
        in_specs.append(any_spec)
        aliases = {4: 0}
        body = _dispatch_next_kernel
        args = args + (xs,)
    return pl.pallas_call(
        body,
        out_shape=jax.ShapeDtypeStruct((n_rows // ROW_UNIT, ROW_UNIT, d), BF16),
        grid_spec=pltpu.PrefetchScalarGridSpec(
            num_scalar_prefetch=2, grid=(T // tb,), in_specs=in_specs, out_specs=any_spec,
            scratch_shapes=[pltpu.VMEM((2, LOCAL_ROWS // ROW_UNIT, ROW_UNIT, d), BF16),
                            pltpu.SemaphoreType.DMA((2,))]),
        input_output_aliases=aliases,
        compiler_params=pltpu.CompilerParams(dimension_semantics=("arbitrary",), has_side_effects=True,
                                             vmem_limit_bytes=VMEM_LIMIT),
        name="dispatch_next" if xs is not None else "dispatch_first",
    )(*args)


def _experts_kernel(te_ref, tv_ref, tf_ref, tn_ref, xs_ref, wg_hbm, bg_ref, wu_hbm, bu_ref, wd_hbm, bd_ref, y_ref,
                    stage_g, stage_u, stage_d, wgb, wub, wdb, sem):
    i = pl.program_id(0)
    valid = tv_ref[i]
    tm, d = xs_ref.shape
    ff = wgb.shape[1]

    def weight_copies(e):
        return [pltpu.make_async_copy(src.at[e], dst, sem.at[j])
                for j, (src, dst) in enumerate(((wg_hbm, stage_g), (wu_hbm, stage_u), (wd_hbm, stage_d)))]

    @pl.when(i == 0)
    def _():
        for cp in weight_copies(te_ref[0]):
            cp.start()

    @pl.when(tf_ref[i] == 1)
    def _():
        for cp in weight_copies(te_ref[i]):
            cp.wait()
        wgb[...] = stage_g[...].astype(BF16)
        wub[...] = stage_u[...].astype(BF16)
        wdb[...] = stage_d[...].astype(BF16)

        @pl.when(tn_ref[i] >= 0)
        def _():
            for cp in weight_copies(tn_ref[i]):
                cp.start()

    @pl.when(valid > 0)
    def _():
        row = lax.broadcasted_iota(jnp.int32, (tm, 1), 0)
        xb = jnp.where(row < valid, xs_ref[...], jnp.zeros((), BF16))
        acc = jnp.zeros((tm, d), F32)
        for c in range(ff // FF_CHUNK):
            cs = slice(c * FF_CHUNK, (c + 1) * FF_CHUNK)
            gate = jnp.minimum(_dot(xb, wgb[:, cs]) + bg_ref[0, :, cs], SWIGLU_LIMIT)
            up = jnp.clip(_dot(xb, wub[:, cs]) + bu_ref[0, :, cs], -SWIGLU_LIMIT, SWIGLU_LIMIT)
            act = (up + 1.0) * gate * jax.nn.sigmoid(SWIGLU_ALPHA * gate)
            acc = acc + _dot(act.astype(BF16), wdb[cs, :])
        y_ref[...] = (acc + bd_ref[0]).astype(BF16)

    @pl.when(valid == 0)
    def _():
        y_ref[...] = jnp.zeros_like(y_ref)


def _experts(tile_expert, tile_valid, tile_first, tile_next, xs, wg, bg, wu, bu, wd, bd, tmg):
    n_tiles = tile_expert.shape[0]
    _, d, ff = wg.shape
    bspec = lambda shp: pl.BlockSpec(shp, lambda i, te, *_: (te[i], 0, 0))
    any_spec = pl.BlockSpec(memory_space=pl.ANY)
    return pl.pallas_call(
        _experts_kernel,
        out_shape=jax.ShapeDtypeStruct(xs.shape, BF16),
        grid_spec=pltpu.PrefetchScalarGridSpec(
            num_scalar_prefetch=4,
            grid=(n_tiles,),
            in_specs=[
                pl.BlockSpec((tmg, d), lambda i, *_: (i, 0)),
                any_spec, bspec((1, 1, ff)),
                any_spec, bspec((1, 1, ff)),
                any_spec, bspec((1, 1, d)),
            ],
            out_specs=pl.BlockSpec((tmg, d), lambda i, *_: (i, 0)),
            scratch_shapes=[pltpu.VMEM((d, ff), F32), pltpu.VMEM((d, ff), F32), pltpu.VMEM((ff, d), F32),
                            pltpu.VMEM((d, ff), BF16), pltpu.VMEM((d, ff), BF16), pltpu.VMEM((ff, d), BF16),
                            pltpu.SemaphoreType.DMA((3,))],
        ),
        compiler_params=pltpu.CompilerParams(dimension_semantics=("arbitrary",), vmem_limit_bytes=EXPERTS_VMEM_LIMIT),
        name="experts",
    )(tile_expert, tile_valid, tile_first, tile_next, xs, wg, bg, wu, bu, wd, bd)


def _combine_kernel(udst_ref, nun_ref, y_ref, route_ref, xmid_ref, g2_ref, nf_ref, o_ref, ybuf, sem):
    sb, rb, d = xmid_ref.shape
    tb = sb * rb
    i = pl.program_id(0) * pl.num_programs(1) + pl.program_id(1)
    n_steps = pl.num_programs(0) * pl.num_programs(1)
    buf = i % 2
    units_per_tile = LOCAL_ROWS // ROW_UNIT

    def copies(tile, b):
        return lambda u: _unit_copy(y_ref, udst_ref[tile * units_per_tile + u], ybuf.at[b], u, sem.at[b])

    @pl.when(i == 0)
    def _():
        ybuf[...] = jnp.zeros_like(ybuf)
        _start_unit_copies(nun_ref[0], copies(0, 0))

    _wait_unit_copies(nun_ref[i], copies(i, buf))

    @pl.when(i + 1 < n_steps)
    def _():
        _start_unit_copies(nun_ref[i + 1], copies(i + 1, 1 - buf))

    chunk_units = PERM_CHUNK // ROW_UNIT

    def sorted_rows(c):
        return ybuf[buf, c * chunk_units:(c + 1) * chunk_units].reshape(PERM_CHUNK, d)

    route = route_ref[...]
    lane = lax.broadcasted_iota(jnp.int32, (tb, PERM_CHUNK), 1).astype(F32)
    wk = [jnp.broadcast_to(route[:, k:k + 1], (tb, PERM_CHUNK)) for k in range(TOP_K)]
    sk = [jnp.broadcast_to(route[:, TOP_K + k:TOP_K + k + 1], (tb, PERM_CHUNK)) for k in range(TOP_K)]

    def weights_chunk(c):
        r = lane + float(c * PERM_CHUNK)
        pw = jnp.zeros((tb, PERM_CHUNK), F32)
        for k in range(TOP_K):
            pw = jnp.where(sk[k] == r, wk[k], pw)
        return pw.astype(BF16)

    def contribution(c):
        return _dot(weights_chunk(c), sorted_rows(c))

    moe = contribution(0)
    for c in range(1, LOCAL_ROWS // PERM_CHUNK):
        moe = moe + contribution(c)
    out = xmid_ref[...] + g2_ref[...] * moe.reshape(sb, rb, d)
    ms = jnp.mean(out * out, axis=-1, keepdims=True)
    o_ref[...] = out * lax.rsqrt(ms + EPS) * nf_ref[...]


def _combine(udst, nun, y, route, xmid, g2, nf, sb, rb):
    nseq, L, d = xmid.shape
    tb = sb * rb
    nt = L // rb
    tok = lambda s, t, *_: (s * nt + t, 0)
    return pl.pallas_call(
        _combine_kernel,
        out_shape=jax.ShapeDtypeStruct((nseq, L, d), F32),
        grid_spec=pltpu.PrefetchScalarGridSpec(
            num_scalar_prefetch=2,
            grid=(nseq // sb, nt),
            in_specs=[
                pl.BlockSpec(memory_space=pl.ANY),
                pl.BlockSpec((tb, 2 * TOP_K), tok),
                pl.BlockSpec((sb, rb, d), lambda s, t, *_: (s, t, 0)),
                pl.BlockSpec((sb, 1, d), lambda s, t, *_: (s, 0, 0)),
                pl.BlockSpec((1, d), lambda s, t, *_: (0, 0)),
            ],
            out_specs=pl.BlockSpec((sb, rb, d), lambda s, t, *_: (s, t, 0)),
            scratch_shapes=[pltpu.VMEM((2, LOCAL_ROWS // ROW_UNIT, ROW_UNIT, d), BF16),
                            pltpu.SemaphoreType.DMA((2,))],
        ),
        compiler_params=_cparams("arbitrary", "arbitrary"),
        name="combine",
    )(udst, nun, y, route, xmid, g2, nf)


def _tile_rows(nseq, L, tile):
    if L >= tile:
        assert L % tile == 0
        return 1, tile
    assert tile % L == 0 and nseq % (tile // L) == 0
    return tile // L, L


def kernel(x_prompt, x_sample, state_gla, c_prompt, c_sample, w_ada, b_ada, norm1, w_in, w_gk, b_gk, gla_norm,
           gmlp_ln_g, gmlp_ln_b, gmlp_w_s, gmlp_b_s, w_out, norm2, w_router, b_router, w_gate, b_gate, w_up,
           b_up, w_down, b_down, norm_f):
    depth = w_ada.shape[0]
    assert depth == 1
    bp, lp, d = x_prompt.shape
    bs, ls, _ = x_sample.shape
    tp, ts = bp * lp, bs * ls

    nc = bp + bs
    ncp = -(-nc // SUBLANE) * SUBLANE
    c_all = jnp.concatenate([c_prompt, c_sample, jnp.zeros((ncp - nc, d), F32)], axis=0)
    mod = _ada(c_all, w_ada[0], b_ada[0][None]).reshape(ncp, N_MOD, 1, d)
    mods_p = [mod[:bp, i] for i in range(N_MOD)]
    mods_s = [mod[bp:nc, i] for i in range(N_MOD)]

    wi = w_in[0]
    c_lr = 2 * GLA_QK + GLA_WIDTH
    c_r = c_lr + GLA_LOWRANK
    wm = jnp.concatenate([wi[:, :c_lr], wi[:, c_r:]], axis=1).astype(BF16)
    wlr = jnp.pad(wi[:, c_lr:c_r], ((0, 0), (0, LANE - GLA_LOWRANK))).astype(BF16)
    wgk_f = jnp.pad(w_gk[0], ((0, LANE - GLA_LOWRANK), (0, 0)))
    wgk_hi = wgk_f.astype(BF16)
    wgk = jnp.concatenate([wgk_hi, (wgk_f - wgk_hi.astype(F32)).astype(BF16)], axis=1)
    bgk = b_gk[0][None]
    n1, n2, nf = norm1[0][None], norm2[0][None], norm_f[None]
    gn, lng, lnb = gla_norm[0][None], gmlp_ln_g[0][None], gmlp_ln_b[0][None]
    ws, bsv = gmlp_w_s[0], gmlp_b_s[0]
    pos_i = jnp.arange(GMLP_BLOCK)
    cmask = (pos_i[None, :] // CHUNK) <= (pos_i[:, None] // CHUNK)
    wm_p = jnp.where(cmask[None], ws, 0.0).astype(BF16)
    bsb_p = jnp.repeat(bsv.T, GMLP_GC, axis=1)
    reps = ROW_BLOCK // ls
    eye = jnp.eye(reps, dtype=F32)
    wm_s = jnp.einsum("ab,gij->gaibj", eye, ws[:, :ls, :ls]).reshape(GMLP_GROUPS, ROW_BLOCK, ROW_BLOCK).astype(BF16)
    bsb_s = jnp.tile(jnp.repeat(bsv[:, :ls].T, GMLP_GC, axis=1), (reps, 1))
    wo = w_out[0].astype(BF16)
    wr_f = jnp.pad(w_router[0], ((0, 0), (0, LANE - N_EXPERTS)))
    wr_hi = wr_f.astype(BF16)
    wr = jnp.concatenate([wr_hi, (wr_f - wr_hi.astype(F32)).astype(BF16)], axis=1)
    br = jnp.concatenate([b_router[0], jnp.full((LANE - N_EXPERTS,), -1e30, F32)])[None]
    upper = (jnp.arange(TOKEN_TILE)[:, None] < jnp.arange(TOKEN_TILE)[None, :]).astype(BF16)
    lower = (jnp.arange(N_EXPERTS)[None, :] < jnp.arange(N_EXPERTS)[:, None]).astype(F32)
    wg, wu, wd = w_gate[0], w_up[0], w_down[0]
    bg, bu, bd = b_gate[0][:, None], b_up[0][:, None], b_down[0][:, None]

    sbp, rbp = _tile_rows(bp, lp, TOKEN_TILE)
    sbs, rbs = _tile_rows(bs, ls, TOKEN_TILE)

    assert lp % TOKEN_TILE == 0
    mix_p, state_p = _mixer_prompt(x_prompt, mods_p[1], mods_p[0], n1, wm, wlr, wgk, bgk, gn, lng, lnb, wm_p, bsb_p,
                                   TOKEN_TILE)
    proj_s, gk_s = _inproj(x_sample, mods_s[1], mods_s[0], n1, wm, wlr, wgk, bgk, sbs, rbs)
    mix_s, state_s, vn_s = _mixer_sample(proj_s, gk_s, state_gla[0], gn, lng, lnb, wm_s, bsb_s)

    xmid_p, h2_p, route_p, slot_p, n16_p = _outproj(
        mix_p, x_prompt, mods_p[2], mods_p[4], mods_p[3], n2, wo, wr, br, upper, lower, sbp, rbp)
    xmid_s, h2_s, route_s, slot_s, n16_s = _outproj(
        mix_s, x_sample, mods_s[2], mods_s[4], mods_s[3], n2, wo, wr, br, upper, lower, sbs, rbs)

    tmg = EXPERT_TILE
    ntp = tp // TOKEN_TILE
    eids = jnp.arange(N_EXPERTS, dtype=jnp.int32)
    n16 = jnp.concatenate([n16_p[:, :, 0], n16_s[:, :, 0]], axis=0).astype(jnp.int32)
    nt_all = n16.shape[0]
    earlier = jnp.cumsum(n16, axis=0) - n16
    tot = jnp.sum(n16, axis=0)
    tiles_e = (tot + tmg - 1) // tmg
    tile_end = jnp.cumsum(tiles_e)
    tile_start = tile_end - tiles_e
    row_start = tile_start * tmg
    n_tiles = (TOP_K * (tp + ts) + nt_all * N_EXPERTS * (ROW_UNIT - 1)) // tmg + N_EXPERTS
    tid = jnp.arange(n_tiles, dtype=jnp.int32)
    te = jnp.minimum(jnp.sum((tid[:, None] >= tile_end[None, :]).astype(jnp.int32), axis=1), N_EXPERTS - 1)
    te_hot = te[:, None] == eids[None, :]
    tot_te = jnp.sum(jnp.where(te_hot, tot[None, :], 0), axis=1)
    start_te = jnp.sum(jnp.where(te_hot, tile_start[None, :], 0), axis=1)
    active = tid < tile_end[-1]
    tv = jnp.where(active, jnp.clip(tot_te - (tid - start_te) * tmg, 0, tmg), 0).astype(jnp.int32)
    last_e = jnp.max(jnp.where(tiles_e > 0, eids, 0)).astype(jnp.int32)
    te = jnp.where(active, te, last_e).astype(jnp.int32)
    te_prev = jnp.concatenate([jnp.full((1,), -1, jnp.int32), te[:-1]])
    tf = (active & (te != te_prev)).astype(jnp.int32)
    later_used = (eids[None, :] > te[:, None]) & (tiles_e[None, :] > 0)
    tn = jnp.min(jnp.where(later_used, eids[None, :], N_EXPERTS), axis=1)
    tn = jnp.where(tn < N_EXPERTS, tn, -1).astype(jnp.int32)

    run_end = jnp.cumsum(n16, axis=1)
    run_start = run_end - n16
    unit_row = jnp.arange(LOCAL_ROWS // ROW_UNIT, dtype=jnp.int32) * ROW_UNIT
    unit_e = jnp.sum((unit_row[None, :, None] >= run_end[:, None, :]).astype(jnp.int32), axis=2)
    unit_hot = unit_e[:, :, None] == eids[None, None, :]
    run_off = row_start[None, :] + earlier - run_start
    udst = jnp.sum(jnp.where(unit_hot, run_off[:, None, :], 0), axis=2) + unit_row[None, :]
    udst = (jnp.where(unit_row[None, :] < run_end[:, -1:], udst, 0) // ROW_UNIT).astype(jnp.int32)
    nun = (run_end[:, -1] // ROW_UNIT).astype(jnp.int32)

    n_rows = n_tiles * tmg
    udst_p, udst_s = udst[:ntp].reshape(-1), udst[ntp:].reshape(-1)
    xs = _dispatch(udst_p, nun[:ntp], slot_p, h2_p, None, n_rows, TOKEN_TILE)
    xs = _dispatch(udst_s, nun[ntp:], slot_s, h2_s, xs, n_rows, TOKEN_TILE)
    y = _experts(te, tv, tf, tn, xs.reshape(n_rows, d), wg, bg, wu, bu, wd, bd, tmg)
    y = y.reshape(n_rows // ROW_UNIT, ROW_UNIT, d)
    y_prompt = _combine(udst_p, nun[:ntp], y, route_p, xmid_p, mods_p[5], nf, sbp, rbp)
    y_sample = _combine(udst_s, nun[ntp:], y, route_s, xmid_s, mods_s[5], nf, sbs, rbs)

    return (y_prompt, y_sample, state_p[None], state_s[None], vn_s[None])
```

```python
import functools

import jax
import jax.numpy as jnp
from jax import lax
from jax.experimental import pallas as pl
from jax.experimental.pallas import tpu as pltpu

F32 = jnp.float32
BF16 = jnp.bfloat16
HIGHEST = lax.Precision.HIGHEST

CHUNK = 64
GLA_HEADS = 4
GLA_DK = 64
GLA_DV = 128
GLA_QK = GLA_HEADS * GLA_DK
GLA_WIDTH = GLA_HEADS * GLA_DV
GLA_LOWRANK = 16
GLA_GATE_NORM = 16.0
GATE_SAFE_MIN = -60.0
GMLP_WIDTH = 512
GMLP_GROUPS = 4
GMLP_GC = GMLP_WIDTH // GMLP_GROUPS
GMLP_BLOCK = 128
N_EXPERTS = 32
TOP_K = 4
SWIGLU_LIMIT = 7.0
SWIGLU_ALPHA = 1.702
EPS = 1e-6
N_MOD = 6

LANE = 128
SUBLANE = 8

Q0 = 0
K0 = Q0 + GLA_QK
V0 = K0 + GLA_QK
R0 = V0 + GLA_WIDTH
U0 = R0 + GLA_WIDTH
G0 = U0 + GMLP_WIDTH
PROJ_COLS = G0 + GMLP_WIDTH

ROW_BLOCK = 128
TOKEN_TILE = 512
EXPERT_TILE = 512
FF_CHUNK = 512
ROW_UNIT = 16
LOCAL_ROWS = 2560
WAIT_GROUP = 8
MAX_PAIRS = LOCAL_ROWS // (2 * ROW_UNIT)
N_RUN_TABLES = 6
SORT_CHUNK = 128
PERM_CHUNK = 256
VMEM_LIMIT = 48 * 1024 * 1024
EXPERTS_VMEM_LIMIT = 56 * 1024 * 1024


def _cparams(*sem):
    return pltpu.CompilerParams(dimension_semantics=sem, vmem_limit_bytes=VMEM_LIMIT)


def _dot(a, b, **kw):
    return jnp.dot(a, b, preferred_element_type=F32, **kw)


def _dot_nt(a, b):
    return lax.dot_general(a, b, (((1,), (1,)), ((), ())), preferred_element_type=F32)


def _gelu(x):
    return 0.5 * x * (1.0 + lax.erf(x * (0.5 ** 0.5)))


def _dot_tn(a, b):
    return lax.dot_general(a, b, (((0,), (0,)), ((), ())), preferred_element_type=F32)


def _ada_kernel(c_ref, w_ref, b_ref, o_ref):
    c = c_ref[...]
    s = c * jax.nn.sigmoid(c)
    o_ref[...] = _dot(s, w_ref[...], precision=HIGHEST) + b_ref[...]


def _ada(c, w_ada, b_ada):
    n, d = c.shape
    return pl.pallas_call(
        _ada_kernel,
        out_shape=jax.ShapeDtypeStruct((n, N_MOD * d), F32),
        grid=(N_MOD,),
        in_specs=[
            pl.BlockSpec((n, d), lambda j: (0, 0)),
            pl.BlockSpec((d, d), lambda j: (0, j)),
            pl.BlockSpec((1, d), lambda j: (0, j)),
        ],
        out_specs=pl.BlockSpec((n, d), lambda j: (0, j)),
        compiler_params=_cparams("arbitrary"),
        name="ada",
    )(c, w_ada, b_ada)


PROJ_COL_CHUNK = 512


def _modulated_norm(x_ref, sc_ref, sh_ref, n1_ref):
    sb, rb, d = x_ref.shape
    x = x_ref[...]
    ms = jnp.mean(x * x, axis=-1, keepdims=True)
    h = x * lax.rsqrt(ms + EPS) * n1_ref[...]
    h = h * (1.0 + sc_ref[...]) + sh_ref[...]
    return h.reshape(sb * rb, d).astype(BF16)


def _gate_preact(hb, wlr_ref, wgk_ref, bgk_ref):
    lr = _dot(hb, wlr_ref[...])
    lr_hi = lr.astype(BF16)
    lr_lo = (lr - lr_hi.astype(F32)).astype(BF16)
    part = _dot(lr_hi, wgk_ref[...])
    return part[:, :GLA_QK] + part[:, GLA_QK:] + _dot(lr_lo, wgk_ref[:, :GLA_QK]) + bgk_ref[...]


def _inproj_kernel(x_ref, sc_ref, sh_ref, n1_ref, wm_ref, wlr_ref, wgk_ref, bgk_ref, proj_ref, gk_ref):
    sb, rb, _ = x_ref.shape
    hb = _modulated_norm(x_ref, sc_ref, sh_ref, n1_ref)
    cw = PROJ_COL_CHUNK
    for c in range(PROJ_COLS // cw):
        p = _dot(hb, wm_ref[:, c * cw:(c + 1) * cw])
        proj_ref[:, :, c * cw:(c + 1) * cw] = p.astype(BF16).reshape(sb, rb, cw)
    gk_ref[...] = _gate_preact(hb, wlr_ref, wgk_ref, bgk_ref).reshape(sb, rb, GLA_QK)


def _inproj(x, sc, sh, n1, wm, wlr, wgk, bgk, sb, rb):
    nseq, L, d = x.shape
    grid = (nseq // sb, L // rb)
    const = lambda s, t: (0, 0)
    return pl.pallas_call(
        _inproj_kernel,
        out_shape=(jax.ShapeDtypeStruct((nseq, L, PROJ_COLS), BF16),
                   jax.ShapeDtypeStruct((nseq, L, GLA_QK), F32)),
        grid=grid,
        in_specs=[
            pl.BlockSpec((sb, rb, d), lambda s, t: (s, t, 0)),
            pl.BlockSpec((sb, 1, d), lambda s, t: (s, 0, 0)),
            pl.BlockSpec((sb, 1, d), lambda s, t: (s, 0, 0)),
            pl.BlockSpec((1, d), const),
            pl.BlockSpec((d, PROJ_COLS), const),
            pl.BlockSpec((d, LANE), const),
            pl.BlockSpec((LANE, 2 * GLA_QK), const),
            pl.BlockSpec((1, GLA_QK), const),
        ],
        out_specs=(pl.BlockSpec((sb, rb, PROJ_COLS), lambda s, t: (s, t, 0)),
                   pl.BlockSpec((sb, rb, GLA_QK), lambda s, t: (s, t, 0))),
        compiler_params=_cparams("arbitrary", "arbitrary"),
        name="inproj",
    )(x, sc, sh, n1, wm, wlr, wgk, bgk)


def _head_masks():
    lane = lax.broadcasted_iota(jnp.int32, (1, GLA_QK), 1)
    return [(lane // GLA_DK) == h for h in range(GLA_HEADS)]


def _stack_heads(x, hm):
    return jnp.concatenate([jnp.where(m, x, 0.0) for m in hm], axis=0).astype(BF16)


def _chunk_pair_ids(c_len):
    ti = lax.broadcasted_iota(jnp.int32, (GLA_HEADS * c_len, c_len), 0) % c_len
    si = lax.broadcasted_iota(jnp.int32, (GLA_HEADS * c_len, c_len), 1)
    return ti, si


def _scores_factored(q4s, k, G, c_len):
    ti, si = _chunk_pair_ids(c_len)
    ke = (k * jnp.exp(-G)).astype(BF16)
    return jnp.concatenate(
        [jnp.where(si <= ti, _dot_nt(q4, ke[c * c_len:(c + 1) * c_len]), 0.0) for c, q4 in enumerate(q4s)], axis=0)


def _scores_bounded(qs, k, G, c_len, hm):
    ti, si = _chunk_pair_ids(c_len)
    t = lax.broadcasted_iota(jnp.int32, (c_len, 1), 0)
    col = lax.broadcasted_iota(jnp.int32, (c_len, c_len), 1)
    outs = []
    for c in range(qs.shape[0] // c_len):
        sl = slice(c * c_len, (c + 1) * c_len)
        q_c, k_c, g_c = qs[sl], k[sl], G[sl]
        a = jnp.where(si == ti, _dot_nt(_stack_heads(q_c, hm), k_c.astype(BF16)), 0.0)
        half = c_len // 2
        while half >= 1:
            blk = 2 * half
            sel = (col == (t // blk) * blk + (half - 1)).astype(F32)
            ref = _dot(sel, g_c, precision=HIGHEST)
            upper = (t % blk) >= half
            qh = jnp.where(upper, q_c * jnp.exp(jnp.minimum(g_c - ref, 0.0)), 0.0)
            kh = jnp.where(upper, 0.0, k_c * jnp.exp(jnp.minimum(ref - g_c, 0.0)))
            same = (ti // blk) == (si // blk)
            a = a + jnp.where(same, _dot_nt(_stack_heads(qh, hm), kh.astype(BF16)), 0.0)
            half //= 2
        outs.append(a)
    return jnp.concatenate(outs, axis=0)


def _cum_log_gates(gkpre, c_len):
    rows = gkpre.shape[0]
    g = jax.nn.log_sigmoid(gkpre) / GLA_GATE_NORM
    ri = lax.broadcasted_iota(jnp.int32, (rows, rows), 0)
    ci = lax.broadcasted_iota(jnp.int32, (rows, rows), 1)
    tri = jnp.where((ci <= ri) & ((ci // c_len) == (ri // c_len)), 1.0, 0.0).astype(BF16)
    g_hi = g.astype(BF16)
    r1 = g - g_hi.astype(F32)
    g_mid = r1.astype(BF16)
    g_lo = (r1 - g_mid.astype(F32)).astype(BF16)
    parts = _dot(tri, jnp.concatenate([g_hi, g_mid, g_lo], axis=1))
    return parts[:, :GLA_QK] + parts[:, GLA_QK:2 * GLA_QK] + parts[:, 2 * GLA_QK:]


def _factored_is_safe(G, c_len):
    ends = [G[(c + 1) * c_len - 1:(c + 1) * c_len] for c in range(G.shape[0] // c_len)]
    return jnp.min(jnp.concatenate(ends, axis=0)) > GATE_SAFE_MIN


def _mixer_block(p, G, states, c_len, gn, lng, lnb, wm_ref, bsb, bounded, side=()):
    rows = p.shape[0]
    n_chunks = rows // c_len
    chained = len(states) == 1
    hm = _head_masks()
    side = list(side)

    def side_step():
        if side:
            side.pop(0)()

    q = p[:, Q0:Q0 + GLA_QK].astype(F32)
    k = p[:, K0:K0 + GLA_QK].astype(F32)
    v = p[:, V0:V0 + GLA_WIDTH]
    r = p[:, R0:R0 + GLA_WIDTH].astype(F32)

    qs = q * (GLA_DK ** -0.5)
    qe = qs * jnp.exp(G)
    q4s = [_stack_heads(qe[c * c_len:(c + 1) * c_len], hm) for c in range(n_chunks)]
    scores = _scores_bounded(qs, k, G, c_len, hm) if bounded else _scores_factored(q4s, k, G, c_len)
    hc = GLA_HEADS * c_len

    new_states = []
    o_rows = []
    st = states[0]
    for c in range(n_chunks):
        lo, hi = c * c_len, (c + 1) * c_len
        if not chained:
            st = states[c]
        q4 = q4s[c]
        a = scores[c * hc:(c + 1) * hc].astype(BF16)
        o_inter = _dot_nt(q4, st.astype(BF16))
        v_c = v[lo:hi]
        heads = []
        for h in range(GLA_HEADS):
            o_h = o_inter[h * c_len:(h + 1) * c_len] + _dot(
                a[h * c_len:(h + 1) * c_len], v_c[:, h * GLA_DV:(h + 1) * GLA_DV])
            heads.append(o_h)
        o_rows.append(jnp.concatenate(heads, axis=1))
        g_last = G[hi - 1:hi]
        kd = (k[lo:hi] * jnp.exp(g_last - G[lo:hi])).astype(BF16)
        upd = _dot_tn(v_c, kd)
        st_new = jnp.exp(g_last) * st
        for h in range(GLA_HEADS):
            st_new = st_new + jnp.where(hm[h], upd[h * GLA_DV:(h + 1) * GLA_DV], 0.0)
        if chained:
            st = st_new
        else:
            new_states.append(st_new)
        side_step()
    if chained:
        new_states = [st]
    o = jnp.concatenate(o_rows, axis=0)

    gla = []
    for h in range(GLA_HEADS):
        o_h = o[:, h * GLA_DV:(h + 1) * GLA_DV]
        ms = jnp.mean(o_h * o_h, axis=-1, keepdims=True)
        r_h = r[:, h * GLA_DV:(h + 1) * GLA_DV]
        gla.append(o_h * lax.rsqrt(ms + EPS) * gn * (r_h * jax.nn.sigmoid(r_h)))
    side_step()

    u = _gelu(p[:, U0:U0 + GMLP_WIDTH].astype(F32))
    side_step()
    vv = _gelu(p[:, G0:G0 + GMLP_WIDTH].astype(F32))
    side_step()
    mu = jnp.mean(vv, axis=-1, keepdims=True)
    xc = vv - mu
    var = jnp.mean(xc * xc, axis=-1, keepdims=True)
    vn = xc * lax.rsqrt(var + EPS) * lng + lnb
    vnb = vn.astype(BF16)
    gm = []
    for gi in range(GMLP_GROUPS):
        sl = slice(gi * GMLP_GC, (gi + 1) * GMLP_GC)
        mixed = _dot(wm_ref[gi], vnb[:, sl]) + bsb[:, sl]
        gm.append(u[:, sl] * mixed)
    out = jnp.concatenate(gla + gm, axis=1)
    while side:
        side_step()
    return out, new_states, vn


def _mixer_prompt_kernel(x_ref, sc_ref, sh_ref, n1_ref, wm_ref, wlr_ref, wgk_ref, bgk_ref,
                         gn_ref, lng_ref, lnb_ref, ws_ref, bsb_ref, mix_ref, s_ref, st_scr):
    t = pl.program_id(1)
    nt = pl.num_programs(1)

    @pl.when(t == 0)
    def _():
        st_scr[...] = jnp.zeros_like(st_scr)

    tb = x_ref.shape[1]
    n_sub = tb // ROW_BLOCK
    cw = PROJ_COL_CHUNK
    hb = _modulated_norm(x_ref, sc_ref, sh_ref, n1_ref)

    def proj_steps(j, parts):
        hb_j = hb[j * ROW_BLOCK:(j + 1) * ROW_BLOCK]
        return [functools.partial(
            lambda c: parts.append(_dot(hb_j, wm_ref[:, c * cw:(c + 1) * cw]).astype(BF16)), c)
            for c in range(PROJ_COLS // cw)]

    gk = _gate_preact(hb, wlr_ref, wgk_ref, bgk_ref)
    first = []
    for step in proj_steps(0, first):
        step()
    Gs = [_cum_log_gates(gk[j * ROW_BLOCK:(j + 1) * ROW_BLOCK], CHUNK) for j in range(n_sub)]
    safe = _factored_is_safe(jnp.concatenate(Gs, axis=0), CHUNK)

    def run(bounded):
        st = st_scr[...]
        parts = first
        for j in range(n_sub):
            p_j = jnp.concatenate(parts, axis=1)
            parts = []
            side = proj_steps(j + 1, parts) if j + 1 < n_sub else []
            out, sts, _ = _mixer_block(p_j, Gs[j], [st], CHUNK, gn_ref[...], lng_ref[...], lnb_ref[...], ws_ref,
                                       bsb_ref[...], bounded, side)
            st = sts[0]
            mix_ref[0, j * ROW_BLOCK:(j + 1) * ROW_BLOCK, :] = out.astype(BF16)
        st_scr[...] = st

    pl.when(safe)(functools.partial(run, False))
    pl.when(jnp.logical_not(safe))(functools.partial(run, True))

    @pl.when(t == nt - 1)
    def _():
        s_ref[0] = st_scr[...].T.reshape(GLA_HEADS, GLA_DK, GLA_DV)


def _mixer_prompt(x, sc, sh, n1, wm, wlr, wgk, bgk, gn, lng, lnb, ws, bsb, tb):
    b, L, d = x.shape
    const2 = lambda s, t: (0, 0)
    return pl.pallas_call(
        _mixer_prompt_kernel,
        out_shape=(jax.ShapeDtypeStruct((b, L, 2 * GLA_WIDTH), BF16),
                   jax.ShapeDtypeStruct((b, GLA_HEADS, GLA_DK, GLA_DV), F32)),
        grid=(b, L // tb),
        in_specs=[
            pl.BlockSpec((1, tb, d), lambda s, t: (s, t, 0)),
            pl.BlockSpec((1, 1, d), lambda s, t: (s, 0, 0)),
            pl.BlockSpec((1, 1, d), lambda s, t: (s, 0, 0)),
            pl.BlockSpec((1, d), const2),
            pl.BlockSpec((d, PROJ_COLS), const2),
            pl.BlockSpec((d, LANE), const2),
            pl.BlockSpec((LANE, 2 * GLA_QK), const2),
            pl.BlockSpec((1, GLA_QK), const2),
            pl.BlockSpec((1, GLA_DV), const2),
            pl.BlockSpec((1, GMLP_WIDTH), const2),
            pl.BlockSpec((1, GMLP_WIDTH), const2),
            pl.BlockSpec((GMLP_GROUPS, ROW_BLOCK, ROW_BLOCK), lambda s, t: (0, 0, 0)),
            pl.BlockSpec((ROW_BLOCK, GMLP_WIDTH), const2),
        ],
        out_specs=(pl.BlockSpec((1, tb, 2 * GLA_WIDTH), lambda s, t: (s, t, 0)),
                   pl.BlockSpec((1, GLA_HEADS, GLA_DK, GLA_DV), lambda s, t: (s, 0, 0, 0))),
        scratch_shapes=[pltpu.VMEM((GLA_DV, GLA_QK), F32)],
        compiler_params=_cparams("arbitrary", "arbitrary"),
        name="mixer_prompt",
    )(x, sc, sh, n1, wm, wlr, wgk, bgk, gn, lng, lnb, ws, bsb)


def _mixer_sample_kernel(proj_ref, gk_ref, s0_ref, gn_ref, lng_ref, lnb_ref, wm_ref, bsb_ref,
                         mix_ref, s_ref, vn_ref):
    sb, rb, _ = proj_ref.shape
    G = _cum_log_gates(gk_ref[...].reshape(sb * rb, GLA_QK), rb)
    safe = _factored_is_safe(G, rb)

    def run(bounded):
        p = proj_ref[...].reshape(sb * rb, PROJ_COLS)
        states = [s0_ref[i].reshape(GLA_QK, GLA_DV).T for i in range(sb)]
        out, sts, vn = _mixer_block(p, G, states, rb, gn_ref[...], lng_ref[...], lnb_ref[...], wm_ref,
                                    bsb_ref[...], bounded)
        mix_ref[...] = out.astype(BF16).reshape(sb, rb, 2 * GLA_WIDTH)
        vn_ref[...] = vn.reshape(sb, rb, GMLP_WIDTH)
        for i in range(sb):
            s_ref[i] = sts[i].T.reshape(GLA_HEADS, GLA_DK, GLA_DV)

    pl.when(safe)(functools.partial(run, False))
    pl.when(jnp.logical_not(safe))(functools.partial(run, True))


def _mixer_sample(proj, gk, s0, gn, lng, lnb, wm, bsb):
    n, L, _ = proj.shape
    sb = ROW_BLOCK // L
    const2 = lambda s: (0, 0)
    return pl.pallas_call(
        _mixer_sample_kernel,
        out_shape=(jax.ShapeDtypeStruct((n, L, 2 * GLA_WIDTH), BF16),
                   jax.ShapeDtypeStruct((n, GLA_HEADS, GLA_DK, GLA_DV), F32),
                   jax.ShapeDtypeStruct((n, L, GMLP_WIDTH), F32)),
        grid=(n // sb,),
        in_specs=[
            pl.BlockSpec((sb, L, PROJ_COLS), lambda s: (s, 0, 0)),
            pl.BlockSpec((sb, L, GLA_QK), lambda s: (s, 0, 0)),
            pl.BlockSpec((sb, GLA_HEADS, GLA_DK, GLA_DV), lambda s: (s, 0, 0, 0)),
            pl.BlockSpec((1, GLA_DV), const2),
            pl.BlockSpec((1, GMLP_WIDTH), const2),
            pl.BlockSpec((1, GMLP_WIDTH), const2),
            pl.BlockSpec((GMLP_GROUPS, ROW_BLOCK, ROW_BLOCK), lambda s: (0, 0, 0)),
            pl.BlockSpec((ROW_BLOCK, GMLP_WIDTH), const2),
        ],
        out_specs=(pl.BlockSpec((sb, L, 2 * GLA_WIDTH), lambda s: (s, 0, 0)),
                   pl.BlockSpec((sb, GLA_HEADS, GLA_DK, GLA_DV), lambda s: (s, 0, 0, 0)),
                   pl.BlockSpec((sb, L, GMLP_WIDTH), lambda s: (s, 0, 0))),
        compiler_params=_cparams("arbitrary"),
        name="mixer_sample",
    )(proj, gk, s0, gn, lng, lnb, wm, bsb)


def _outproj_kernel(mix_ref, x_ref, g1_ref, sc_ref, sh_ref, n2_ref, wo_ref, wr_ref, br_ref, upper_ref, lower_ref,
                    xmid_ref, h2_ref, route_ref, slot_ref, n16_ref):
    sb, rb, d = x_ref.shape
    tm = sb * rb

    y = _dot(mix_ref[...].reshape(tm, d), wo_ref[...])
    xm = x_ref[...] + g1_ref[...] * y.reshape(sb, rb, d)
    xmid_ref[...] = xm
    ms = jnp.mean(xm * xm, axis=-1, keepdims=True)
    h2 = xm * lax.rsqrt(ms + EPS) * n2_ref[...]
    h2 = (h2 * (1.0 + sc_ref[...]) + sh_ref[...]).reshape(tm, d)
    h_hi = h2.astype(BF16)
    h2_ref[...] = h_hi

    h_lo = (h2 - h_hi.astype(F32)).astype(BF16)
    part = _dot(h_hi, wr_ref[...])
    logits = part[:, :LANE] + part[:, LANE:] + _dot(h_lo, wr_ref[:, :LANE]) + br_ref[...]
    l = logits.T[:N_EXPERTS]
    eid = lax.broadcasted_iota(jnp.int32, (N_EXPERTS, tm), 0).astype(F32)
    top_l, sel = [], []
    for _ in range(TOP_K):
        m = jnp.max(l, axis=0, keepdims=True)
        idx = jnp.min(jnp.where(l == m, eid, float(N_EXPERTS)), axis=0, keepdims=True)
        hit = eid == idx
        top_l.append(m)
        sel.append(hit)
        l = jnp.where(hit, -jnp.inf, l)
    ex = [jnp.exp(t - top_l[0]) for t in top_l]
    den = ex[0] + ex[1] + ex[2] + ex[3]
    top_w = [e / den for e in ex]

    chosen = (sel[0] | sel[1] | sel[2] | sel[3])
    cb = jnp.where(chosen, 1.0, 0.0)
    before = _dot(cb.astype(BF16), upper_ref[...])
    n = jnp.sum(cb, axis=1, keepdims=True)
    n16 = jnp.floor((n + (ROW_UNIT - 1)) * (1.0 / ROW_UNIT)) * ROW_UNIT
    n16b = jnp.broadcast_to(n16, (N_EXPERTS, LANE))
    ls = _dot(lower_ref[...], n16b, precision=HIGHEST)[:, 0:1]
    base = before + ls
    slots = [jnp.sum(jnp.where(s, base, 0.0), axis=0, keepdims=True) for s in sel]
    slot_ref[...] = jnp.concatenate(slots, axis=0).astype(jnp.int32)
    n16_ref[0] = n16b
    rows = jnp.concatenate(top_w + slots + [jnp.zeros((LANE - 2 * TOP_K, tm), F32)], axis=0)
    route_ref[...] = rows.T[:, :2 * TOP_K]


def _outproj(mix, x, g1, sc2, sh2, n2, wo, wr, br, upper, lower, sb, rb):
    nseq, L, d = x.shape
    tm = sb * rb
    nt = L // rb
    T = nseq * L
    grid = (nseq // sb, nt)
    const = lambda s, t: (0, 0)
    tok = lambda s, t: (0, s * nt + t)
    return pl.pallas_call(
        _outproj_kernel,
        out_shape=(jax.ShapeDtypeStruct((nseq, L, d), F32),
                   jax.ShapeDtypeStruct((T, d), BF16),
                   jax.ShapeDtypeStruct((T, 2 * TOP_K), F32),
                   jax.ShapeDtypeStruct((TOP_K, T), jnp.int32),
                   jax.ShapeDtypeStruct((T // tm, N_EXPERTS, LANE), F32)),
        grid=grid,
        in_specs=[
            pl.BlockSpec((sb, rb, d), lambda s, t: (s, t, 0)),
            pl.BlockSpec((sb, rb, d), lambda s, t: (s, t, 0)),
            pl.BlockSpec((sb, 1, d), lambda s, t: (s, 0, 0)),
            pl.BlockSpec((sb, 1, d), lambda s, t: (s, 0, 0)),
            pl.BlockSpec((sb, 1, d), lambda s, t: (s, 0, 0)),
            pl.BlockSpec((1, d), const),
            pl.BlockSpec((d, d), const),
            pl.BlockSpec((d, 2 * LANE), const),
            pl.BlockSpec((1, LANE), const),
            pl.BlockSpec((tm, tm), const),
            pl.BlockSpec((N_EXPERTS, N_EXPERTS), const),
        ],
        out_specs=(pl.BlockSpec((sb, rb, d), lambda s, t: (s, t, 0)),
                   pl.BlockSpec((tm, d), lambda s, t: (s * nt + t, 0)),
                   pl.BlockSpec((tm, 2 * TOP_K), lambda s, t: (s * nt + t, 0)),
                   pl.BlockSpec((TOP_K, tm), tok),
                   pl.BlockSpec((1, N_EXPERTS, LANE), lambda s, t: (s * nt + t, 0, 0))),
        compiler_params=_cparams("arbitrary", "arbitrary"),
        name="outproj",
    )(mix, x, g1, sc2, sh2, n2, wo, wr, br, upper, lower)


def _unit_copy(src_ref, src_unit, dst_ref, dst_unit, sem):
    return pltpu.make_async_copy(src_ref.at[src_unit], dst_ref.at[dst_unit], sem)


def _pair_copy(src_ref, src_unit, dst_ref, dst_unit, sem):
    return pltpu.make_async_copy(src_ref.at[pl.ds(src_unit, 2)], dst_ref.at[pl.ds(dst_unit, 2)], sem)


def _start_run_copies(tabs, tile, loc_ref, glob_ref, sem, to_global):
    pair_loc, pair_glob, n_pair, single_loc, single_glob, n_single = tabs
    p0 = tile * MAX_PAIRS
    s0 = tile * N_EXPERTS

    def mk(copy, loc, glob):
        return copy(loc_ref, loc, glob_ref, glob, sem) if to_global else copy(glob_ref, glob, loc_ref, loc, sem)

    def issue_pairs(j, carry):
        for q in (p0 + 2 * j, p0 + 2 * j + 1):
            mk(_pair_copy, pair_loc[q], pair_glob[q]).start()
        return carry

    npair = n_pair[tile]
    lax.fori_loop(0, npair // 2, issue_pairs, 0)

    @pl.when(npair % 2 == 1)
    def _():
        q = p0 + npair - 1
        mk(_pair_copy, pair_loc[q], pair_glob[q]).start()

    def issue_single(j, carry):
        mk(_unit_copy, single_loc[s0 + j], single_glob[s0 + j]).start()
        return carry

    lax.fori_loop(0, n_single[tile], issue_single, 0)


def _wait_unit_copies(nun, src_ref, dst_ref, sem):
    def drain_group(j, carry):
        pltpu.make_async_copy(src_ref.at[pl.ds(0, WAIT_GROUP)], dst_ref.at[pl.ds(0, WAIT_GROUP)], sem).wait()
        return carry

    def drain_unit(j, carry):
        _unit_copy(src_ref, 0, dst_ref, 0, sem).wait()
        return carry

    lax.fori_loop(0, nun // WAIT_GROUP, drain_group, 0)
    lax.fori_loop(0, nun % WAIT_GROUP, drain_unit, 0)


def _dispatch_body(tabs, nun_ref, slot_ref, h2_ref, xs_ref, xloc, sem):
    i = pl.program_id(0)
    last = pl.num_programs(0) - 1
    tb = slot_ref.shape[1]
    nun = nun_ref[i]
    s = slot_ref[...]
    h2 = h2_ref[...]
    buf = i % 2

    def sort_chunk(c):
        r = lax.broadcasted_iota(jnp.int32, (SORT_CHUNK, tb), 0) + c * SORT_CHUNK
        hit = (s[0:1] == r) | (s[1:2] == r) | (s[2:3] == r) | (s[3:4] == r)
        p = jnp.where(hit, 1.0, 0.0).astype(BF16)
        units = SORT_CHUNK // ROW_UNIT
        xloc[buf, c * units:(c + 1) * units] = _dot(p, h2).astype(BF16).reshape(units, ROW_UNIT, h2.shape[1])

    typical_rows = TOP_K * tb + N_EXPERTS * ROW_UNIT // 2
    for c in range(LOCAL_ROWS // SORT_CHUNK):
        if (c + 1) * SORT_CHUNK <= typical_rows:
            sort_chunk(c)
        else:
            pl.when(c * (SORT_CHUNK // ROW_UNIT) < nun)(functools.partial(sort_chunk, c))

    _start_run_copies(tabs, i, xloc.at[buf], xs_ref, sem.at[buf], to_global=True)

    @pl.when(i > 0)
    def _():
        _wait_unit_copies(nun_ref[i - 1], xloc.at[1 - buf], xs_ref, sem.at[1 - buf])

    @pl.when(i == last)
    def _():
        _wait_unit_copies(nun, xloc.at[buf], xs_ref, sem.at[buf])


def _dispatch_first_kernel(*refs):
    tabs, (nun_ref, slot_ref, h2_ref, xs_ref, xloc, sem) = refs[:N_RUN_TABLES], refs[N_RUN_TABLES:]
    _dispatch_body(tabs, nun_ref, slot_ref, h2_ref, xs_ref, xloc, sem)


def _dispatch_next_kernel(*refs):
    tabs, (nun_ref, slot_ref, h2_ref, _, xs_ref, xloc, sem) = refs[:N_RUN_TABLES], refs[N_RUN_TABLES:]
    _dispatch_body(tabs, nun_ref, slot_ref, h2_ref, xs_ref, xloc, sem)


def _dispatch(tabs, nun, slot, h2, xs, n_rows, tb):
    T, d = h2.shape
    any_spec = pl.BlockSpec(memory_space=pl.ANY)
    in_specs = [pl.BlockSpec((TOP_K, tb), lambda i, *_: (0, i)),
                pl.BlockSpec((tb, d), lambda i, *_: (i, 0))]
    aliases = {}
    body = _dispatch_first_kernel
    args = tuple(tabs) + (nun, slot, h2)
    if xs is not None:
        in_specs.append(any_spec)
        aliases = {len(args): 0}
        body = _dispatch_next_kernel
        args = args + (xs,)
    return pl.pallas_call(
        body,
        out_shape=jax.ShapeDtypeStruct((n_rows // ROW_UNIT, ROW_UNIT, d), BF16),
        grid_spec=pltpu.PrefetchScalarGridSpec(
            num_scalar_prefetch=N_RUN_TABLES + 1, grid=(T // tb,), in_specs=in_specs, out_specs=any_spec,
            scratch_shapes=[pltpu.VMEM((2, LOCAL_ROWS // ROW_UNIT, ROW_UNIT, d), BF16),
                            pltpu.SemaphoreType.DMA((2,))]),
        input_output_aliases=aliases,
        compiler_params=pltpu.CompilerParams(dimension_semantics=("arbitrary",), has_side_effects=True,
                                             vmem_limit_bytes=VMEM_LIMIT),
        name="dispatch_next" if xs is not None else "dispatch_first",
    )(*args)


def _experts_kernel(te_ref, tv_ref, tf_ref, tn_ref, xs_ref, wg_hbm, bg_ref, wu_hbm, bu_ref, wd_hbm, bd_ref, y_ref,
                    stage_g, stage_u, stage_d, wgb, wub, wdb, sem):
    i = pl.program_id(0)
    valid = tv_ref[i]
    tm, d = xs_ref.shape
    ff = wgb.shape[1]

    def weight_copies(e):
        return [pltpu.make_async_copy(src.at[e], dst, sem.at[j])
                for j, (src, dst) in enumerate(((wg_hbm, stage_g), (wu_hbm, stage_u), (wd_hbm, stage_d)))]

    @pl.when(i == 0)
    def _():
        for cp in weight_copies(te_ref[0]):
            cp.start()

    @pl.when(tf_ref[i] == 1)
    def _():
        for cp in weight_copies(te_ref[i]):
            cp.wait()
        wgb[...] = stage_g[...].astype(BF16)
        wub[...] = stage_u[...].astype(BF16)
        wdb[...] = stage_d[...].astype(BF16)

        @pl.when(tn_ref[i] >= 0)
        def _():
            for cp in weight_copies(tn_ref[i]):
                cp.start()

    @pl.when(valid > 0)
    def _():
        row = lax.broadcasted_iota(jnp.int32, (tm, 1), 0)
        xb = jnp.where(row < valid, xs_ref[...], jnp.zeros((), BF16))
        acc = jnp.zeros((tm, d), F32)
        for c in range(ff // FF_CHUNK):
            cs = slice(c * FF_CHUNK, (c + 1) * FF_CHUNK)
            gate = jnp.minimum(_dot(xb, wgb[:, cs]) + bg_ref[0, :, cs], SWIGLU_LIMIT)
            up = jnp.clip(_dot(xb, wub[:, cs]) + bu_ref[0, :, cs], -SWIGLU_LIMIT, SWIGLU_LIMIT)
            act = (up + 1.0) * gate * jax.nn.sigmoid(SWIGLU_ALPHA * gate)
            acc = acc + _dot(act.astype(BF16), wdb[cs, :])
        y_ref[...] = (acc + bd_ref[0]).astype(BF16)

    @pl.when(valid == 0)
    def _():
        y_ref[...] = jnp.zeros_like(y_ref)


def _experts(tile_expert, tile_valid, tile_first, tile_next, xs, wg, bg, wu, bu, wd, bd, tmg):
    n_tiles = tile_expert.shape[0]
    _, d, ff = wg.shape
    bspec = lambda shp: pl.BlockSpec(shp, lambda i, te, *_: (te[i], 0, 0))
    any_spec = pl.BlockSpec(memory_space=pl.ANY)
    return pl.pallas_call(
        _experts_kernel,
        out_shape=jax.ShapeDtypeStruct(xs.shape, BF16),
        grid_spec=pltpu.PrefetchScalarGridSpec(
            num_scalar_prefetch=4,
            grid=(n_tiles,),
            in_specs=[
                pl.BlockSpec((tmg, d), lambda i, *_: (i, 0)),
                any_spec, bspec((1, 1, ff)),
                any_spec, bspec((1, 1, ff)),
                any_spec, bspec((1, 1, d)),
            ],
            out_specs=pl.BlockSpec((tmg, d), lambda i, *_: (i, 0)),
            scratch_shapes=[pltpu.VMEM((d, ff), F32), pltpu.VMEM((d, ff), F32), pltpu.VMEM((ff, d), F32),
                            pltpu.VMEM((d, ff), BF16), pltpu.VMEM((d, ff), BF16), pltpu.VMEM((ff, d), BF16),
                            pltpu.SemaphoreType.DMA((3,))],
        ),
        compiler_params=pltpu.CompilerParams(dimension_semantics=("arbitrary",), vmem_limit_bytes=EXPERTS_VMEM_LIMIT),
        name="experts",
    )(tile_expert, tile_valid, tile_first, tile_next, xs, wg, bg, wu, bu, wd, bd)


def _combine_kernel(*refs):
    tabs = refs[:N_RUN_TABLES]
    nun_ref, y_ref, route_ref, xmid_ref, g2_ref, nf_ref, o_ref, ybuf, sem = refs[N_RUN_TABLES:]
    sb, rb, d = xmid_ref.shape
    tb = sb * rb
    i = pl.program_id(0) * pl.num_programs(1) + pl.program_id(1)
    n_steps = pl.num_programs(0) * pl.num_programs(1)
    buf = i % 2

    @pl.when(i == 0)
    def _():
        ybuf[...] = jnp.zeros_like(ybuf)
        _start_run_copies(tabs, 0, ybuf.at[0], y_ref, sem.at[0], to_global=False)

    _wait_unit_copies(nun_ref[i], y_ref, ybuf.at[buf], sem.at[buf])

    @pl.when(i + 1 < n_steps)
    def _():
        _start_run_copies(tabs, i + 1, ybuf.at[1 - buf], y_ref, sem.at[1 - buf], to_global=False)

    chunk_units = PERM_CHUNK // ROW_UNIT

    def sorted_rows(c):
        return ybuf[buf, c * chunk_units:(c + 1) * chunk_units].reshape(PERM_CHUNK, d)

    route = route_ref[...]
    lane = lax.broadcasted_iota(jnp.int32, (tb, PERM_CHUNK), 1).astype(F32)
    wk = [jnp.broadcast_to(route[:, k:k + 1], (tb, PERM_CHUNK)) for k in range(TOP_K)]
    sk = [jnp.broadcast_to(route[:, TOP_K + k:TOP_K + k + 1], (tb, PERM_CHUNK)) for k in range(TOP_K)]

    def weights_chunk(c):
        r = lane + float(c * PERM_CHUNK)
        pw = jnp.zeros((tb, PERM_CHUNK), F32)
        for k in range(TOP_K):
            pw = jnp.where(sk[k] == r, wk[k], pw)
        return pw.astype(BF16)

    def contribution(c):
        return _dot(weights_chunk(c), sorted_rows(c))

    moe = contribution(0)
    for c in range(1, LOCAL_ROWS // PERM_CHUNK):
        moe = moe + contribution(c)
    out = xmid_ref[...] + g2_ref[...] * moe.reshape(sb, rb, d)
    ms = jnp.mean(out * out, axis=-1, keepdims=True)
    o_ref[...] = out * lax.rsqrt(ms + EPS) * nf_ref[...]


def _combine(tabs, nun, y, route, xmid, g2, nf, sb, rb):
    nseq, L, d = xmid.shape
    tb = sb * rb
    nt = L // rb
    tok = lambda s, t, *_: (s * nt + t, 0)
    return pl.pallas_call(
        _combine_kernel,
        out_shape=jax.ShapeDtypeStruct((nseq, L, d), F32),
        grid_spec=pltpu.PrefetchScalarGridSpec(
            num_scalar_prefetch=N_RUN_TABLES + 1,
            grid=(nseq // sb, nt),
            in_specs=[
                pl.BlockSpec(memory_space=pl.ANY),
                pl.BlockSpec((tb, 2 * TOP_K), tok),
                pl.BlockSpec((sb, rb, d), lambda s, t, *_: (s, t, 0)),
                pl.BlockSpec((sb, 1, d), lambda s, t, *_: (s, 0, 0)),
                pl.BlockSpec((1, d), lambda s, t, *_: (0, 0)),
            ],
            out_specs=pl.BlockSpec((sb, rb, d), lambda s, t, *_: (s, t, 0)),
            scratch_shapes=[pltpu.VMEM((2, LOCAL_ROWS // ROW_UNIT, ROW_UNIT, d), BF16),
                            pltpu.SemaphoreType.DMA((2,))],
        ),
        compiler_params=_cparams("arbitrary", "arbitrary"),
        name="combine",
    )(*tabs, nun, y, route, xmid, g2, nf)


def _tile_rows(nseq, L, tile):
    if L >= tile:
        assert L % tile == 0
        return 1, tile
    assert tile % L == 0 and nseq % (tile // L) == 0
    return tile // L, L


def kernel(x_prompt, x_sample, state_gla, c_prompt, c_sample, w_ada, b_ada, norm1, w_in, w_gk, b_gk, gla_norm,
           gmlp_ln_g, gmlp_ln_b, gmlp_w_s, gmlp_b_s, w_out, norm2, w_router, b_router, w_gate, b_gate, w_up,
           b_up, w_down, b_down, norm_f):
    depth = w_ada.shape[0]
    assert depth == 1
    bp, lp, d = x_prompt.shape
    bs, ls, _ = x_sample.shape
    tp, ts = bp * lp, bs * ls

    nc = bp + bs
    ncp = -(-nc // SUBLANE) * SUBLANE
    c_all = jnp.concatenate([c_prompt, c_sample, jnp.zeros((ncp - nc, d), F32)], axis=0)
    mod = _ada(c_all, w_ada[0], b_ada[0][None]).reshape(ncp, N_MOD, 1, d)
    mods_p = [mod[:bp, i] for i in range(N_MOD)]
    mods_s = [mod[bp:nc, i] for i in range(N_MOD)]

    wi = w_in[0]
    c_lr = 2 * GLA_QK + GLA_WIDTH
    c_r = c_lr + GLA_LOWRANK
    wm = jnp.concatenate([wi[:, :c_lr], wi[:, c_r:]], axis=1).astype(BF16)
    wlr = jnp.pad(wi[:, c_lr:c_r], ((0, 0), (0, LANE - GLA_LOWRANK))).astype(BF16)
    wgk_f = jnp.pad(w_gk[0], ((0, LANE - GLA_LOWRANK), (0, 0)))
    wgk_hi = wgk_f.astype(BF16)
    wgk = jnp.concatenate([wgk_hi, (wgk_f - wgk_hi.astype(F32)).astype(BF16)], axis=1)
    bgk = b_gk[0][None]
    n1, n2, nf = norm1[0][None], norm2[0][None], norm_f[None]
    gn, lng, lnb = gla_norm[0][None], gmlp_ln_g[0][None], gmlp_ln_b[0][None]
    ws, bsv = gmlp_w_s[0], gmlp_b_s[0]
    pos_i = jnp.arange(GMLP_BLOCK)
    cmask = (pos_i[None, :] // CHUNK) <= (pos_i[:, None] // CHUNK)
    wm_p = jnp.where(cmask[None], ws, 0.0).astype(BF16)
    bsb_p = jnp.repeat(bsv.T, GMLP_GC, axis=1)
    reps = ROW_BLOCK // ls
    eye = jnp.eye(reps, dtype=F32)
    wm_s = jnp.einsum("ab,gij->gaibj", eye, ws[:, :ls, :ls]).reshape(GMLP_GROUPS, ROW_BLOCK, ROW_BLOCK).astype(BF16)
    bsb_s = jnp.tile(jnp.repeat(bsv[:, :ls].T, GMLP_GC, axis=1), (reps, 1))
    wo = w_out[0].astype(BF16)
    wr_f = jnp.pad(w_router[0], ((0, 0), (0, LANE - N_EXPERTS)))
    wr_hi = wr_f.astype(BF16)
    wr = jnp.concatenate([wr_hi, (wr_f - wr_hi.astype(F32)).astype(BF16)], axis=1)
    br = jnp.concatenate([b_router[0], jnp.full((LANE - N_EXPERTS,), -1e30, F32)])[None]
    upper = (jnp.arange(TOKEN_TILE)[:, None] < jnp.arange(TOKEN_TILE)[None, :]).astype(BF16)
    lower = (jnp.arange(N_EXPERTS)[None, :] < jnp.arange(N_EXPERTS)[:, None]).astype(F32)
    wg, wu, wd = w_gate[0], w_up[0], w_down[0]
    bg, bu, bd = b_gate[0][:, None], b_up[0][:, None], b_down[0][:, None]

    sbp, rbp = _tile_rows(bp, lp, TOKEN_TILE)
    sbs, rbs = _tile_rows(bs, ls, TOKEN_TILE)

    assert lp % TOKEN_TILE == 0
    mix_p, state_p = _mixer_prompt(x_prompt, mods_p[1], mods_p[0], n1, wm, wlr, wgk, bgk, gn, lng, lnb, wm_p, bsb_p,
                                   TOKEN_TILE)
    proj_s, gk_s = _inproj(x_sample, mods_s[1], mods_s[0], n1, wm, wlr, wgk, bgk, sbs, rbs)
    mix_s, state_s, vn_s = _mixer_sample(proj_s, gk_s, state_gla[0], gn, lng, lnb, wm_s, bsb_s)

    xmid_p, h2_p, route_p, slot_p, n16_p = _outproj(
        mix_p, x_prompt, mods_p[2], mods_p[4], mods_p[3], n2, wo, wr, br, upper, lower, sbp, rbp)
    xmid_s, h2_s, route_s, slot_s, n16_s = _outproj(
        mix_s, x_sample, mods_s[2], mods_s[4], mods_s[3], n2, wo, wr, br, upper, lower, sbs, rbs)

    tmg = EXPERT_TILE
    ntp = tp // TOKEN_TILE
    eids = jnp.arange(N_EXPERTS, dtype=jnp.int32)
    n16 = jnp.concatenate([n16_p[:, :, 0], n16_s[:, :, 0]], axis=0).astype(jnp.int32)
    nt_all = n16.shape[0]
    earlier = jnp.cumsum(n16, axis=0) - n16
    tot = jnp.sum(n16, axis=0)
    tiles_e = (tot + tmg - 1) // tmg
    tile_end = jnp.cumsum(tiles_e)
    tile_start = tile_end - tiles_e
    row_start = tile_start * tmg
    n_tiles = (TOP_K * (tp + ts) + nt_all * N_EXPERTS * (ROW_UNIT - 1)) // tmg + N_EXPERTS
    tid = jnp.arange(n_tiles, dtype=jnp.int32)
    te = jnp.minimum(jnp.sum((tid[:, None] >= tile_end[None, :]).astype(jnp.int32), axis=1), N_EXPERTS - 1)
    te_hot = te[:, None] == eids[None, :]
    tot_te = jnp.sum(jnp.where(te_hot, tot[None, :], 0), axis=1)
    start_te = jnp.sum(jnp.where(te_hot, tile_start[None, :], 0), axis=1)
    active = tid < tile_end[-1]
    tv = jnp.where(active, jnp.clip(tot_te - (tid - start_te) * tmg, 0, tmg), 0).astype(jnp.int32)
    last_e = jnp.max(jnp.where(tiles_e > 0, eids, 0)).astype(jnp.int32)
    te = jnp.where(active, te, last_e).astype(jnp.int32)
    te_prev = jnp.concatenate([jnp.full((1,), -1, jnp.int32), te[:-1]])
    tf = (active & (te != te_prev)).astype(jnp.int32)
    later_used = (eids[None, :] > te[:, None]) & (tiles_e[None, :] > 0)
    tn = jnp.min(jnp.where(later_used, eids[None, :], N_EXPERTS), axis=1)
    tn = jnp.where(tn < N_EXPERTS, tn, -1).astype(jnp.int32)

    run_units = n16 // ROW_UNIT
    run_loc = (jnp.cumsum(n16, axis=1) - n16) // ROW_UNIT
    run_glob = (row_start[None, :] + earlier) // ROW_UNIT
    nun = jnp.sum(run_units, axis=1).astype(jnp.int32)

    def piece_tables(cnt, loc0, glob0, stride, max_pieces):
        end = jnp.cumsum(cnt, axis=1)
        start = end - cnt
        piece = jnp.arange(max_pieces, dtype=jnp.int32)
        run = jnp.sum((piece[None, :, None] >= end[:, None, :]).astype(jnp.int32), axis=2)
        hot = run[:, :, None] == eids[None, None, :]
        pick = lambda a: jnp.sum(jnp.where(hot, a[:, None, :], 0), axis=2)
        within = (piece[None, :] - pick(start)) * stride
        live = piece[None, :] < end[:, -1:]
        loc = jnp.where(live, pick(loc0) + within, 0).astype(jnp.int32)
        glob = jnp.where(live, pick(glob0) + within, 0).astype(jnp.int32)
        return loc, glob, end[:, -1].astype(jnp.int32)

    pair_loc, pair_glob, n_pair = piece_tables(run_units // 2, run_loc, run_glob, 2, MAX_PAIRS)
    odd = run_units % 2
    single_loc, single_glob, n_single = piece_tables(odd, run_loc + run_units - 1, run_glob + run_units - 1, 0,
                                                     N_EXPERTS)

    def tables_for(tiles):
        flat = lambda a: a[tiles].reshape(-1)
        return (flat(pair_loc), flat(pair_glob), n_pair[tiles], flat(single_loc), flat(single_glob), n_single[tiles])

    tabs_p, tabs_s = tables_for(slice(0, ntp)), tables_for(slice(ntp, nt_all))

    n_rows = n_tiles * tmg
    xs = _dispatch(tabs_p, nun[:ntp], slot_p, h2_p, None, n_rows, TOKEN_TILE)
    xs = _dispatch(tabs_s, nun[ntp:], slot_s, h2_s, xs, n_rows, TOKEN_TILE)
    y = _experts(te, tv, tf, tn, xs.reshape(n_rows, d), wg, bg, wu, bu, wd, bd, tmg)
    y = y.reshape(n_rows // ROW_UNIT, ROW_UNIT, d)
    y_prompt = _combine(tabs_p, nun[:ntp], y, route_p, xmid_p, mods_p[5], nf, sbp, rbp)
    y_sample = _combine(tabs_s, nun[ntp:], y, route_s, xmid_s, mods_s[5], nf, sbs, rbs)

    return (y_prompt, y_sample, state_p[None], state_s[None], vn_s[None])
```

```python
import functools

import jax
import jax.numpy as jnp
from jax import lax
from jax.experimental import pallas as pl
from jax.experimental.pallas import tpu as pltpu

F32 = jnp.float32
BF16 = jnp.bfloat16
HIGHEST = lax.Precision.HIGHEST

CHUNK = 64
GLA_HEADS = 4
GLA_DK = 64
GLA_DV = 128
GLA_QK = GLA_HEADS * GLA_DK
GLA_WIDTH = GLA_HEADS * GLA_DV
GLA_LOWRANK = 16
GLA_GATE_NORM = 16.0
GATE_SAFE_MIN = -60.0
GMLP_WIDTH = 512
GMLP_GROUPS = 4
GMLP_GC = GMLP_WIDTH // GMLP_GROUPS
GMLP_BLOCK = 128
N_EXPERTS = 32
TOP_K = 4
SWIGLU_LIMIT = 7.0
SWIGLU_ALPHA = 1.702
EPS = 1e-6
N_MOD = 6

LANE = 128
SUBLANE = 8

Q0 = 0
K0 = Q0 + GLA_QK
V0 = K0 + GLA_QK
R0 = V0 + GLA_WIDTH
U0 = R0 + GLA_WIDTH
G0 = U0 + GMLP_WIDTH
PROJ_COLS = G0 + GMLP_WIDTH

ROW_BLOCK = 128
TOKEN_TILE = 512
EXPERT_TILE = 512
FF_CHUNK = 512
ROW_UNIT = 16
LOCAL_ROWS = 2560
WAIT_GROUP = 8
PIECE_SIZES = (4, 2, 1)
N_RUN_TABLES = 3 * len(PIECE_SIZES)
SORT_CHUNK = 128
PERM_CHUNK = 256
VMEM_LIMIT = 48 * 1024 * 1024
EXPERTS_VMEM_LIMIT = 56 * 1024 * 1024


def _cparams(*sem):
    return pltpu.CompilerParams(dimension_semantics=sem, vmem_limit_bytes=VMEM_LIMIT)


def _dot(a, b, **kw):
    return jnp.dot(a, b, preferred_element_type=F32, **kw)


def _dot_nt(a, b):
    return lax.dot_general(a, b, (((1,), (1,)), ((), ())), preferred_element_type=F32)


def _gelu(x):
    return 0.5 * x * (1.0 + lax.erf(x * (0.5 ** 0.5)))


def _dot_tn(a, b):
    return lax.dot_general(a, b, (((0,), (0,)), ((), ())), preferred_element_type=F32)


def _ada_kernel(c_ref, w_ref, b_ref, o_ref):
    c = c_ref[...]
    s = c * jax.nn.sigmoid(c)
    o_ref[...] = _dot(s, w_ref[...], precision=HIGHEST) + b_ref[...]


def _ada(c, w_ada, b_ada):
    n, d = c.shape
    return pl.pallas_call(
        _ada_kernel,
        out_shape=jax.ShapeDtypeStruct((n, N_MOD * d), F32),
        grid=(N_MOD,),
        in_specs=[
            pl.BlockSpec((n, d), lambda j: (0, 0)),
            pl.BlockSpec((d, d), lambda j: (0, j)),
            pl.BlockSpec((1, d), lambda j: (0, j)),
        ],
        out_specs=pl.BlockSpec((n, d), lambda j: (0, j)),
        compiler_params=_cparams("arbitrary"),
        name="ada",
    )(c, w_ada, b_ada)


PROJ_COL_CHUNK = 512


def _modulated_norm(x_ref, sc_ref, sh_ref, n1_ref):
    sb, rb, d = x_ref.shape
    x = x_ref[...]
    ms = jnp.mean(x * x, axis=-1, keepdims=True)
    h = x * lax.rsqrt(ms + EPS) * n1_ref[...]
    h = h * (1.0 + sc_ref[...]) + sh_ref[...]
    return h.reshape(sb * rb, d).astype(BF16)


def _gate_preact(hb, wlr_ref, wgk_ref, bgk_ref):
    lr = _dot(hb, wlr_ref[...])
    lr_hi = lr.astype(BF16)
    lr_lo = (lr - lr_hi.astype(F32)).astype(BF16)
    part = _dot(lr_hi, wgk_ref[...])
    return part[:, :GLA_QK] + part[:, GLA_QK:] + _dot(lr_lo, wgk_ref[:, :GLA_QK]) + bgk_ref[...]


def _inproj_kernel(x_ref, sc_ref, sh_ref, n1_ref, wm_ref, wlr_ref, wgk_ref, bgk_ref, proj_ref, gk_ref):
    sb, rb, _ = x_ref.shape
    hb = _modulated_norm(x_ref, sc_ref, sh_ref, n1_ref)
    cw = PROJ_COL_CHUNK
    for c in range(PROJ_COLS // cw):
        p = _dot(hb, wm_ref[:, c * cw:(c + 1) * cw])
        proj_ref[:, :, c * cw:(c + 1) * cw] = p.astype(BF16).reshape(sb, rb, cw)
    gk_ref[...] = _gate_preact(hb, wlr_ref, wgk_ref, bgk_ref).reshape(sb, rb, GLA_QK)


def _inproj(x, sc, sh, n1, wm, wlr, wgk, bgk, sb, rb):
    nseq, L, d = x.shape
    grid = (nseq // sb, L // rb)
    const = lambda s, t: (0, 0)
    return pl.pallas_call(
        _inproj_kernel,
        out_shape=(jax.ShapeDtypeStruct((nseq, L, PROJ_COLS), BF16),
                   jax.ShapeDtypeStruct((nseq, L, GLA_QK), F32)),
        grid=grid,
        in_specs=[
            pl.BlockSpec((sb, rb, d), lambda s, t: (s, t, 0)),
            pl.BlockSpec((sb, 1, d), lambda s, t: (s, 0, 0)),
            pl.BlockSpec((sb, 1, d), lambda s, t: (s, 0, 0)),
            pl.BlockSpec((1, d), const),
            pl.BlockSpec((d, PROJ_COLS), const),
            pl.BlockSpec((d, LANE), const),
            pl.BlockSpec((LANE, 2 * GLA_QK), const),
            pl.BlockSpec((1, GLA_QK), const),
        ],
        out_specs=(pl.BlockSpec((sb, rb, PROJ_COLS), lambda s, t: (s, t, 0)),
                   pl.BlockSpec((sb, rb, GLA_QK), lambda s, t: (s, t, 0))),
        compiler_params=_cparams("arbitrary", "arbitrary"),
        name="inproj",
    )(x, sc, sh, n1, wm, wlr, wgk, bgk)


def _head_masks():
    lane = lax.broadcasted_iota(jnp.int32, (1, GLA_QK), 1)
    return [(lane // GLA_DK) == h for h in range(GLA_HEADS)]


def _stack_heads(x, hm):
    return jnp.concatenate([jnp.where(m, x, 0.0) for m in hm], axis=0).astype(BF16)


def _chunk_pair_ids(c_len):
    ti = lax.broadcasted_iota(jnp.int32, (GLA_HEADS * c_len, c_len), 0) % c_len
    si = lax.broadcasted_iota(jnp.int32, (GLA_HEADS * c_len, c_len), 1)
    return ti, si


def _scores_factored(q4s, k, G, c_len):
    ti, si = _chunk_pair_ids(c_len)
    ke = (k * jnp.exp(-G)).astype(BF16)
    return jnp.concatenate(
        [jnp.where(si <= ti, _dot_nt(q4, ke[c * c_len:(c + 1) * c_len]), 0.0) for c, q4 in enumerate(q4s)], axis=0)


def _scores_bounded(qs, k, G, c_len, hm):
    ti, si = _chunk_pair_ids(c_len)
    t = lax.broadcasted_iota(jnp.int32, (c_len, 1), 0)
    col = lax.broadcasted_iota(jnp.int32, (c_len, c_len), 1)
    outs = []
    for c in range(qs.shape[0] // c_len):
        sl = slice(c * c_len, (c + 1) * c_len)
        q_c, k_c, g_c = qs[sl], k[sl], G[sl]
        a = jnp.where(si == ti, _dot_nt(_stack_heads(q_c, hm), k_c.astype(BF16)), 0.0)
        half = c_len // 2
        while half >= 1:
            blk = 2 * half
            sel = (col == (t // blk) * blk + (half - 1)).astype(F32)
            ref = _dot(sel, g_c, precision=HIGHEST)
            upper = (t % blk) >= half
            qh = jnp.where(upper, q_c * jnp.exp(jnp.minimum(g_c - ref, 0.0)), 0.0)
            kh = jnp.where(upper, 0.0, k_c * jnp.exp(jnp.minimum(ref - g_c, 0.0)))
            same = (ti // blk) == (si // blk)
            a = a + jnp.where(same, _dot_nt(_stack_heads(qh, hm), kh.astype(BF16)), 0.0)
            half //= 2
        outs.append(a)
    return jnp.concatenate(outs, axis=0)


def _cum_log_gates(gkpre, c_len):
    rows = gkpre.shape[0]
    g = jax.nn.log_sigmoid(gkpre) / GLA_GATE_NORM
    ri = lax.broadcasted_iota(jnp.int32, (rows, rows), 0)
    ci = lax.broadcasted_iota(jnp.int32, (rows, rows), 1)
    tri = jnp.where((ci <= ri) & ((ci // c_len) == (ri // c_len)), 1.0, 0.0).astype(BF16)
    g_hi = g.astype(BF16)
    r1 = g - g_hi.astype(F32)
    g_mid = r1.astype(BF16)
    g_lo = (r1 - g_mid.astype(F32)).astype(BF16)
    parts = _dot(tri, jnp.concatenate([g_hi, g_mid, g_lo], axis=1))
    return parts[:, :GLA_QK] + parts[:, GLA_QK:2 * GLA_QK] + parts[:, 2 * GLA_QK:]


def _factored_is_safe(G, c_len):
    ends = [G[(c + 1) * c_len - 1:(c + 1) * c_len] for c in range(G.shape[0] // c_len)]
    return jnp.min(jnp.concatenate(ends, axis=0)) > GATE_SAFE_MIN


def _mixer_block(p, G, states, c_len, gn, lng, lnb, wm_ref, bsb, bounded, side=()):
    rows = p.shape[0]
    n_chunks = rows // c_len
    chained = len(states) == 1
    hm = _head_masks()
    side = list(side)

    def side_step():
        if side:
            side.pop(0)()

    q = p[:, Q0:Q0 + GLA_QK].astype(F32)
    k = p[:, K0:K0 + GLA_QK].astype(F32)
    v = p[:, V0:V0 + GLA_WIDTH]
    r = p[:, R0:R0 + GLA_WIDTH].astype(F32)

    qs = q * (GLA_DK ** -0.5)
    qe = qs * jnp.exp(G)
    q4s = [_stack_heads(qe[c * c_len:(c + 1) * c_len], hm) for c in range(n_chunks)]
    scores = _scores_bounded(qs, k, G, c_len, hm) if bounded else _scores_factored(q4s, k, G, c_len)
    hc = GLA_HEADS * c_len

    new_states = []
    o_rows = []
    st = states[0]
    for c in range(n_chunks):
        lo, hi = c * c_len, (c + 1) * c_len
        if not chained:
            st = states[c]
        q4 = q4s[c]
        a = scores[c * hc:(c + 1) * hc].astype(BF16)
        o_inter = _dot_nt(q4, st.astype(BF16))
        v_c = v[lo:hi]
        heads = []
        for h in range(GLA_HEADS):
            o_h = o_inter[h * c_len:(h + 1) * c_len] + _dot(
                a[h * c_len:(h + 1) * c_len], v_c[:, h * GLA_DV:(h + 1) * GLA_DV])
            heads.append(o_h)
        o_rows.append(jnp.concatenate(heads, axis=1))
        g_last = G[hi - 1:hi]
        kd = (k[lo:hi] * jnp.exp(g_last - G[lo:hi])).astype(BF16)
        upd = _dot_tn(v_c, kd)
        st_new = jnp.exp(g_last) * st
        for h in range(GLA_HEADS):
            st_new = st_new + jnp.where(hm[h], upd[h * GLA_DV:(h + 1) * GLA_DV], 0.0)
        if chained:
            st = st_new
        else:
            new_states.append(st_new)
        side_step()
    if chained:
        new_states = [st]
    o = jnp.concatenate(o_rows, axis=0)

    gla = []
    for h in range(GLA_HEADS):
        o_h = o[:, h * GLA_DV:(h + 1) * GLA_DV]
        ms = jnp.mean(o_h * o_h, axis=-1, keepdims=True)
        r_h = r[:, h * GLA_DV:(h + 1) * GLA_DV]
        gla.append(o_h * lax.rsqrt(ms + EPS) * gn * (r_h * jax.nn.sigmoid(r_h)))
    side_step()

    u = _gelu(p[:, U0:U0 + GMLP_WIDTH].astype(F32))
    side_step()
    vv = _gelu(p[:, G0:G0 + GMLP_WIDTH].astype(F32))
    side_step()
    mu = jnp.mean(vv, axis=-1, keepdims=True)
    xc = vv - mu
    var = jnp.mean(xc * xc, axis=-1, keepdims=True)
    vn = xc * lax.rsqrt(var + EPS) * lng + lnb
    vnb = vn.astype(BF16)
    gm = []
    for gi in range(GMLP_GROUPS):
        sl = slice(gi * GMLP_GC, (gi + 1) * GMLP_GC)
        mixed = _dot(wm_ref[gi], vnb[:, sl]) + bsb[:, sl]
        gm.append(u[:, sl] * mixed)
    out = jnp.concatenate(gla + gm, axis=1)
    while side:
        side_step()
    return out, new_states, vn


def _mixer_prompt_kernel(x_ref, sc_ref, sh_ref, n1_ref, wm_ref, wlr_ref, wgk_ref, bgk_ref,
                         gn_ref, lng_ref, lnb_ref, ws_ref, bsb_ref, mix_ref, s_ref, st_scr):
    t = pl.program_id(1)
    nt = pl.num_programs(1)

    @pl.when(t == 0)
    def _():
        st_scr[...] = jnp.zeros_like(st_scr)

    tb = x_ref.shape[1]
    n_sub = tb // ROW_BLOCK
    cw = PROJ_COL_CHUNK
    hb = _modulated_norm(x_ref, sc_ref, sh_ref, n1_ref)

    def proj_steps(j, parts):
        hb_j = hb[j * ROW_BLOCK:(j + 1) * ROW_BLOCK]
        return [functools.partial(
            lambda c: parts.append(_dot(hb_j, wm_ref[:, c * cw:(c + 1) * cw]).astype(BF16)), c)
            for c in range(PROJ_COLS // cw)]

    gk = _gate_preact(hb, wlr_ref, wgk_ref, bgk_ref)
    first = []
    for step in proj_steps(0, first):
        step()
    Gs = [_cum_log_gates(gk[j * ROW_BLOCK:(j + 1) * ROW_BLOCK], CHUNK) for j in range(n_sub)]
    safe = _factored_is_safe(jnp.concatenate(Gs, axis=0), CHUNK)

    def run(bounded):
        st = st_scr[...]
        parts = first
        for j in range(n_sub):
            p_j = jnp.concatenate(parts, axis=1)
            parts = []
            side = proj_steps(j + 1, parts) if j + 1 < n_sub else []
            out, sts, _ = _mixer_block(p_j, Gs[j], [st], CHUNK, gn_ref[...], lng_ref[...], lnb_ref[...], ws_ref,
                                       bsb_ref[...], bounded, side)
            st = sts[0]
            mix_ref[0, j * ROW_BLOCK:(j + 1) * ROW_BLOCK, :] = out.astype(BF16)
        st_scr[...] = st

    pl.when(safe)(functools.partial(run, False))
    pl.when(jnp.logical_not(safe))(functools.partial(run, True))

    @pl.when(t == nt - 1)
    def _():
        s_ref[0] = st_scr[...].T.reshape(GLA_HEADS, GLA_DK, GLA_DV)


def _mixer_prompt(x, sc, sh, n1, wm, wlr, wgk, bgk, gn, lng, lnb, ws, bsb, tb):
    b, L, d = x.shape
    const2 = lambda s, t: (0, 0)
    return pl.pallas_call(
        _mixer_prompt_kernel,
        out_shape=(jax.ShapeDtypeStruct((b, L, 2 * GLA_WIDTH), BF16),
                   jax.ShapeDtypeStruct((b, GLA_HEADS, GLA_DK, GLA_DV), F32)),
        grid=(b, L // tb),
        in_specs=[
            pl.BlockSpec((1, tb, d), lambda s, t: (s, t, 0)),
            pl.BlockSpec((1, 1, d), lambda s, t: (s, 0, 0)),
            pl.BlockSpec((1, 1, d), lambda s, t: (s, 0, 0)),
            pl.BlockSpec((1, d), const2),
            pl.BlockSpec((d, PROJ_COLS), const2),
            pl.BlockSpec((d, LANE), const2),
            pl.BlockSpec((LANE, 2 * GLA_QK), const2),
            pl.BlockSpec((1, GLA_QK), const2),
            pl.BlockSpec((1, GLA_DV), const2),
            pl.BlockSpec((1, GMLP_WIDTH), const2),
            pl.BlockSpec((1, GMLP_WIDTH), const2),
            pl.BlockSpec((GMLP_GROUPS, ROW_BLOCK, ROW_BLOCK), lambda s, t: (0, 0, 0)),
            pl.BlockSpec((ROW_BLOCK, GMLP_WIDTH), const2),
        ],
        out_specs=(pl.BlockSpec((1, tb, 2 * GLA_WIDTH), lambda s, t: (s, t, 0)),
                   pl.BlockSpec((1, GLA_HEADS, GLA_DK, GLA_DV), lambda s, t: (s, 0, 0, 0))),
        scratch_shapes=[pltpu.VMEM((GLA_DV, GLA_QK), F32)],
        compiler_params=_cparams("arbitrary", "arbitrary"),
        name="mixer_prompt",
    )(x, sc, sh, n1, wm, wlr, wgk, bgk, gn, lng, lnb, ws, bsb)


def _mixer_sample_kernel(proj_ref, gk_ref, s0_ref, gn_ref, lng_ref, lnb_ref, wm_ref, bsb_ref,
                         mix_ref, s_ref, vn_ref):
    sb, rb, _ = proj_ref.shape
    G = _cum_log_gates(gk_ref[...].reshape(sb * rb, GLA_QK), rb)
    safe = _factored_is_safe(G, rb)

    def run(bounded):
        p = proj_ref[...].reshape(sb * rb, PROJ_COLS)
        states = [s0_ref[i].reshape(GLA_QK, GLA_DV).T for i in range(sb)]
        out, sts, vn = _mixer_block(p, G, states, rb, gn_ref[...], lng_ref[...], lnb_ref[...], wm_ref,
                                    bsb_ref[...], bounded)
        mix_ref[...] = out.astype(BF16).reshape(sb, rb, 2 * GLA_WIDTH)
        vn_ref[...] = vn.reshape(sb, rb, GMLP_WIDTH)
        for i in range(sb):
            s_ref[i] = sts[i].T.reshape(GLA_HEADS, GLA_DK, GLA_DV)

    pl.when(safe)(functools.partial(run, False))
    pl.when(jnp.logical_not(safe))(functools.partial(run, True))


def _mixer_sample(proj, gk, s0, gn, lng, lnb, wm, bsb):
    n, L, _ = proj.shape
    sb = ROW_BLOCK // L
    const2 = lambda s: (0, 0)
    return pl.pallas_call(
        _mixer_sample_kernel,
        out_shape=(jax.ShapeDtypeStruct((n, L, 2 * GLA_WIDTH), BF16),
                   jax.ShapeDtypeStruct((n, GLA_HEADS, GLA_DK, GLA_DV), F32),
                   jax.ShapeDtypeStruct((n, L, GMLP_WIDTH), F32)),
        grid=(n // sb,),
        in_specs=[
            pl.BlockSpec((sb, L, PROJ_COLS), lambda s: (s, 0, 0)),
            pl.BlockSpec((sb, L, GLA_QK), lambda s: (s, 0, 0)),
            pl.BlockSpec((sb, GLA_HEADS, GLA_DK, GLA_DV), lambda s: (s, 0, 0, 0)),
            pl.BlockSpec((1, GLA_DV), const2),
            pl.BlockSpec((1, GMLP_WIDTH), const2),
            pl.BlockSpec((1, GMLP_WIDTH), const2),
            pl.BlockSpec((GMLP_GROUPS, ROW_BLOCK, ROW_BLOCK), lambda s: (0, 0, 0)),
            pl.BlockSpec((ROW_BLOCK, GMLP_WIDTH), const2),
        ],
        out_specs=(pl.BlockSpec((sb, L, 2 * GLA_WIDTH), lambda s: (s, 0, 0)),
                   pl.BlockSpec((sb, GLA_HEADS, GLA_DK, GLA_DV), lambda s: (s, 0, 0, 0)),
                   pl.BlockSpec((sb, L, GMLP_WIDTH), lambda s: (s, 0, 0))),
        compiler_params=_cparams("arbitrary"),
        name="mixer_sample",
    )(proj, gk, s0, gn, lng, lnb, wm, bsb)


def _outproj_kernel(mix_ref, x_ref, g1_ref, sc_ref, sh_ref, n2_ref, wo_ref, wr_ref, br_ref, upper_ref, lower_ref,
                    xmid_ref, h2_ref, route_ref, slot_ref, n16_ref):
    sb, rb, d = x_ref.shape
    tm = sb * rb

    y = _dot(mix_ref[...].reshape(tm, d), wo_ref[...])
    xm = x_ref[...] + g1_ref[...] * y.reshape(sb, rb, d)
    xmid_ref[...] = xm
    ms = jnp.mean(xm * xm, axis=-1, keepdims=True)
    h2 = xm * lax.rsqrt(ms + EPS) * n2_ref[...]
    h2 = (h2 * (1.0 + sc_ref[...]) + sh_ref[...]).reshape(tm, d)
    h_hi = h2.astype(BF16)
    h2_ref[...] = h_hi

    h_lo = (h2 - h_hi.astype(F32)).astype(BF16)
    part = _dot(h_hi, wr_ref[...])
    logits = part[:, :LANE] + part[:, LANE:] + _dot(h_lo, wr_ref[:, :LANE]) + br_ref[...]
    l = logits.T[:N_EXPERTS]
    eid = lax.broadcasted_iota(jnp.int32, (N_EXPERTS, tm), 0).astype(F32)
    top_l, sel = [], []
    for _ in range(TOP_K):
        m = jnp.max(l, axis=0, keepdims=True)
        idx = jnp.min(jnp.where(l == m, eid, float(N_EXPERTS)), axis=0, keepdims=True)
        hit = eid == idx
        top_l.append(m)
        sel.append(hit)
        l = jnp.where(hit, -jnp.inf, l)
    ex = [jnp.exp(t - top_l[0]) for t in top_l]
    den = ex[0] + ex[1] + ex[2] + ex[3]
    top_w = [e / den for e in ex]

    chosen = (sel[0] | sel[1] | sel[2] | sel[3])
    cb = jnp.where(chosen, 1.0, 0.0)
    before = _dot(cb.astype(BF16), upper_ref[...])
    n = jnp.sum(cb, axis=1, keepdims=True)
    n16 = jnp.floor((n + (ROW_UNIT - 1)) * (1.0 / ROW_UNIT)) * ROW_UNIT
    n16b = jnp.broadcast_to(n16, (N_EXPERTS, LANE))
    ls = _dot(lower_ref[...], n16b, precision=HIGHEST)[:, 0:1]
    base = before + ls
    slots = [jnp.sum(jnp.where(s, base, 0.0), axis=0, keepdims=True) for s in sel]
    slot_ref[...] = jnp.concatenate(slots, axis=0).astype(jnp.int32)
    n16_ref[0] = n16b
    rows = jnp.concatenate(top_w + slots + [jnp.zeros((LANE - 2 * TOP_K, tm), F32)], axis=0)
    route_ref[...] = rows.T[:, :2 * TOP_K]


def _outproj(mix, x, g1, sc2, sh2, n2, wo, wr, br, upper, lower, sb, rb):
    nseq, L, d = x.shape
    tm = sb * rb
    nt = L // rb
    T = nseq * L
    grid = (nseq // sb, nt)
    const = lambda s, t: (0, 0)
    tok = lambda s, t: (0, s * nt + t)
    return pl.pallas_call(
        _outproj_kernel,
        out_shape=(jax.ShapeDtypeStruct((nseq, L, d), F32),
                   jax.ShapeDtypeStruct((T, d), BF16),
                   jax.ShapeDtypeStruct((T, 2 * TOP_K), F32),
                   jax.ShapeDtypeStruct((TOP_K, T), jnp.int32),
                   jax.ShapeDtypeStruct((T // tm, N_EXPERTS, LANE), F32)),
        grid=grid,
        in_specs=[
            pl.BlockSpec((sb, rb, d), lambda s, t: (s, t, 0)),
            pl.BlockSpec((sb, rb, d), lambda s, t: (s, t, 0)),
            pl.BlockSpec((sb, 1, d), lambda s, t: (s, 0, 0)),
            pl.BlockSpec((sb, 1, d), lambda s, t: (s, 0, 0)),
            pl.BlockSpec((sb, 1, d), lambda s, t: (s, 0, 0)),
            pl.BlockSpec((1, d), const),
            pl.BlockSpec((d, d), const),
            pl.BlockSpec((d, 2 * LANE), const),
            pl.BlockSpec((1, LANE), const),
            pl.BlockSpec((tm, tm), const),
            pl.BlockSpec((N_EXPERTS, N_EXPERTS), const),
        ],
        out_specs=(pl.BlockSpec((sb, rb, d), lambda s, t: (s, t, 0)),
                   pl.BlockSpec((tm, d), lambda s, t: (s * nt + t, 0)),
                   pl.BlockSpec((tm, 2 * TOP_K), lambda s, t: (s * nt + t, 0)),
                   pl.BlockSpec((TOP_K, tm), tok),
                   pl.BlockSpec((1, N_EXPERTS, LANE), lambda s, t: (s * nt + t, 0, 0))),
        compiler_params=_cparams("arbitrary", "arbitrary"),
        name="outproj",
    )(mix, x, g1, sc2, sh2, n2, wo, wr, br, upper, lower)


def _unit_copy(src_ref, src_unit, dst_ref, dst_unit, sem):
    return pltpu.make_async_copy(src_ref.at[src_unit], dst_ref.at[dst_unit], sem)


def _piece_copy(size, src_ref, src_unit, dst_ref, dst_unit, sem):
    return pltpu.make_async_copy(src_ref.at[pl.ds(src_unit, size)], dst_ref.at[pl.ds(dst_unit, size)], sem)


def _max_pieces(size):
    return LOCAL_ROWS // (ROW_UNIT * size) if size == PIECE_SIZES[0] else N_EXPERTS


def _start_run_copies(tabs, tile, loc_ref, glob_ref, sem, to_global):
    for ci, size in enumerate(PIECE_SIZES):
        loc_tab, glob_tab, n_tab = tabs[3 * ci:3 * ci + 3]
        base = tile * _max_pieces(size)

        def start(q, size=size, loc_tab=loc_tab, glob_tab=glob_tab):
            loc, glob = loc_tab[q], glob_tab[q]
            src, dst = ((loc_ref, loc), (glob_ref, glob)) if to_global else ((glob_ref, glob), (loc_ref, loc))
            _piece_copy(size, src[0], src[1], dst[0], dst[1], sem).start()

        n = n_tab[tile]
        if ci == 0:
            def issue_two(j, carry, start=start, base=base):
                start(base + 2 * j)
                start(base + 2 * j + 1)
                return carry

            lax.fori_loop(0, n // 2, issue_two, 0)
            pl.when(n % 2 == 1)(functools.partial(start, base + n - 1))
        else:
            def issue_one(j, carry, start=start, base=base):
                start(base + j)
                return carry

            lax.fori_loop(0, n, issue_one, 0)


def _wait_unit_copies(nun, src_ref, dst_ref, sem):
    def drain_group(j, carry):
        pltpu.make_async_copy(src_ref.at[pl.ds(0, WAIT_GROUP)], dst_ref.at[pl.ds(0, WAIT_GROUP)], sem).wait()
        return carry

    def drain_unit(j, carry):
        _unit_copy(src_ref, 0, dst_ref, 0, sem).wait()
        return carry

    lax.fori_loop(0, nun // WAIT_GROUP, drain_group, 0)
    lax.fori_loop(0, nun % WAIT_GROUP, drain_unit, 0)


def _dispatch_body(tabs, nun_ref, slot_ref, h2_ref, xs_ref, xloc, sem):
    i = pl.program_id(0)
    last = pl.num_programs(0) - 1
    tb = slot_ref.shape[1]
    nun = nun_ref[i]
    s = slot_ref[...]
    h2 = h2_ref[...]
    buf = i % 2

    def sort_chunk(c):
        r = lax.broadcasted_iota(jnp.int32, (SORT_CHUNK, tb), 0) + c * SORT_CHUNK
        hit = (s[0:1] == r) | (s[1:2] == r) | (s[2:3] == r) | (s[3:4] == r)
        p = jnp.where(hit, 1.0, 0.0).astype(BF16)
        units = SORT_CHUNK // ROW_UNIT
        xloc[buf, c * units:(c + 1) * units] = _dot(p, h2).astype(BF16).reshape(units, ROW_UNIT, h2.shape[1])

    typical_rows = TOP_K * tb + N_EXPERTS * ROW_UNIT // 2
    for c in range(LOCAL_ROWS // SORT_CHUNK):
        if (c + 1) * SORT_CHUNK <= typical_rows:
            sort_chunk(c)
        else:
            pl.when(c * (SORT_CHUNK // ROW_UNIT) < nun)(functools.partial(sort_chunk, c))

    _start_run_copies(tabs, i, xloc.at[buf], xs_ref, sem.at[buf], to_global=True)

    @pl.when(i > 0)
    def _():
        _wait_unit_copies(nun_ref[i - 1], xloc.at[1 - buf], xs_ref, sem.at[1 - buf])

    @pl.when(i == last)
    def _():
        _wait_unit_copies(nun, xloc.at[buf], xs_ref, sem.at[buf])


def _dispatch_first_kernel(*refs):
    tabs, (nun_ref, slot_ref, h2_ref, xs_ref, xloc, sem) = refs[:N_RUN_TABLES], refs[N_RUN_TABLES:]
    _dispatch_body(tabs, nun_ref, slot_ref, h2_ref, xs_ref, xloc, sem)


def _dispatch_next_kernel(*refs):
    tabs, (nun_ref, slot_ref, h2_ref, _, xs_ref, xloc, sem) = refs[:N_RUN_TABLES], refs[N_RUN_TABLES:]
    _dispatch_body(tabs, nun_ref, slot_ref, h2_ref, xs_ref, xloc, sem)


def _dispatch(tabs, nun, slot, h2, xs, n_rows, tb):
    T, d = h2.shape
    any_spec = pl.BlockSpec(memory_space=pl.ANY)
    in_specs = [pl.BlockSpec((TOP_K, tb), lambda i, *_: (0, i)),
                pl.BlockSpec((tb, d), lambda i, *_: (i, 0))]
    aliases = {}
    body = _dispatch_first_kernel
    args = tuple(tabs) + (nun, slot, h2)
    if xs is not None:
        in_specs.append(any_spec)
        aliases = {len(args): 0}
        body = _dispatch_next_kernel
        args = args + (xs,)
    return pl.pallas_call(
        body,
        out_shape=jax.ShapeDtypeStruct((n_rows // ROW_UNIT, ROW_UNIT, d), BF16),
        grid_spec=pltpu.PrefetchScalarGridSpec(
            num_scalar_prefetch=N_RUN_TABLES + 1, grid=(T // tb,), in_specs=in_specs, out_specs=any_spec,
            scratch_shapes=[pltpu.VMEM((2, LOCAL_ROWS // ROW_UNIT, ROW_UNIT, d), BF16),
                            pltpu.SemaphoreType.DMA((2,))]),
        input_output_aliases=aliases,
        compiler_params=pltpu.CompilerParams(dimension_semantics=("arbitrary",), has_side_effects=True,
                                             vmem_limit_bytes=VMEM_LIMIT),
        name="dispatch_next" if xs is not None else "dispatch_first",
    )(*args)


def _experts_kernel(te_ref, tv_ref, tf_ref, tn_ref, xs_ref, wg_hbm, bg_ref, wu_hbm, bu_ref, wd_hbm, bd_ref, y_ref,
                    stage_g, stage_u, stage_d, wgb, wub, wdb, sem):
    i = pl.program_id(0)
    valid = tv_ref[i]
    tm, d = xs_ref.shape
    ff = wgb.shape[1]

    def weight_copies(e):
        return [pltpu.make_async_copy(src.at[e], dst, sem.at[j])
                for j, (src, dst) in enumerate(((wg_hbm, stage_g), (wu_hbm, stage_u), (wd_hbm, stage_d)))]

    @pl.when(i == 0)
    def _():
        for cp in weight_copies(te_ref[0]):
            cp.start()

    @pl.when(tf_ref[i] == 1)
    def _():
        for cp in weight_copies(te_ref[i]):
            cp.wait()
        wgb[...] = stage_g[...].astype(BF16)
        wub[...] = stage_u[...].astype(BF16)
        wdb[...] = stage_d[...].astype(BF16)

        @pl.when(tn_ref[i] >= 0)
        def _():
            for cp in weight_copies(tn_ref[i]):
                cp.start()

    @pl.when(valid > 0)
    def _():
        row = lax.broadcasted_iota(jnp.int32, (tm, 1), 0)
        xb = jnp.where(row < valid, xs_ref[...], jnp.zeros((), BF16))
        acc = jnp.zeros((tm, d), F32)
        for c in range(ff // FF_CHUNK):
            cs = slice(c * FF_CHUNK, (c + 1) * FF_CHUNK)
            gate = jnp.minimum(_dot(xb, wgb[:, cs]) + bg_ref[0, :, cs], SWIGLU_LIMIT)
            up = jnp.clip(_dot(xb, wub[:, cs]) + bu_ref[0, :, cs], -SWIGLU_LIMIT, SWIGLU_LIMIT)
            act = (up + 1.0) * gate * jax.nn.sigmoid(SWIGLU_ALPHA * gate)
            acc = acc + _dot(act.astype(BF16), wdb[cs, :])
        y_ref[...] = (acc + bd_ref[0]).astype(BF16)

    @pl.when(valid == 0)
    def _():
        y_ref[...] = jnp.zeros_like(y_ref)


def _experts(tile_expert, tile_valid, tile_first, tile_next, xs, wg, bg, wu, bu, wd, bd, tmg):
    n_tiles = tile_expert.shape[0]
    _, d, ff = wg.shape
    bspec = lambda shp: pl.BlockSpec(shp, lambda i, te, *_: (te[i], 0, 0))
    any_spec = pl.BlockSpec(memory_space=pl.ANY)
    return pl.pallas_call(
        _experts_kernel,
        out_shape=jax.ShapeDtypeStruct(xs.shape, BF16),
        grid_spec=pltpu.PrefetchScalarGridSpec(
            num_scalar_prefetch=4,
            grid=(n_tiles,),
            in_specs=[
                pl.BlockSpec((tmg, d), lambda i, *_: (i, 0)),
                any_spec, bspec((1, 1, ff)),
                any_spec, bspec((1, 1, ff)),
                any_spec, bspec((1, 1, d)),
            ],
            out_specs=pl.BlockSpec((tmg, d), lambda i, *_: (i, 0)),
            scratch_shapes=[pltpu.VMEM((d, ff), F32), pltpu.VMEM((d, ff), F32), pltpu.VMEM((ff, d), F32),
                            pltpu.VMEM((d, ff), BF16), pltpu.VMEM((d, ff), BF16), pltpu.VMEM((ff, d), BF16),
                            pltpu.SemaphoreType.DMA((3,))],
        ),
        compiler_params=pltpu.CompilerParams(dimension_semantics=("arbitrary",), vmem_limit_bytes=EXPERTS_VMEM_LIMIT),
        name="experts",
    )(tile_expert, tile_valid, tile_first, tile_next, xs, wg, bg, wu, bu, wd, bd)


def _combine_kernel(*refs):
    tabs = refs[:N_RUN_TABLES]
    nun_ref, y_ref, route_ref, xmid_ref, g2_ref, nf_ref, o_ref, ybuf, sem = refs[N_RUN_TABLES:]
    sb, rb, d = xmid_ref.shape
    tb = sb * rb
    i = pl.program_id(0) * pl.num_programs(1) + pl.program_id(1)
    n_steps = pl.num_programs(0) * pl.num_programs(1)
    buf = i % 2

    @pl.when(i == 0)
    def _():
        ybuf[...] = jnp.zeros_like(ybuf)
        _start_run_copies(tabs, 0, ybuf.at[0], y_ref, sem.at[0], to_global=False)

    _wait_unit_copies(nun_ref[i], y_ref, ybuf.at[buf], sem.at[buf])

    @pl.when(i + 1 < n_steps)
    def _():
        _start_run_copies(tabs, i + 1, ybuf.at[1 - buf], y_ref, sem.at[1 - buf], to_global=False)

    chunk_units = PERM_CHUNK // ROW_UNIT

    def sorted_rows(c):
        return ybuf[buf, c * chunk_units:(c + 1) * chunk_units].reshape(PERM_CHUNK, d)

    route = route_ref[...]
    lane = lax.broadcasted_iota(jnp.int32, (tb, PERM_CHUNK), 1).astype(F32)
    wk = [jnp.broadcast_to(route[:, k:k + 1], (tb, PERM_CHUNK)) for k in range(TOP_K)]
    sk = [jnp.broadcast_to(route[:, TOP_K + k:TOP_K + k + 1], (tb, PERM_CHUNK)) for k in range(TOP_K)]

    def weights_chunk(c):
        r = lane + float(c * PERM_CHUNK)
        pw = jnp.zeros((tb, PERM_CHUNK), F32)
        for k in range(TOP_K):
            pw = jnp.where(sk[k] == r, wk[k], pw)
        return pw.astype(BF16)

    def contribution(c):
        return _dot(weights_chunk(c), sorted_rows(c))

    moe = contribution(0)
    for c in range(1, LOCAL_ROWS // PERM_CHUNK):
        moe = moe + contribution(c)
    out = xmid_ref[...] + g2_ref[...] * moe.reshape(sb, rb, d)
    ms = jnp.mean(out * out, axis=-1, keepdims=True)
    o_ref[...] = out * lax.rsqrt(ms + EPS) * nf_ref[...]


def _combine(tabs, nun, y, route, xmid, g2, nf, sb, rb):
    nseq, L, d = xmid.shape
    tb = sb * rb
    nt = L // rb
    tok = lambda s, t, *_: (s * nt + t, 0)
    return pl.pallas_call(
        _combine_kernel,
        out_shape=jax.ShapeDtypeStruct((nseq, L, d), F32),
        grid_spec=pltpu.PrefetchScalarGridSpec(
            num_scalar_prefetch=N_RUN_TABLES + 1,
            grid=(nseq // sb, nt),
            in_specs=[
                pl.BlockSpec(memory_space=pl.ANY),
                pl.BlockSpec((tb, 2 * TOP_K), tok),
                pl.BlockSpec((sb, rb, d), lambda s, t, *_: (s, t, 0)),
                pl.BlockSpec((sb, 1, d), lambda s, t, *_: (s, 0, 0)),
                pl.BlockSpec((1, d), lambda s, t, *_: (0, 0)),
            ],
            out_specs=pl.BlockSpec((sb, rb, d), lambda s, t, *_: (s, t, 0)),
            scratch_shapes=[pltpu.VMEM((2, LOCAL_ROWS // ROW_UNIT, ROW_UNIT, d), BF16),
                            pltpu.SemaphoreType.DMA((2,))],
        ),
        compiler_params=_cparams("arbitrary", "arbitrary"),
        name="combine",
    )(*tabs, nun, y, route, xmid, g2, nf)


def _tile_rows(nseq, L, tile):
    if L >= tile:
        assert L % tile == 0
        return 1, tile
    assert tile % L == 0 and nseq % (tile // L) == 0
    return tile // L, L


def kernel(x_prompt, x_sample, state_gla, c_prompt, c_sample, w_ada, b_ada, norm1, w_in, w_gk, b_gk, gla_norm,
           gmlp_ln_g, gmlp_ln_b, gmlp_w_s, gmlp_b_s, w_out, norm2, w_router, b_router, w_gate, b_gate, w_up,
           b_up, w_down, b_down, norm_f):
    depth = w_ada.shape[0]
    assert depth == 1
    bp, lp, d = x_prompt.shape
    bs, ls, _ = x_sample.shape
    tp, ts = bp * lp, bs * ls

    nc = bp + bs
    ncp = -(-nc // SUBLANE) * SUBLANE
    c_all = jnp.concatenate([c_prompt, c_sample, jnp.zeros((ncp - nc, d), F32)], axis=0)
    mod = _ada(c_all, w_ada[0], b_ada[0][None]).reshape(ncp, N_MOD, 1, d)
    mods_p = [mod[:bp, i] for i in range(N_MOD)]
    mods_s = [mod[bp:nc, i] for i in range(N_MOD)]

    wi = w_in[0]
    c_lr = 2 * GLA_QK + GLA_WIDTH
    c_r = c_lr + GLA_LOWRANK
    wm = jnp.concatenate([wi[:, :c_lr], wi[:, c_r:]], axis=1).astype(BF16)
    wlr = jnp.pad(wi[:, c_lr:c_r], ((0, 0), (0, LANE - GLA_LOWRANK))).astype(BF16)
    wgk_f = jnp.pad(w_gk[0], ((0, LANE - GLA_LOWRANK), (0, 0)))
    wgk_hi = wgk_f.astype(BF16)
    wgk = jnp.concatenate([wgk_hi, (wgk_f - wgk_hi.astype(F32)).astype(BF16)], axis=1)
    bgk = b_gk[0][None]
    n1, n2, nf = norm1[0][None], norm2[0][None], norm_f[None]
    gn, lng, lnb = gla_norm[0][None], gmlp_ln_g[0][None], gmlp_ln_b[0][None]
    ws, bsv = gmlp_w_s[0], gmlp_b_s[0]
    pos_i = jnp.arange(GMLP_BLOCK)
    cmask = (pos_i[None, :] // CHUNK) <= (pos_i[:, None] // CHUNK)
    wm_p = jnp.where(cmask[None], ws, 0.0).astype(BF16)
    bsb_p = jnp.repeat(bsv.T, GMLP_GC, axis=1)
    reps = ROW_BLOCK // ls
    eye = jnp.eye(reps, dtype=F32)
    wm_s = jnp.einsum("ab,gij->gaibj", eye, ws[:, :ls, :ls]).reshape(GMLP_GROUPS, ROW_BLOCK, ROW_BLOCK).astype(BF16)
    bsb_s = jnp.tile(jnp.repeat(bsv[:, :ls].T, GMLP_GC, axis=1), (reps, 1))
    wo = w_out[0].astype(BF16)
    wr_f = jnp.pad(w_router[0], ((0, 0), (0, LANE - N_EXPERTS)))
    wr_hi = wr_f.astype(BF16)
    wr = jnp.concatenate([wr_hi, (wr_f - wr_hi.astype(F32)).astype(BF16)], axis=1)
    br = jnp.concatenate([b_router[0], jnp.full((LANE - N_EXPERTS,), -1e30, F32)])[None]
    upper = (jnp.arange(TOKEN_TILE)[:, None] < jnp.arange(TOKEN_TILE)[None, :]).astype(BF16)
    lower = (jnp.arange(N_EXPERTS)[None, :] < jnp.arange(N_EXPERTS)[:, None]).astype(F32)
    wg, wu, wd = w_gate[0], w_up[0], w_down[0]
    bg, bu, bd = b_gate[0][:, None], b_up[0][:, None], b_down[0][:, None]

    sbp, rbp = _tile_rows(bp, lp, TOKEN_TILE)
    sbs, rbs = _tile_rows(bs, ls, TOKEN_TILE)

    assert lp % TOKEN_TILE == 0
    mix_p, state_p = _mixer_prompt(x_prompt, mods_p[1], mods_p[0], n1, wm, wlr, wgk, bgk, gn, lng, lnb, wm_p, bsb_p,
                                   TOKEN_TILE)
    proj_s, gk_s = _inproj(x_sample, mods_s[1], mods_s[0], n1, wm, wlr, wgk, bgk, sbs, rbs)
    mix_s, state_s, vn_s = _mixer_sample(proj_s, gk_s, state_gla[0], gn, lng, lnb, wm_s, bsb_s)

    xmid_p, h2_p, route_p, slot_p, n16_p = _outproj(
        mix_p, x_prompt, mods_p[2], mods_p[4], mods_p[3], n2, wo, wr, br, upper, lower, sbp, rbp)
    xmid_s, h2_s, route_s, slot_s, n16_s = _outproj(
        mix_s, x_sample, mods_s[2], mods_s[4], mods_s[3], n2, wo, wr, br, upper, lower, sbs, rbs)

    tmg = EXPERT_TILE
    ntp = tp // TOKEN_TILE
    eids = jnp.arange(N_EXPERTS, dtype=jnp.int32)
    n16 = jnp.concatenate([n16_p[:, :, 0], n16_s[:, :, 0]], axis=0).astype(jnp.int32)
    nt_all = n16.shape[0]
    earlier = jnp.cumsum(n16, axis=0) - n16
    tot = jnp.sum(n16, axis=0)
    tiles_e = (tot + tmg - 1) // tmg
    tile_end = jnp.cumsum(tiles_e)
    tile_start = tile_end - tiles_e
    row_start = tile_start * tmg
    n_tiles = (TOP_K * (tp + ts) + nt_all * N_EXPERTS * (ROW_UNIT - 1)) // tmg + N_EXPERTS
    tid = jnp.arange(n_tiles, dtype=jnp.int32)
    te = jnp.minimum(jnp.sum((tid[:, None] >= tile_end[None, :]).astype(jnp.int32), axis=1), N_EXPERTS - 1)
    te_hot = te[:, None] == eids[None, :]
    tot_te = jnp.sum(jnp.where(te_hot, tot[None, :], 0), axis=1)
    start_te = jnp.sum(jnp.where(te_hot, tile_start[None, :], 0), axis=1)
    active = tid < tile_end[-1]
    tv = jnp.where(active, jnp.clip(tot_te - (tid - start_te) * tmg, 0, tmg), 0).astype(jnp.int32)
    last_e = jnp.max(jnp.where(tiles_e > 0, eids, 0)).astype(jnp.int32)
    te = jnp.where(active, te, last_e).astype(jnp.int32)
    te_prev = jnp.concatenate([jnp.full((1,), -1, jnp.int32), te[:-1]])
    tf = (active & (te != te_prev)).astype(jnp.int32)
    later_used = (eids[None, :] > te[:, None]) & (tiles_e[None, :] > 0)
    tn = jnp.min(jnp.where(later_used, eids[None, :], N_EXPERTS), axis=1)
    tn = jnp.where(tn < N_EXPERTS, tn, -1).astype(jnp.int32)

    run_units = n16 // ROW_UNIT
    run_loc = (jnp.cumsum(n16, axis=1) - n16) // ROW_UNIT
    run_glob = (row_start[None, :] + earlier) // ROW_UNIT
    nun = jnp.sum(run_units, axis=1).astype(jnp.int32)

    def piece_tables(cnt, loc0, glob0, stride, max_pieces):
        end = jnp.cumsum(cnt, axis=1)
        start = end - cnt
        piece = jnp.arange(max_pieces, dtype=jnp.int32)
        run = jnp.sum((piece[None, :, None] >= end[:, None, :]).astype(jnp.int32), axis=2)
        hot = run[:, :, None] == eids[None, None, :]
        pick = lambda a: jnp.sum(jnp.where(hot, a[:, None, :], 0), axis=2)
        within = (piece[None, :] - pick(start)) * stride
        live = piece[None, :] < end[:, -1:]
        loc = jnp.where(live, pick(loc0) + within, 0).astype(jnp.int32)
        glob = jnp.where(live, pick(glob0) + within, 0).astype(jnp.int32)
        return loc, glob, end[:, -1].astype(jnp.int32)

    tables = []
    done = jnp.zeros_like(run_units)
    for size in PIECE_SIZES:
        cnt = (run_units - done) // size
        tables.append(piece_tables(cnt, run_loc + done, run_glob + done, size, _max_pieces(size)))
        done = done + cnt * size

    def tables_for(tiles):
        return tuple(a[tiles].reshape(-1) for tab in tables for a in tab)

    tabs_p, tabs_s = tables_for(slice(0, ntp)), tables_for(slice(ntp, nt_all))

    n_rows = n_tiles * tmg
    xs = _dispatch(tabs_p, nun[:ntp], slot_p, h2_p, None, n_rows, TOKEN_TILE)
    xs = _dispatch(tabs_s, nun[ntp:], slot_s, h2_s, xs, n_rows, TOKEN_TILE)
    y = _experts(te, tv, tf, tn, xs.reshape(n_rows, d), wg, bg, wu, bu, wd, bd, tmg)
    y = y.reshape(n_rows // ROW_UNIT, ROW_UNIT, d)
    y_prompt = _combine(tabs_p, nun[:ntp], y, route_p, xmid_p, mods_p[5], nf, sbp, rbp)
    y_sample = _combine(tabs_s, nun[ntp:], y, route_s, xmid_s, mods_s[5], nf, sbs, rbs)

    return (y_prompt, y_sample, state_p[None], state_s[None], vn_s[None])
```

```python
import functools

import jax
import jax.numpy as jnp
from jax import lax
from jax.experimental import pallas as pl
from jax.experimental.pallas import tpu as pltpu

F32 = jnp.float32
BF16 = jnp.bfloat16
HIGHEST = lax.Precision.HIGHEST

CHUNK = 64
GLA_HEADS = 4
GLA_DK = 64
GLA_DV = 128
GLA_QK = GLA_HEADS * GLA_DK
GLA_WIDTH = GLA_HEADS * GLA_DV
GLA_LOWRANK = 16
GLA_GATE_NORM = 16.0
GATE_SAFE_MIN = -60.0
GMLP_WIDTH = 512
GMLP_GROUPS = 4
GMLP_GC = GMLP_WIDTH // GMLP_GROUPS
GMLP_BLOCK = 128
N_EXPERTS = 32
TOP_K = 4
SWIGLU_LIMIT = 7.0
SWIGLU_ALPHA = 1.702
EPS = 1e-6
N_MOD = 6

LANE = 128
SUBLANE = 8

Q0 = 0
K0 = Q0 + GLA_QK
V0 = K0 + GLA_QK
R0 = V0 + GLA_WIDTH
U0 = R0 + GLA_WIDTH
G0 = U0 + GMLP_WIDTH
PROJ_COLS = G0 + GMLP_WIDTH

ROW_BLOCK = 128
TOKEN_TILE = 512
EXPERT_TILE = 512
FF_CHUNK = 512
ROW_UNIT = 16
LOCAL_ROWS = 2560
WAIT_GROUP = 8
PIECE_SIZES = (4, 2, 1)
N_RUN_TABLES = 3 * len(PIECE_SIZES)
SORT_CHUNK = 128
PERM_CHUNK = 256
VMEM_LIMIT = 48 * 1024 * 1024
EXPERTS_VMEM_LIMIT = 56 * 1024 * 1024


def _cparams(*sem):
    return pltpu.CompilerParams(dimension_semantics=sem, vmem_limit_bytes=VMEM_LIMIT)


def _dot(a, b, **kw):
    return jnp.dot(a, b, preferred_element_type=F32, **kw)


def _dot_nt(a, b):
    return lax.dot_general(a, b, (((1,), (1,)), ((), ())), preferred_element_type=F32)


def _gelu(x):
    return 0.5 * x * (1.0 + lax.erf(x * (0.5 ** 0.5)))


def _dot_tn(a, b):
    return lax.dot_general(a, b, (((0,), (0,)), ((), ())), preferred_element_type=F32)


def _ada_kernel(c_ref, w_ref, b_ref, o_ref):
    c = c_ref[...]
    s = c * jax.nn.sigmoid(c)
    o_ref[...] = _dot(s, w_ref[...], precision=HIGHEST) + b_ref[...]


def _ada(c, w_ada, b_ada):
    n, d = c.shape
    return pl.pallas_call(
        _ada_kernel,
        out_shape=jax.ShapeDtypeStruct((n, N_MOD * d), F32),
        grid=(N_MOD,),
        in_specs=[
            pl.BlockSpec((n, d), lambda j: (0, 0)),
            pl.BlockSpec((d, d), lambda j: (0, j)),
            pl.BlockSpec((1, d), lambda j: (0, j)),
        ],
        out_specs=pl.BlockSpec((n, d), lambda j: (0, j)),
        compiler_params=_cparams("arbitrary"),
        name="ada",
    )(c, w_ada, b_ada)


PROJ_COL_CHUNK = 512


def _modulated_norm(x_ref, sc_ref, sh_ref, n1_ref):
    sb, rb, d = x_ref.shape
    x = x_ref[...]
    ms = jnp.mean(x * x, axis=-1, keepdims=True)
    h = x * lax.rsqrt(ms + EPS) * n1_ref[...]
    h = h * (1.0 + sc_ref[...]) + sh_ref[...]
    return h.reshape(sb * rb, d).astype(BF16)


def _gate_preact(hb, wlr_ref, wgk_ref, bgk_ref):
    lr = _dot(hb, wlr_ref[...])
    lr_hi = lr.astype(BF16)
    lr_lo = (lr - lr_hi.astype(F32)).astype(BF16)
    part = _dot(lr_hi, wgk_ref[...])
    return part[:, :GLA_QK] + part[:, GLA_QK:] + _dot(lr_lo, wgk_ref[:, :GLA_QK]) + bgk_ref[...]


def _inproj_kernel(x_ref, sc_ref, sh_ref, n1_ref, wm_ref, wlr_ref, wgk_ref, bgk_ref, proj_ref, gk_ref):
    sb, rb, _ = x_ref.shape
    hb = _modulated_norm(x_ref, sc_ref, sh_ref, n1_ref)
    cw = PROJ_COL_CHUNK
    for c in range(PROJ_COLS // cw):
        p = _dot(hb, wm_ref[:, c * cw:(c + 1) * cw])
        proj_ref[:, :, c * cw:(c + 1) * cw] = p.astype(BF16).reshape(sb, rb, cw)
    gk_ref[...] = _gate_preact(hb, wlr_ref, wgk_ref, bgk_ref).reshape(sb, rb, GLA_QK)


def _inproj(x, sc, sh, n1, wm, wlr, wgk, bgk, sb, rb):
    nseq, L, d = x.shape
    grid = (nseq // sb, L // rb)
    const = lambda s, t: (0, 0)
    return pl.pallas_call(
        _inproj_kernel,
        out_shape=(jax.ShapeDtypeStruct((nseq, L, PROJ_COLS), BF16),
                   jax.ShapeDtypeStruct((nseq, L, GLA_QK), F32)),
        grid=grid,
        in_specs=[
            pl.BlockSpec((sb, rb, d), lambda s, t: (s, t, 0)),
            pl.BlockSpec((sb, 1, d), lambda s, t: (s, 0, 0)),
            pl.BlockSpec((sb, 1, d), lambda s, t: (s, 0, 0)),
            pl.BlockSpec((1, d), const),
            pl.BlockSpec((d, PROJ_COLS), const),
            pl.BlockSpec((d, LANE), const),
            pl.BlockSpec((LANE, 2 * GLA_QK), const),
            pl.BlockSpec((1, GLA_QK), const),
        ],
        out_specs=(pl.BlockSpec((sb, rb, PROJ_COLS), lambda s, t: (s, t, 0)),
                   pl.BlockSpec((sb, rb, GLA_QK), lambda s, t: (s, t, 0))),
        compiler_params=_cparams("arbitrary", "arbitrary"),
        name="inproj",
    )(x, sc, sh, n1, wm, wlr, wgk, bgk)


def _head_masks():
    lane = lax.broadcasted_iota(jnp.int32, (1, GLA_QK), 1)
    return [(lane // GLA_DK) == h for h in range(GLA_HEADS)]


def _stack_heads(x, hm):
    return jnp.concatenate([jnp.where(m, x, 0.0) for m in hm], axis=0).astype(BF16)


def _chunk_pair_ids(c_len):
    ti = lax.broadcasted_iota(jnp.int32, (GLA_HEADS * c_len, c_len), 0) % c_len
    si = lax.broadcasted_iota(jnp.int32, (GLA_HEADS * c_len, c_len), 1)
    return ti, si


def _scores_factored(q4s, k, G, c_len):
    ti, si = _chunk_pair_ids(c_len)
    ke = (k * jnp.exp(-G)).astype(BF16)
    return jnp.concatenate(
        [jnp.where(si <= ti, _dot_nt(q4, ke[c * c_len:(c + 1) * c_len]), 0.0) for c, q4 in enumerate(q4s)], axis=0)


def _scores_bounded(qs, k, G, c_len, hm):
    ti, si = _chunk_pair_ids(c_len)
    t = lax.broadcasted_iota(jnp.int32, (c_len, 1), 0)
    col = lax.broadcasted_iota(jnp.int32, (c_len, c_len), 1)
    outs = []
    for c in range(qs.shape[0] // c_len):
        sl = slice(c * c_len, (c + 1) * c_len)
        q_c, k_c, g_c = qs[sl], k[sl], G[sl]
        a = jnp.where(si == ti, _dot_nt(_stack_heads(q_c, hm), k_c.astype(BF16)), 0.0)
        half = c_len // 2
        while half >= 1:
            blk = 2 * half
            sel = (col == (t // blk) * blk + (half - 1)).astype(F32)
            ref = _dot(sel, g_c, precision=HIGHEST)
            upper = (t % blk) >= half
            qh = jnp.where(upper, q_c * jnp.exp(jnp.minimum(g_c - ref, 0.0)), 0.0)
            kh = jnp.where(upper, 0.0, k_c * jnp.exp(jnp.minimum(ref - g_c, 0.0)))
            same = (ti // blk) == (si // blk)
            a = a + jnp.where(same, _dot_nt(_stack_heads(qh, hm), kh.astype(BF16)), 0.0)
            half //= 2
        outs.append(a)
    return jnp.concatenate(outs, axis=0)


def _cum_log_gates(gkpre, c_len):
    rows = gkpre.shape[0]
    g = jax.nn.log_sigmoid(gkpre) / GLA_GATE_NORM
    ri = lax.broadcasted_iota(jnp.int32, (rows, rows), 0)
    ci = lax.broadcasted_iota(jnp.int32, (rows, rows), 1)
    tri = jnp.where((ci <= ri) & ((ci // c_len) == (ri // c_len)), 1.0, 0.0).astype(BF16)
    g_hi = g.astype(BF16)
    r1 = g - g_hi.astype(F32)
    g_mid = r1.astype(BF16)
    g_lo = (r1 - g_mid.astype(F32)).astype(BF16)
    parts = _dot(tri, jnp.concatenate([g_hi, g_mid, g_lo], axis=1))
    return parts[:, :GLA_QK] + parts[:, GLA_QK:2 * GLA_QK] + parts[:, 2 * GLA_QK:]


def _factored_is_safe(G, c_len):
    ends = [G[(c + 1) * c_len - 1:(c + 1) * c_len] for c in range(G.shape[0] // c_len)]
    return jnp.min(jnp.concatenate(ends, axis=0)) > GATE_SAFE_MIN


def _mixer_block(p, G, states, c_len, gn, lng, lnb, wm_ref, bsb, bounded, side=()):
    rows = p.shape[0]
    n_chunks = rows // c_len
    chained = len(states) == 1
    hm = _head_masks()
    side = list(side)

    def side_step():
        if side:
            side.pop(0)()

    q = p[:, Q0:Q0 + GLA_QK].astype(F32)
    k = p[:, K0:K0 + GLA_QK].astype(F32)
    v = p[:, V0:V0 + GLA_WIDTH]
    r = p[:, R0:R0 + GLA_WIDTH].astype(F32)

    qs = q * (GLA_DK ** -0.5)
    qe = qs * jnp.exp(G)
    q4s = [_stack_heads(qe[c * c_len:(c + 1) * c_len], hm) for c in range(n_chunks)]
    scores = _scores_bounded(qs, k, G, c_len, hm) if bounded else _scores_factored(q4s, k, G, c_len)
    hc = GLA_HEADS * c_len

    new_states = []
    o_rows = []
    st = states[0]
    for c in range(n_chunks):
        lo, hi = c * c_len, (c + 1) * c_len
        if not chained:
            st = states[c]
        q4 = q4s[c]
        a = scores[c * hc:(c + 1) * hc].astype(BF16)
        o_inter = _dot_nt(q4, st.astype(BF16))
        v_c = v[lo:hi]
        heads = []
        for h in range(GLA_HEADS):
            o_h = o_inter[h * c_len:(h + 1) * c_len] + _dot(
                a[h * c_len:(h + 1) * c_len], v_c[:, h * GLA_DV:(h + 1) * GLA_DV])
            heads.append(o_h)
        o_rows.append(jnp.concatenate(heads, axis=1))
        g_last = G[hi - 1:hi]
        kd = (k[lo:hi] * jnp.exp(g_last - G[lo:hi])).astype(BF16)
        upd = _dot_tn(v_c, kd)
        st_new = jnp.exp(g_last) * st
        for h in range(GLA_HEADS):
            st_new = st_new + jnp.where(hm[h], upd[h * GLA_DV:(h + 1) * GLA_DV], 0.0)
        if chained:
            st = st_new
        else:
            new_states.append(st_new)
        side_step()
    if chained:
        new_states = [st]
    o = jnp.concatenate(o_rows, axis=0)

    gla = []
    for h in range(GLA_HEADS):
        o_h = o[:, h * GLA_DV:(h + 1) * GLA_DV]
        ms = jnp.mean(o_h * o_h, axis=-1, keepdims=True)
        r_h = r[:, h * GLA_DV:(h + 1) * GLA_DV]
        gla.append(o_h * lax.rsqrt(ms + EPS) * gn * (r_h * jax.nn.sigmoid(r_h)))
    side_step()

    u = _gelu(p[:, U0:U0 + GMLP_WIDTH].astype(F32))
    side_step()
    vv = _gelu(p[:, G0:G0 + GMLP_WIDTH].astype(F32))
    side_step()
    mu = jnp.mean(vv, axis=-1, keepdims=True)
    xc = vv - mu
    var = jnp.mean(xc * xc, axis=-1, keepdims=True)
    vn = xc * lax.rsqrt(var + EPS) * lng + lnb
    vnb = vn.astype(BF16)
    gm = []
    for gi in range(GMLP_GROUPS):
        sl = slice(gi * GMLP_GC, (gi + 1) * GMLP_GC)
        mixed = _dot(wm_ref[gi], vnb[:, sl]) + bsb[:, sl]
        gm.append(u[:, sl] * mixed)
    out = jnp.concatenate(gla + gm, axis=1)
    while side:
        side_step()
    return out, new_states, vn


def _mixer_prompt_kernel(x_ref, sc_ref, sh_ref, n1_ref, wm_ref, wlr_ref, wgk_ref, bgk_ref,
                         gn_ref, lng_ref, lnb_ref, ws_ref, bsb_ref, mix_ref, s_ref, st_scr):
    t = pl.program_id(1)
    nt = pl.num_programs(1)

    @pl.when(t == 0)
    def _():
        st_scr[...] = jnp.zeros_like(st_scr)

    tb = x_ref.shape[1]
    n_sub = tb // ROW_BLOCK
    cw = PROJ_COL_CHUNK
    hb = _modulated_norm(x_ref, sc_ref, sh_ref, n1_ref)

    def proj_steps(j, parts):
        hb_j = hb[j * ROW_BLOCK:(j + 1) * ROW_BLOCK]
        return [functools.partial(
            lambda c: parts.append(_dot(hb_j, wm_ref[:, c * cw:(c + 1) * cw]).astype(BF16)), c)
            for c in range(PROJ_COLS // cw)]

    gk = _gate_preact(hb, wlr_ref, wgk_ref, bgk_ref)
    first = []
    for step in proj_steps(0, first):
        step()
    Gs = [_cum_log_gates(gk[j * ROW_BLOCK:(j + 1) * ROW_BLOCK], CHUNK) for j in range(n_sub)]
    safe = _factored_is_safe(jnp.concatenate(Gs, axis=0), CHUNK)

    def run(bounded):
        st = st_scr[...]
        parts = first
        for j in range(n_sub):
            p_j = jnp.concatenate(parts, axis=1)
            parts = []
            side = proj_steps(j + 1, parts) if j + 1 < n_sub else []
            out, sts, _ = _mixer_block(p_j, Gs[j], [st], CHUNK, gn_ref[...], lng_ref[...], lnb_ref[...], ws_ref,
                                       bsb_ref[...], bounded, side)
            st = sts[0]
            mix_ref[0, j * ROW_BLOCK:(j + 1) * ROW_BLOCK, :] = out.astype(BF16)
        st_scr[...] = st

    pl.when(safe)(functools.partial(run, False))
    pl.when(jnp.logical_not(safe))(functools.partial(run, True))

    @pl.when(t == nt - 1)
    def _():
        s_ref[0] = st_scr[...].T.reshape(GLA_HEADS, GLA_DK, GLA_DV)


def _mixer_prompt(x, sc, sh, n1, wm, wlr, wgk, bgk, gn, lng, lnb, ws, bsb, tb):
    b, L, d = x.shape
    const2 = lambda s, t: (0, 0)
    return pl.pallas_call(
        _mixer_prompt_kernel,
        out_shape=(jax.ShapeDtypeStruct((b, L, 2 * GLA_WIDTH), BF16),
                   jax.ShapeDtypeStruct((b, GLA_HEADS, GLA_DK, GLA_DV), F32)),
        grid=(b, L // tb),
        in_specs=[
            pl.BlockSpec((1, tb, d), lambda s, t: (s, t, 0)),
            pl.BlockSpec((1, 1, d), lambda s, t: (s, 0, 0)),
            pl.BlockSpec((1, 1, d), lambda s, t: (s, 0, 0)),
            pl.BlockSpec((1, d), const2),
            pl.BlockSpec((d, PROJ_COLS), const2),
            pl.BlockSpec((d, LANE), const2),
            pl.BlockSpec((LANE, 2 * GLA_QK), const2),
            pl.BlockSpec((1, GLA_QK), const2),
            pl.BlockSpec((1, GLA_DV), const2),
            pl.BlockSpec((1, GMLP_WIDTH), const2),
            pl.BlockSpec((1, GMLP_WIDTH), const2),
            pl.BlockSpec((GMLP_GROUPS, ROW_BLOCK, ROW_BLOCK), lambda s, t: (0, 0, 0)),
            pl.BlockSpec((ROW_BLOCK, GMLP_WIDTH), const2),
        ],
        out_specs=(pl.BlockSpec((1, tb, 2 * GLA_WIDTH), lambda s, t: (s, t, 0)),
                   pl.BlockSpec((1, GLA_HEADS, GLA_DK, GLA_DV), lambda s, t: (s, 0, 0, 0))),
        scratch_shapes=[pltpu.VMEM((GLA_DV, GLA_QK), F32)],
        compiler_params=_cparams("arbitrary", "arbitrary"),
        name="mixer_prompt",
    )(x, sc, sh, n1, wm, wlr, wgk, bgk, gn, lng, lnb, ws, bsb)


def _mixer_sample_kernel(proj_ref, gk_ref, s0_ref, gn_ref, lng_ref, lnb_ref, wm_ref, bsb_ref,
                         mix_ref, s_ref, vn_ref):
    sb, rb, _ = proj_ref.shape
    G = _cum_log_gates(gk_ref[...].reshape(sb * rb, GLA_QK), rb)
    safe = _factored_is_safe(G, rb)

    def run(bounded):
        p = proj_ref[...].reshape(sb * rb, PROJ_COLS)
        states = [s0_ref[i].reshape(GLA_QK, GLA_DV).T for i in range(sb)]
        out, sts, vn = _mixer_block(p, G, states, rb, gn_ref[...], lng_ref[...], lnb_ref[...], wm_ref,
                                    bsb_ref[...], bounded)
        mix_ref[...] = out.astype(BF16).reshape(sb, rb, 2 * GLA_WIDTH)
        vn_ref[...] = vn.reshape(sb, rb, GMLP_WIDTH)
        for i in range(sb):
            s_ref[i] = sts[i].T.reshape(GLA_HEADS, GLA_DK, GLA_DV)

    pl.when(safe)(functools.partial(run, False))
    pl.when(jnp.logical_not(safe))(functools.partial(run, True))


def _mixer_sample(proj, gk, s0, gn, lng, lnb, wm, bsb):
    n, L, _ = proj.shape
    sb = ROW_BLOCK // L
    const2 = lambda s: (0, 0)
    return pl.pallas_call(
        _mixer_sample_kernel,
        out_shape=(jax.ShapeDtypeStruct((n, L, 2 * GLA_WIDTH), BF16),
                   jax.ShapeDtypeStruct((n, GLA_HEADS, GLA_DK, GLA_DV), F32),
                   jax.ShapeDtypeStruct((n, L, GMLP_WIDTH), F32)),
        grid=(n // sb,),
        in_specs=[
            pl.BlockSpec((sb, L, PROJ_COLS), lambda s: (s, 0, 0)),
            pl.BlockSpec((sb, L, GLA_QK), lambda s: (s, 0, 0)),
            pl.BlockSpec((sb, GLA_HEADS, GLA_DK, GLA_DV), lambda s: (s, 0, 0, 0)),
            pl.BlockSpec((1, GLA_DV), const2),
            pl.BlockSpec((1, GMLP_WIDTH), const2),
            pl.BlockSpec((1, GMLP_WIDTH), const2),
            pl.BlockSpec((GMLP_GROUPS, ROW_BLOCK, ROW_BLOCK), lambda s: (0, 0, 0)),
            pl.BlockSpec((ROW_BLOCK, GMLP_WIDTH), const2),
        ],
        out_specs=(pl.BlockSpec((sb, L, 2 * GLA_WIDTH), lambda s: (s, 0, 0)),
                   pl.BlockSpec((sb, GLA_HEADS, GLA_DK, GLA_DV), lambda s: (s, 0, 0, 0)),
                   pl.BlockSpec((sb, L, GMLP_WIDTH), lambda s: (s, 0, 0))),
        compiler_params=_cparams("arbitrary"),
        name="mixer_sample",
    )(proj, gk, s0, gn, lng, lnb, wm, bsb)


def _outproj_kernel(mix_ref, x_ref, g1_ref, sc_ref, sh_ref, n2_ref, wo_ref, wr_ref, br_ref, upper_ref, lower_ref,
                    xmid_ref, h2_ref, route_ref, slot_ref, n16_ref):
    sb, rb, d = x_ref.shape
    tm = sb * rb

    y = _dot(mix_ref[...].reshape(tm, d), wo_ref[...])
    xm = x_ref[...] + g1_ref[...] * y.reshape(sb, rb, d)
    xmid_ref[...] = xm
    ms = jnp.mean(xm * xm, axis=-1, keepdims=True)
    h2 = xm * lax.rsqrt(ms + EPS) * n2_ref[...]
    h2 = (h2 * (1.0 + sc_ref[...]) + sh_ref[...]).reshape(tm, d)
    h_hi = h2.astype(BF16)
    h2_ref[...] = h_hi

    h_lo = (h2 - h_hi.astype(F32)).astype(BF16)
    part = _dot(h_hi, wr_ref[...])
    logits = part[:, :LANE] + part[:, LANE:] + _dot(h_lo, wr_ref[:, :LANE]) + br_ref[...]
    l = logits.T[:N_EXPERTS]
    eid = lax.broadcasted_iota(jnp.int32, (N_EXPERTS, tm), 0).astype(F32)
    top_l, sel = [], []
    for _ in range(TOP_K):
        m = jnp.max(l, axis=0, keepdims=True)
        idx = jnp.min(jnp.where(l == m, eid, float(N_EXPERTS)), axis=0, keepdims=True)
        hit = eid == idx
        top_l.append(m)
        sel.append(hit)
        l = jnp.where(hit, -jnp.inf, l)
    ex = [jnp.exp(t - top_l[0]) for t in top_l]
    den = ex[0] + ex[1] + ex[2] + ex[3]
    top_w = [e / den for e in ex]

    chosen = (sel[0] | sel[1] | sel[2] | sel[3])
    cb = jnp.where(chosen, 1.0, 0.0)
    before = _dot(cb.astype(BF16), upper_ref[...])
    n = jnp.sum(cb, axis=1, keepdims=True)
    n16 = jnp.floor((n + (ROW_UNIT - 1)) * (1.0 / ROW_UNIT)) * ROW_UNIT
    n16b = jnp.broadcast_to(n16, (N_EXPERTS, LANE))
    ls = _dot(lower_ref[...], n16b, precision=HIGHEST)[:, 0:1]
    base = before + ls
    slots = [jnp.sum(jnp.where(s, base, 0.0), axis=0, keepdims=True) for s in sel]
    slot_ref[...] = jnp.concatenate(slots, axis=0).astype(jnp.int32)
    n16_ref[0] = n16b
    rows = jnp.concatenate(top_w + slots + [jnp.zeros((LANE - 2 * TOP_K, tm), F32)], axis=0)
    route_ref[...] = rows.T[:, :2 * TOP_K]


def _outproj(mix, x, g1, sc2, sh2, n2, wo, wr, br, upper, lower, sb, rb):
    nseq, L, d = x.shape
    tm = sb * rb
    nt = L // rb
    T = nseq * L
    grid = (nseq // sb, nt)
    const = lambda s, t: (0, 0)
    tok = lambda s, t: (0, s * nt + t)
    return pl.pallas_call(
        _outproj_kernel,
        out_shape=(jax.ShapeDtypeStruct((nseq, L, d), F32),
                   jax.ShapeDtypeStruct((T, d), BF16),
                   jax.ShapeDtypeStruct((T, 2 * TOP_K), F32),
                   jax.ShapeDtypeStruct((TOP_K, T), jnp.int32),
                   jax.ShapeDtypeStruct((T // tm, N_EXPERTS, LANE), F32)),
        grid=grid,
        in_specs=[
            pl.BlockSpec((sb, rb, d), lambda s, t: (s, t, 0)),
            pl.BlockSpec((sb, rb, d), lambda s, t: (s, t, 0)),
            pl.BlockSpec((sb, 1, d), lambda s, t: (s, 0, 0)),
            pl.BlockSpec((sb, 1, d), lambda s, t: (s, 0, 0)),
            pl.BlockSpec((sb, 1, d), lambda s, t: (s, 0, 0)),
            pl.BlockSpec((1, d), const),
            pl.BlockSpec((d, d), const),
            pl.BlockSpec((d, 2 * LANE), const),
            pl.BlockSpec((1, LANE), const),
            pl.BlockSpec((tm, tm), const),
            pl.BlockSpec((N_EXPERTS, N_EXPERTS), const),
        ],
        out_specs=(pl.BlockSpec((sb, rb, d), lambda s, t: (s, t, 0)),
                   pl.BlockSpec((tm, d), lambda s, t: (s * nt + t, 0)),
                   pl.BlockSpec((tm, 2 * TOP_K), lambda s, t: (s * nt + t, 0)),
                   pl.BlockSpec((TOP_K, tm), tok),
                   pl.BlockSpec((1, N_EXPERTS, LANE), lambda s, t: (s * nt + t, 0, 0))),
        compiler_params=_cparams("arbitrary", "arbitrary"),
        name="outproj",
    )(mix, x, g1, sc2, sh2, n2, wo, wr, br, upper, lower)


def _unit_copy(src_ref, src_unit, dst_ref, dst_unit, sem):
    return pltpu.make_async_copy(src_ref.at[src_unit], dst_ref.at[dst_unit], sem)


def _piece_copy(size, src_ref, src_unit, dst_ref, dst_unit, sem):
    return pltpu.make_async_copy(src_ref.at[pl.ds(src_unit, size)], dst_ref.at[pl.ds(dst_unit, size)], sem)


def _max_pieces(size):
    return LOCAL_ROWS // (ROW_UNIT * size) if size == PIECE_SIZES[0] else N_EXPERTS


def _start_run_copies(tabs, tile, loc_ref, glob_ref, sem, to_global):
    for ci, size in enumerate(PIECE_SIZES):
        loc_tab, glob_tab, n_tab = tabs[3 * ci:3 * ci + 3]
        base = tile * _max_pieces(size)

        def start(q, size=size, loc_tab=loc_tab, glob_tab=glob_tab):
            loc, glob = loc_tab[q], glob_tab[q]
            src, dst = ((loc_ref, loc), (glob_ref, glob)) if to_global else ((glob_ref, glob), (loc_ref, loc))
            _piece_copy(size, src[0], src[1], dst[0], dst[1], sem).start()

        n = n_tab[tile]
        if ci == 0:
            def issue_two(j, carry, start=start, base=base):
                start(base + 2 * j)
                start(base + 2 * j + 1)
                return carry

            lax.fori_loop(0, n // 2, issue_two, 0)
            pl.when(n % 2 == 1)(functools.partial(start, base + n - 1))
        else:
            def issue_one(j, carry, start=start, base=base):
                start(base + j)
                return carry

            lax.fori_loop(0, n, issue_one, 0)


def _wait_unit_copies(nun, src_ref, dst_ref, sem):
    def drain_group(j, carry):
        pltpu.make_async_copy(src_ref.at[pl.ds(0, WAIT_GROUP)], dst_ref.at[pl.ds(0, WAIT_GROUP)], sem).wait()
        return carry

    def drain_unit(j, carry):
        _unit_copy(src_ref, 0, dst_ref, 0, sem).wait()
        return carry

    lax.fori_loop(0, nun // WAIT_GROUP, drain_group, 0)
    lax.fori_loop(0, nun % WAIT_GROUP, drain_unit, 0)


def _dispatch_body(tabs, nun_ref, slot_ref, h2_ref, xs_ref, xloc, sem):
    i = pl.program_id(0)
    last = pl.num_programs(0) - 1
    tb = slot_ref.shape[1]
    nun = nun_ref[i]
    s = slot_ref[...]
    h2 = h2_ref[...]
    buf = i % 2

    def sort_chunk(c):
        r = lax.broadcasted_iota(jnp.int32, (SORT_CHUNK, tb), 0) + c * SORT_CHUNK
        hit = (s[0:1] == r) | (s[1:2] == r) | (s[2:3] == r) | (s[3:4] == r)
        p = jnp.where(hit, 1.0, 0.0).astype(BF16)
        units = SORT_CHUNK // ROW_UNIT
        xloc[buf, c * units:(c + 1) * units] = _dot(p, h2).astype(BF16).reshape(units, ROW_UNIT, h2.shape[1])

    typical_rows = TOP_K * tb + N_EXPERTS * ROW_UNIT // 2
    for c in range(LOCAL_ROWS // SORT_CHUNK):
        if (c + 1) * SORT_CHUNK <= typical_rows:
            sort_chunk(c)
        else:
            pl.when(c * (SORT_CHUNK // ROW_UNIT) < nun)(functools.partial(sort_chunk, c))

    _start_run_copies(tabs, i, xloc.at[buf], xs_ref, sem.at[buf], to_global=True)

    @pl.when(i > 0)
    def _():
        _wait_unit_copies(nun_ref[i - 1], xloc.at[1 - buf], xs_ref, sem.at[1 - buf])

    @pl.when(i == last)
    def _():
        _wait_unit_copies(nun, xloc.at[buf], xs_ref, sem.at[buf])


def _dispatch_first_kernel(*refs):
    tabs, (nun_ref, slot_ref, h2_ref, xs_ref, xloc, sem) = refs[:N_RUN_TABLES], refs[N_RUN_TABLES:]
    _dispatch_body(tabs, nun_ref, slot_ref, h2_ref, xs_ref, xloc, sem)


def _dispatch_next_kernel(*refs):
    tabs, (nun_ref, slot_ref, h2_ref, _, xs_ref, xloc, sem) = refs[:N_RUN_TABLES], refs[N_RUN_TABLES:]
    _dispatch_body(tabs, nun_ref, slot_ref, h2_ref, xs_ref, xloc, sem)


def _dispatch(tabs, nun, slot, h2, xs, n_rows, tb):
    T, d = h2.shape
    any_spec = pl.BlockSpec(memory_space=pl.ANY)
    in_specs = [pl.BlockSpec((TOP_K, tb), lambda i, *_: (0, i)),
                pl.BlockSpec((tb, d), lambda i, *_: (i, 0))]
    aliases = {}
    body = _dispatch_first_kernel
    args = tuple(tabs) + (nun, slot, h2)
    if xs is not None:
        in_specs.append(any_spec)
        aliases = {len(args): 0}
        body = _dispatch_next_kernel
        args = args + (xs,)
    return pl.pallas_call(
        body,
        out_shape=jax.ShapeDtypeStruct((n_rows // ROW_UNIT, ROW_UNIT, d), BF16),
        grid_spec=pltpu.PrefetchScalarGridSpec(
            num_scalar_prefetch=N_RUN_TABLES + 1, grid=(T // tb,), in_specs=in_specs, out_specs=any_spec,
            scratch_shapes=[pltpu.VMEM((2, LOCAL_ROWS // ROW_UNIT, ROW_UNIT, d), BF16),
                            pltpu.SemaphoreType.DMA((2,))]),
        input_output_aliases=aliases,
        compiler_params=pltpu.CompilerParams(dimension_semantics=("arbitrary",), has_side_effects=True,
                                             vmem_limit_bytes=VMEM_LIMIT),
        name="dispatch_next" if xs is not None else "dispatch_first",
    )(*args)


def _experts_kernel(te_ref, tv_ref, tf_ref, tn_ref, xs_ref, wg_hbm, bg_ref, wu_hbm, bu_ref, wd_hbm, bd_ref, y_ref,
                    stage_g, stage_u, stage_d, wgb, wub, wdb, sem):
    i = pl.program_id(0)
    valid = tv_ref[i]
    tm, d = xs_ref.shape
    ff = wgb.shape[1]

    def weight_copies(e):
        return [pltpu.make_async_copy(src.at[e], dst, sem.at[j])
                for j, (src, dst) in enumerate(((wg_hbm, stage_g), (wu_hbm, stage_u), (wd_hbm, stage_d)))]

    @pl.when(i == 0)
    def _():
        for cp in weight_copies(te_ref[0]):
            cp.start()

    @pl.when(tf_ref[i] == 1)
    def _():
        for cp in weight_copies(te_ref[i]):
            cp.wait()
        wgb[...] = stage_g[...].astype(BF16)
        wub[...] = stage_u[...].astype(BF16)
        wdb[...] = stage_d[...].astype(BF16)

        @pl.when(tn_ref[i] >= 0)
        def _():
            for cp in weight_copies(tn_ref[i]):
                cp.start()

    @pl.when(valid > 0)
    def _():
        row = lax.broadcasted_iota(jnp.int32, (tm, 1), 0)
        xb = jnp.where(row < valid, xs_ref[...], jnp.zeros((), BF16))
        acc = jnp.zeros((tm, d), F32)
        for c in range(ff // FF_CHUNK):
            cs = slice(c * FF_CHUNK, (c + 1) * FF_CHUNK)
            gate = jnp.minimum(_dot(xb, wgb[:, cs]) + bg_ref[0, :, cs], SWIGLU_LIMIT)
            up = jnp.clip(_dot(xb, wub[:, cs]) + bu_ref[0, :, cs], -SWIGLU_LIMIT, SWIGLU_LIMIT)
            act = (up + 1.0) * gate * jax.nn.sigmoid(SWIGLU_ALPHA * gate)
            acc = acc + _dot(act.astype(BF16), wdb[cs, :])
        y_ref[...] = (acc + bd_ref[0]).astype(BF16)

    @pl.when(valid == 0)
    def _():
        y_ref[...] = jnp.zeros_like(y_ref)


def _experts(tile_expert, tile_valid, tile_first, tile_next, xs, wg, bg, wu, bu, wd, bd, tmg):
    n_tiles = tile_expert.shape[0]
    _, d, ff = wg.shape
    bspec = lambda shp: pl.BlockSpec(shp, lambda i, te, *_: (te[i], 0, 0))
    any_spec = pl.BlockSpec(memory_space=pl.ANY)
    return pl.pallas_call(
        _experts_kernel,
        out_shape=jax.ShapeDtypeStruct(xs.shape, BF16),
        grid_spec=pltpu.PrefetchScalarGridSpec(
            num_scalar_prefetch=4,
            grid=(n_tiles,),
            in_specs=[
                pl.BlockSpec((tmg, d), lambda i, te, tv, *_: (jnp.where(tv[i] > 0, i, 0), 0)),
                any_spec, bspec((1, 1, ff)),
                any_spec, bspec((1, 1, ff)),
                any_spec, bspec((1, 1, d)),
            ],
            out_specs=pl.BlockSpec((tmg, d), lambda i, *_: (i, 0)),
            scratch_shapes=[pltpu.VMEM((d, ff), F32), pltpu.VMEM((d, ff), F32), pltpu.VMEM((ff, d), F32),
                            pltpu.VMEM((d, ff), BF16), pltpu.VMEM((d, ff), BF16), pltpu.VMEM((ff, d), BF16),
                            pltpu.SemaphoreType.DMA((3,))],
        ),
        compiler_params=pltpu.CompilerParams(dimension_semantics=("arbitrary",), vmem_limit_bytes=EXPERTS_VMEM_LIMIT),
        name="experts",
    )(tile_expert, tile_valid, tile_first, tile_next, xs, wg, bg, wu, bu, wd, bd)


def _combine_kernel(*refs):
    tabs = refs[:N_RUN_TABLES]
    nun_ref, y_ref, route_ref, xmid_ref, g2_ref, nf_ref, o_ref, ybuf, sem = refs[N_RUN_TABLES:]
    sb, rb, d = xmid_ref.shape
    tb = sb * rb
    i = pl.program_id(0) * pl.num_programs(1) + pl.program_id(1)
    n_steps = pl.num_programs(0) * pl.num_programs(1)
    buf = i % 2

    @pl.when(i == 0)
    def _():
        ybuf[...] = jnp.zeros_like(ybuf)
        _start_run_copies(tabs, 0, ybuf.at[0], y_ref, sem.at[0], to_global=False)

    _wait_unit_copies(nun_ref[i], y_ref, ybuf.at[buf], sem.at[buf])

    @pl.when(i + 1 < n_steps)
    def _():
        _start_run_copies(tabs, i + 1, ybuf.at[1 - buf], y_ref, sem.at[1 - buf], to_global=False)

    chunk_units = PERM_CHUNK // ROW_UNIT

    def sorted_rows(c):
        return ybuf[buf, c * chunk_units:(c + 1) * chunk_units].reshape(PERM_CHUNK, d)

    route = route_ref[...]
    lane = lax.broadcasted_iota(jnp.int32, (tb, PERM_CHUNK), 1).astype(F32)
    wk = [jnp.broadcast_to(route[:, k:k + 1], (tb, PERM_CHUNK)) for k in range(TOP_K)]
    sk = [jnp.broadcast_to(route[:, TOP_K + k:TOP_K + k + 1], (tb, PERM_CHUNK)) for k in range(TOP_K)]

    def weights_chunk(c):
        r = lane + float(c * PERM_CHUNK)
        pw = jnp.zeros((tb, PERM_CHUNK), F32)
        for k in range(TOP_K):
            pw = jnp.where(sk[k] == r, wk[k], pw)
        return pw.astype(BF16)

    def contribution(c):
        return _dot(weights_chunk(c), sorted_rows(c))

    moe = contribution(0)
    for c in range(1, LOCAL_ROWS // PERM_CHUNK):
        moe = moe + contribution(c)
    out = xmid_ref[...] + g2_ref[...] * moe.reshape(sb, rb, d)
    ms = jnp.mean(out * out, axis=-1, keepdims=True)
    o_ref[...] = out * lax.rsqrt(ms + EPS) * nf_ref[...]


def _combine(tabs, nun, y, route, xmid, g2, nf, sb, rb):
    nseq, L, d = xmid.shape
    tb = sb * rb
    nt = L // rb
    tok = lambda s, t, *_: (s * nt + t, 0)
    return pl.pallas_call(
        _combine_kernel,
        out_shape=jax.ShapeDtypeStruct((nseq, L, d), F32),
        grid_spec=pltpu.PrefetchScalarGridSpec(
            num_scalar_prefetch=N_RUN_TABLES + 1,
            grid=(nseq // sb, nt),
            in_specs=[
                pl.BlockSpec(memory_space=pl.ANY),
                pl.BlockSpec((tb, 2 * TOP_K), tok),
                pl.BlockSpec((sb, rb, d), lambda s, t, *_: (s, t, 0)),
                pl.BlockSpec((sb, 1, d), lambda s, t, *_: (s, 0, 0)),
                pl.BlockSpec((1, d), lambda s, t, *_: (0, 0)),
            ],
            out_specs=pl.BlockSpec((sb, rb, d), lambda s, t, *_: (s, t, 0)),
            scratch_shapes=[pltpu.VMEM((2, LOCAL_ROWS // ROW_UNIT, ROW_UNIT, d), BF16),
                            pltpu.SemaphoreType.DMA((2,))],
        ),
        compiler_params=_cparams("arbitrary", "arbitrary"),
        name="combine",
    )(*tabs, nun, y, route, xmid, g2, nf)


def _tile_rows(nseq, L, tile):
    if L >= tile:
        assert L % tile == 0
        return 1, tile
    assert tile % L == 0 and nseq % (tile // L) == 0
    return tile // L, L


def kernel(x_prompt, x_sample, state_gla, c_prompt, c_sample, w_ada, b_ada, norm1, w_in, w_gk, b_gk, gla_norm,
           gmlp_ln_g, gmlp_ln_b, gmlp_w_s, gmlp_b_s, w_out, norm2, w_router, b_router, w_gate, b_gate, w_up,
           b_up, w_down, b_down, norm_f):
    depth = w_ada.shape[0]
    assert depth == 1
    bp, lp, d = x_prompt.shape
    bs, ls, _ = x_sample.shape
    tp, ts = bp * lp, bs * ls

    nc = bp + bs
    ncp = -(-nc // SUBLANE) * SUBLANE
    c_all = jnp.concatenate([c_prompt, c_sample, jnp.zeros((ncp - nc, d), F32)], axis=0)
    mod = _ada(c_all, w_ada[0], b_ada[0][None]).reshape(ncp, N_MOD, 1, d)
    mods_p = [mod[:bp, i] for i in range(N_MOD)]
    mods_s = [mod[bp:nc, i] for i in range(N_MOD)]

    wi = w_in[0]
    c_lr = 2 * GLA_QK + GLA_WIDTH
    c_r = c_lr + GLA_LOWRANK
    wm = jnp.concatenate([wi[:, :c_lr], wi[:, c_r:]], axis=1).astype(BF16)
    wlr = jnp.pad(wi[:, c_lr:c_r], ((0, 0), (0, LANE - GLA_LOWRANK))).astype(BF16)
    wgk_f = jnp.pad(w_gk[0], ((0, LANE - GLA_LOWRANK), (0, 0)))
    wgk_hi = wgk_f.astype(BF16)
    wgk = jnp.concatenate([wgk_hi, (wgk_f - wgk_hi.astype(F32)).astype(BF16)], axis=1)
    bgk = b_gk[0][None]
    n1, n2, nf = norm1[0][None], norm2[0][None], norm_f[None]
    gn, lng, lnb = gla_norm[0][None], gmlp_ln_g[0][None], gmlp_ln_b[0][None]
    ws, bsv = gmlp_w_s[0], gmlp_b_s[0]
    pos_i = jnp.arange(GMLP_BLOCK)
    cmask = (pos_i[None, :] // CHUNK) <= (pos_i[:, None] // CHUNK)
    wm_p = jnp.where(cmask[None], ws, 0.0).astype(BF16)
    bsb_p = jnp.repeat(bsv.T, GMLP_GC, axis=1)
    reps = ROW_BLOCK // ls
    eye = jnp.eye(reps, dtype=F32)
    wm_s = jnp.einsum("ab,gij->gaibj", eye, ws[:, :ls, :ls]).reshape(GMLP_GROUPS, ROW_BLOCK, ROW_BLOCK).astype(BF16)
    bsb_s = jnp.tile(jnp.repeat(bsv[:, :ls].T, GMLP_GC, axis=1), (reps, 1))
    wo = w_out[0].astype(BF16)
    wr_f = jnp.pad(w_router[0], ((0, 0), (0, LANE - N_EXPERTS)))
    wr_hi = wr_f.astype(BF16)
    wr = jnp.concatenate([wr_hi, (wr_f - wr_hi.astype(F32)).astype(BF16)], axis=1)
    br = jnp.concatenate([b_router[0], jnp.full((LANE - N_EXPERTS,), -1e30, F32)])[None]
    upper = (jnp.arange(TOKEN_TILE)[:, None] < jnp.arange(TOKEN_TILE)[None, :]).astype(BF16)
    lower = (jnp.arange(N_EXPERTS)[None, :] < jnp.arange(N_EXPERTS)[:, None]).astype(F32)
    wg, wu, wd = w_gate[0], w_up[0], w_down[0]
    bg, bu, bd = b_gate[0][:, None], b_up[0][:, None], b_down[0][:, None]

    sbp, rbp = _tile_rows(bp, lp, TOKEN_TILE)
    sbs, rbs = _tile_rows(bs, ls, TOKEN_TILE)

    assert lp % TOKEN_TILE == 0
    mix_p, state_p = _mixer_prompt(x_prompt, mods_p[1], mods_p[0], n1, wm, wlr, wgk, bgk, gn, lng, lnb, wm_p, bsb_p,
                                   TOKEN_TILE)
    proj_s, gk_s = _inproj(x_sample, mods_s[1], mods_s[0], n1, wm, wlr, wgk, bgk, sbs, rbs)
    mix_s, state_s, vn_s = _mixer_sample(proj_s, gk_s, state_gla[0], gn, lng, lnb, wm_s, bsb_s)

    xmid_p, h2_p, route_p, slot_p, n16_p = _outproj(
        mix_p, x_prompt, mods_p[2], mods_p[4], mods_p[3], n2, wo, wr, br, upper, lower, sbp, rbp)
    xmid_s, h2_s, route_s, slot_s, n16_s = _outproj(
        mix_s, x_sample, mods_s[2], mods_s[4], mods_s[3], n2, wo, wr, br, upper, lower, sbs, rbs)

    tmg = EXPERT_TILE
    ntp = tp // TOKEN_TILE
    eids = jnp.arange(N_EXPERTS, dtype=jnp.int32)
    n16 = jnp.concatenate([n16_p[:, :, 0], n16_s[:, :, 0]], axis=0).astype(jnp.int32)
    nt_all = n16.shape[0]
    earlier = jnp.cumsum(n16, axis=0) - n16
    tot = jnp.sum(n16, axis=0)
    tiles_e = (tot + tmg - 1) // tmg
    tile_end = jnp.cumsum(tiles_e)
    tile_start = tile_end - tiles_e
    row_start = tile_start * tmg
    n_tiles = (TOP_K * (tp + ts) + nt_all * N_EXPERTS * (ROW_UNIT - 1)) // tmg + N_EXPERTS
    tid = jnp.arange(n_tiles, dtype=jnp.int32)
    te = jnp.minimum(jnp.sum((tid[:, None] >= tile_end[None, :]).astype(jnp.int32), axis=1), N_EXPERTS - 1)
    te_hot = te[:, None] == eids[None, :]
    tot_te = jnp.sum(jnp.where(te_hot, tot[None, :], 0), axis=1)
    start_te = jnp.sum(jnp.where(te_hot, tile_start[None, :], 0), axis=1)
    active = tid < tile_end[-1]
    tv = jnp.where(active, jnp.clip(tot_te - (tid - start_te) * tmg, 0, tmg), 0).astype(jnp.int32)
    last_e = jnp.max(jnp.where(tiles_e > 0, eids, 0)).astype(jnp.int32)
    te = jnp.where(active, te, last_e).astype(jnp.int32)
    te_prev = jnp.concatenate([jnp.full((1,), -1, jnp.int32), te[:-1]])
    tf = (active & (te != te_prev)).astype(jnp.int32)
    later_used = (eids[None, :] > te[:, None]) & (tiles_e[None, :] > 0)
    tn = jnp.min(jnp.where(later_used, eids[None, :], N_EXPERTS), axis=1)
    tn = jnp.where(tn < N_EXPERTS, tn, -1).astype(jnp.int32)

    run_units = n16 // ROW_UNIT
    run_loc = (jnp.cumsum(n16, axis=1) - n16) // ROW_UNIT
    run_glob = (row_start[None, :] + earlier) // ROW_UNIT
    nun = jnp.sum(run_units, axis=1).astype(jnp.int32)

    def piece_tables(cnt, loc0, glob0, stride, max_pieces):
        end = jnp.cumsum(cnt, axis=1)
        start = end - cnt
        piece = jnp.arange(max_pieces, dtype=jnp.int32)
        run = jnp.sum((piece[None, :, None] >= end[:, None, :]).astype(jnp.int32), axis=2)
        hot = run[:, :, None] == eids[None, None, :]
        pick = lambda a: jnp.sum(jnp.where(hot, a[:, None, :], 0), axis=2)
        within = (piece[None, :] - pick(start)) * stride
        live = piece[None, :] < end[:, -1:]
        loc = jnp.where(live, pick(loc0) + within, 0).astype(jnp.int32)
        glob = jnp.where(live, pick(glob0) + within, 0).astype(jnp.int32)
        return loc, glob, end[:, -1].astype(jnp.int32)

    tables = []
    done = jnp.zeros_like(run_units)
    for size in PIECE_SIZES:
        cnt = (run_units - done) // size
        tables.append(piece_tables(cnt, run_loc + done, run_glob + done, size, _max_pieces(size)))
        done = done + cnt * size

    def tables_for(tiles):
        return tuple(a[tiles].reshape(-1) for tab in tables for a in tab)

    tabs_p, tabs_s = tables_for(slice(0, ntp)), tables_for(slice(ntp, nt_all))

    n_rows = n_tiles * tmg
    xs = _dispatch(tabs_p, nun[:ntp], slot_p, h2_p, None, n_rows, TOKEN_TILE)
    xs = _dispatch(tabs_s, nun[ntp:], slot_s, h2_s, xs, n_rows, TOKEN_TILE)
    y = _experts(te, tv, tf, tn, xs.reshape(n_rows, d), wg, bg, wu, bu, wd, bd, tmg)
    y = y.reshape(n_rows // ROW_UNIT, ROW_UNIT, d)
    y_prompt = _combine(tabs_p, nun[:ntp], y, route_p, xmid_p, mods_p[5], nf, sbp, rbp)
    y_sample = _combine(tabs_s, nun[ntp:], y, route_s, xmid_s, mods_s[5], nf, sbs, rbs)

    return (y_prompt, y_sample, state_p[None], state_s[None], vn_s[None])
```

```python
import functools

import jax
import jax.numpy as jnp
from jax import lax
from jax.experimental import pallas as pl
from jax.experimental.pallas import tpu as pltpu

F32 = jnp.float32
BF16 = jnp.bfloat16
HIGHEST = lax.Precision.HIGHEST

CHUNK = 64
GLA_HEADS = 4
GLA_DK = 64
GLA_DV = 128
GLA_QK = GLA_HEADS * GLA_DK
GLA_WIDTH = GLA_HEADS * GLA_DV
GLA_LOWRANK = 16
GLA_GATE_NORM = 16.0
GATE_SAFE_MIN = -60.0
GMLP_WIDTH = 512
GMLP_GROUPS = 4
GMLP_GC = GMLP_WIDTH // GMLP_GROUPS
GMLP_BLOCK = 128
N_EXPERTS = 32
TOP_K = 4
SWIGLU_LIMIT = 7.0
SWIGLU_ALPHA = 1.702
EPS = 1e-6
N_MOD = 6

LANE = 128
SUBLANE = 8

Q0 = 0
K0 = Q0 + GLA_QK
V0 = K0 + GLA_QK
R0 = V0 + GLA_WIDTH
U0 = R0 + GLA_WIDTH
G0 = U0 + GMLP_WIDTH
PROJ_COLS = G0 + GMLP_WIDTH

ROW_BLOCK = 128
TOKEN_TILE = 512
EXPERT_TILE = 1024
FF_CHUNK = 512
ROW_UNIT = 16
LOCAL_ROWS = 2560
WAIT_GROUP = 8
PIECE_SIZES = (4, 2, 1)
N_RUN_TABLES = 3 * len(PIECE_SIZES)
SORT_CHUNK = 128
PERM_CHUNK = 256
VMEM_LIMIT = 48 * 1024 * 1024
EXPERTS_VMEM_LIMIT = 56 * 1024 * 1024


def _cparams(*sem):
    return pltpu.CompilerParams(dimension_semantics=sem, vmem_limit_bytes=VMEM_LIMIT)


def _dot(a, b, **kw):
    return jnp.dot(a, b, preferred_element_type=F32, **kw)


def _dot_nt(a, b):
    return lax.dot_general(a, b, (((1,), (1,)), ((), ())), preferred_element_type=F32)


def _gelu(x):
    return 0.5 * x * (1.0 + lax.erf(x * (0.5 ** 0.5)))


def _dot_tn(a, b):
    return lax.dot_general(a, b, (((0,), (0,)), ((), ())), preferred_element_type=F32)


def _ada_kernel(c_ref, w_ref, b_ref, o_ref):
    c = c_ref[...]
    s = c * jax.nn.sigmoid(c)
    o_ref[...] = _dot(s, w_ref[...], precision=HIGHEST) + b_ref[...]


def _ada(c, w_ada, b_ada):
    n, d = c.shape
    return pl.pallas_call(
        _ada_kernel,
        out_shape=jax.ShapeDtypeStruct((n, N_MOD * d), F32),
        grid=(N_MOD,),
        in_specs=[
            pl.BlockSpec((n, d), lambda j: (0, 0)),
            pl.BlockSpec((d, d), lambda j: (0, j)),
            pl.BlockSpec((1, d), lambda j: (0, j)),
        ],
        out_specs=pl.BlockSpec((n, d), lambda j: (0, j)),
        compiler_params=_cparams("arbitrary"),
        name="ada",
    )(c, w_ada, b_ada)


PROJ_COL_CHUNK = 512


def _modulated_norm(x_ref, sc_ref, sh_ref, n1_ref):
    sb, rb, d = x_ref.shape
    x = x_ref[...]
    ms = jnp.mean(x * x, axis=-1, keepdims=True)
    h = x * lax.rsqrt(ms + EPS) * n1_ref[...]
    h = h * (1.0 + sc_ref[...]) + sh_ref[...]
    return h.reshape(sb * rb, d).astype(BF16)


def _gate_preact(hb, wlr_ref, wgk_ref, bgk_ref):
    lr = _dot(hb, wlr_ref[...])
    lr_hi = lr.astype(BF16)
    lr_lo = (lr - lr_hi.astype(F32)).astype(BF16)
    part = _dot(lr_hi, wgk_ref[...])
    return part[:, :GLA_QK] + part[:, GLA_QK:] + _dot(lr_lo, wgk_ref[:, :GLA_QK]) + bgk_ref[...]


def _inproj_kernel(x_ref, sc_ref, sh_ref, n1_ref, wm_ref, wlr_ref, wgk_ref, bgk_ref, proj_ref, gk_ref):
    sb, rb, _ = x_ref.shape
    hb = _modulated_norm(x_ref, sc_ref, sh_ref, n1_ref)
    cw = PROJ_COL_CHUNK
    for c in range(PROJ_COLS // cw):
        p = _dot(hb, wm_ref[:, c * cw:(c + 1) * cw])
        proj_ref[:, :, c * cw:(c + 1) * cw] = p.astype(BF16).reshape(sb, rb, cw)
    gk_ref[...] = _gate_preact(hb, wlr_ref, wgk_ref, bgk_ref).reshape(sb, rb, GLA_QK)


def _inproj(x, sc, sh, n1, wm, wlr, wgk, bgk, sb, rb):
    nseq, L, d = x.shape
    grid = (nseq // sb, L // rb)
    const = lambda s, t: (0, 0)
    return pl.pallas_call(
        _inproj_kernel,
        out_shape=(jax.ShapeDtypeStruct((nseq, L, PROJ_COLS), BF16),
                   jax.ShapeDtypeStruct((nseq, L, GLA_QK), F32)),
        grid=grid,
        in_specs=[
            pl.BlockSpec((sb, rb, d), lambda s, t: (s, t, 0)),
            pl.BlockSpec((sb, 1, d), lambda s, t: (s, 0, 0)),
            pl.BlockSpec((sb, 1, d), lambda s, t: (s, 0, 0)),
            pl.BlockSpec((1, d), const),
            pl.BlockSpec((d, PROJ_COLS), const),
            pl.BlockSpec((d, LANE), const),
            pl.BlockSpec((LANE, 2 * GLA_QK), const),
            pl.BlockSpec((1, GLA_QK), const),
        ],
        out_specs=(pl.BlockSpec((sb, rb, PROJ_COLS), lambda s, t: (s, t, 0)),
                   pl.BlockSpec((sb, rb, GLA_QK), lambda s, t: (s, t, 0))),
        compiler_params=_cparams("arbitrary", "arbitrary"),
        name="inproj",
    )(x, sc, sh, n1, wm, wlr, wgk, bgk)


def _head_masks():
    lane = lax.broadcasted_iota(jnp.int32, (1, GLA_QK), 1)
    return [(lane // GLA_DK) == h for h in range(GLA_HEADS)]


def _stack_heads(x, hm):
    return jnp.concatenate([jnp.where(m, x, 0.0) for m in hm], axis=0).astype(BF16)


def _chunk_pair_ids(c_len):
    ti = lax.broadcasted_iota(jnp.int32, (GLA_HEADS * c_len, c_len), 0) % c_len
    si = lax.broadcasted_iota(jnp.int32, (GLA_HEADS * c_len, c_len), 1)
    return ti, si


def _scores_factored(q4s, k, G, c_len):
    ti, si = _chunk_pair_ids(c_len)
    ke = (k * jnp.exp(-G)).astype(BF16)
    return jnp.concatenate(
        [jnp.where(si <= ti, _dot_nt(q4, ke[c * c_len:(c + 1) * c_len]), 0.0) for c, q4 in enumerate(q4s)], axis=0)


def _scores_bounded(qs, k, G, c_len, hm):
    ti, si = _chunk_pair_ids(c_len)
    t = lax.broadcasted_iota(jnp.int32, (c_len, 1), 0)
    col = lax.broadcasted_iota(jnp.int32, (c_len, c_len), 1)
    outs = []
    for c in range(qs.shape[0] // c_len):
        sl = slice(c * c_len, (c + 1) * c_len)
        q_c, k_c, g_c = qs[sl], k[sl], G[sl]
        a = jnp.where(si == ti, _dot_nt(_stack_heads(q_c, hm), k_c.astype(BF16)), 0.0)
        half = c_len // 2
        while half >= 1:
            blk = 2 * half
            sel = (col == (t // blk) * blk + (half - 1)).astype(F32)
            ref = _dot(sel, g_c, precision=HIGHEST)
            upper = (t % blk) >= half
            qh = jnp.where(upper, q_c * jnp.exp(jnp.minimum(g_c - ref, 0.0)), 0.0)
            kh = jnp.where(upper, 0.0, k_c * jnp.exp(jnp.minimum(ref - g_c, 0.0)))
            same = (ti // blk) == (si // blk)
            a = a + jnp.where(same, _dot_nt(_stack_heads(qh, hm), kh.astype(BF16)), 0.0)
            half //= 2
        outs.append(a)
    return jnp.concatenate(outs, axis=0)


def _cum_log_gates(gkpre, c_len):
    rows = gkpre.shape[0]
    g = jax.nn.log_sigmoid(gkpre) / GLA_GATE_NORM
    ri = lax.broadcasted_iota(jnp.int32, (rows, rows), 0)
    ci = lax.broadcasted_iota(jnp.int32, (rows, rows), 1)
    tri = jnp.where((ci <= ri) & ((ci // c_len) == (ri // c_len)), 1.0, 0.0).astype(BF16)
    g_hi = g.astype(BF16)
    r1 = g - g_hi.astype(F32)
    g_mid = r1.astype(BF16)
    g_lo = (r1 - g_mid.astype(F32)).astype(BF16)
    parts = _dot(tri, jnp.concatenate([g_hi, g_mid, g_lo], axis=1))
    return parts[:, :GLA_QK] + parts[:, GLA_QK:2 * GLA_QK] + parts[:, 2 * GLA_QK:]


def _factored_is_safe(G, c_len):
    ends = [G[(c + 1) * c_len - 1:(c + 1) * c_len] for c in range(G.shape[0] // c_len)]
    return jnp.min(jnp.concatenate(ends, axis=0)) > GATE_SAFE_MIN


def _mixer_block(p, G, states, c_len, gn, lng, lnb, wm_ref, bsb, bounded, side=()):
    rows = p.shape[0]
    n_chunks = rows // c_len
    chained = len(states) == 1
    hm = _head_masks()
    side = list(side)

    def side_step():
        if side:
            side.pop(0)()

    q = p[:, Q0:Q0 + GLA_QK].astype(F32)
    k = p[:, K0:K0 + GLA_QK].astype(F32)
    v = p[:, V0:V0 + GLA_WIDTH]
    r = p[:, R0:R0 + GLA_WIDTH].astype(F32)

    qs = q * (GLA_DK ** -0.5)
    qe = qs * jnp.exp(G)
    q4s = [_stack_heads(qe[c * c_len:(c + 1) * c_len], hm) for c in range(n_chunks)]
    scores = _scores_bounded(qs, k, G, c_len, hm) if bounded else _scores_factored(q4s, k, G, c_len)
    hc = GLA_HEADS * c_len

    new_states = []
    o_rows = []
    st = states[0]
    for c in range(n_chunks):
        lo, hi = c * c_len, (c + 1) * c_len
        if not chained:
            st = states[c]
        q4 = q4s[c]
        a = scores[c * hc:(c + 1) * hc].astype(BF16)
        o_inter = _dot_nt(q4, st.astype(BF16))
        v_c = v[lo:hi]
        heads = []
        for h in range(GLA_HEADS):
            o_h = o_inter[h * c_len:(h + 1) * c_len] + _dot(
                a[h * c_len:(h + 1) * c_len], v_c[:, h * GLA_DV:(h + 1) * GLA_DV])
            heads.append(o_h)
        o_rows.append(jnp.concatenate(heads, axis=1))
        g_last = G[hi - 1:hi]
        kd = (k[lo:hi] * jnp.exp(g_last - G[lo:hi])).astype(BF16)
        upd = _dot_tn(v_c, kd)
        st_new = jnp.exp(g_last) * st
        for h in range(GLA_HEADS):
            st_new = st_new + jnp.where(hm[h], upd[h * GLA_DV:(h + 1) * GLA_DV], 0.0)
        if chained:
            st = st_new
        else:
            new_states.append(st_new)
        side_step()
    if chained:
        new_states = [st]
    o = jnp.concatenate(o_rows, axis=0)

    gla = []
    for h in range(GLA_HEADS):
        o_h = o[:, h * GLA_DV:(h + 1) * GLA_DV]
        ms = jnp.mean(o_h * o_h, axis=-1, keepdims=True)
        r_h = r[:, h * GLA_DV:(h + 1) * GLA_DV]
        gla.append(o_h * lax.rsqrt(ms + EPS) * gn * (r_h * jax.nn.sigmoid(r_h)))
    side_step()

    u = _gelu(p[:, U0:U0 + GMLP_WIDTH].astype(F32))
    side_step()
    vv = _gelu(p[:, G0:G0 + GMLP_WIDTH].astype(F32))
    side_step()
    mu = jnp.mean(vv, axis=-1, keepdims=True)
    xc = vv - mu
    var = jnp.mean(xc * xc, axis=-1, keepdims=True)
    vn = xc * lax.rsqrt(var + EPS) * lng + lnb
    vnb = vn.astype(BF16)
    gm = []
    for gi in range(GMLP_GROUPS):
        sl = slice(gi * GMLP_GC, (gi + 1) * GMLP_GC)
        mixed = _dot(wm_ref[gi], vnb[:, sl]) + bsb[:, sl]
        gm.append(u[:, sl] * mixed)
    out = jnp.concatenate(gla + gm, axis=1)
    while side:
        side_step()
    return out, new_states, vn


def _mixer_prompt_kernel(x_ref, sc_ref, sh_ref, n1_ref, wm_ref, wlr_ref, wgk_ref, bgk_ref,
                         gn_ref, lng_ref, lnb_ref, ws_ref, bsb_ref, mix_ref, s_ref, st_scr):
    t = pl.program_id(1)
    nt = pl.num_programs(1)

    @pl.when(t == 0)
    def _():
        st_scr[...] = jnp.zeros_like(st_scr)

    tb = x_ref.shape[1]
    n_sub = tb // ROW_BLOCK
    cw = PROJ_COL_CHUNK
    hb = _modulated_norm(x_ref, sc_ref, sh_ref, n1_ref)

    def proj_steps(j, parts):
        hb_j = hb[j * ROW_BLOCK:(j + 1) * ROW_BLOCK]
        return [functools.partial(
            lambda c: parts.append(_dot(hb_j, wm_ref[:, c * cw:(c + 1) * cw]).astype(BF16)), c)
            for c in range(PROJ_COLS // cw)]

    gk = _gate_preact(hb, wlr_ref, wgk_ref, bgk_ref)
    first = []
    for step in proj_steps(0, first):
        step()
    Gs = [_cum_log_gates(gk[j * ROW_BLOCK:(j + 1) * ROW_BLOCK], CHUNK) for j in range(n_sub)]
    safe = _factored_is_safe(jnp.concatenate(Gs, axis=0), CHUNK)

    def run(bounded):
        st = st_scr[...]
        parts = first
        for j in range(n_sub):
            p_j = jnp.concatenate(parts, axis=1)
            parts = []
            side = proj_steps(j + 1, parts) if j + 1 < n_sub else []
            out, sts, _ = _mixer_block(p_j, Gs[j], [st], CHUNK, gn_ref[...], lng_ref[...], lnb_ref[...], ws_ref,
                                       bsb_ref[...], bounded, side)
            st = sts[0]
            mix_ref[0, j * ROW_BLOCK:(j + 1) * ROW_BLOCK, :] = out.astype(BF16)
        st_scr[...] = st

    pl.when(safe)(functools.partial(run, False))
    pl.when(jnp.logical_not(safe))(functools.partial(run, True))

    @pl.when(t == nt - 1)
    def _():
        s_ref[0] = st_scr[...].T.reshape(GLA_HEADS, GLA_DK, GLA_DV)


def _mixer_prompt(x, sc, sh, n1, wm, wlr, wgk, bgk, gn, lng, lnb, ws, bsb, tb):
    b, L, d = x.shape
    const2 = lambda s, t: (0, 0)
    return pl.pallas_call(
        _mixer_prompt_kernel,
        out_shape=(jax.ShapeDtypeStruct((b, L, 2 * GLA_WIDTH), BF16),
                   jax.ShapeDtypeStruct((b, GLA_HEADS, GLA_DK, GLA_DV), F32)),
        grid=(b, L // tb),
        in_specs=[
            pl.BlockSpec((1, tb, d), lambda s, t: (s, t, 0)),
            pl.BlockSpec((1, 1, d), lambda s, t: (s, 0, 0)),
            pl.BlockSpec((1, 1, d), lambda s, t: (s, 0, 0)),
            pl.BlockSpec((1, d), const2),
            pl.BlockSpec((d, PROJ_COLS), const2),
            pl.BlockSpec((d, LANE), const2),
            pl.BlockSpec((LANE, 2 * GLA_QK), const2),
            pl.BlockSpec((1, GLA_QK), const2),
            pl.BlockSpec((1, GLA_DV), const2),
            pl.BlockSpec((1, GMLP_WIDTH), const2),
            pl.BlockSpec((1, GMLP_WIDTH), const2),
            pl.BlockSpec((GMLP_GROUPS, ROW_BLOCK, ROW_BLOCK), lambda s, t: (0, 0, 0)),
            pl.BlockSpec((ROW_BLOCK, GMLP_WIDTH), const2),
        ],
        out_specs=(pl.BlockSpec((1, tb, 2 * GLA_WIDTH), lambda s, t: (s, t, 0)),
                   pl.BlockSpec((1, GLA_HEADS, GLA_DK, GLA_DV), lambda s, t: (s, 0, 0, 0))),
        scratch_shapes=[pltpu.VMEM((GLA_DV, GLA_QK), F32)],
        compiler_params=_cparams("arbitrary", "arbitrary"),
        name="mixer_prompt",
    )(x, sc, sh, n1, wm, wlr, wgk, bgk, gn, lng, lnb, ws, bsb)


def _mixer_sample_kernel(proj_ref, gk_ref, s0_ref, gn_ref, lng_ref, lnb_ref, wm_ref, bsb_ref,
                         mix_ref, s_ref, vn_ref):
    sb, rb, _ = proj_ref.shape
    G = _cum_log_gates(gk_ref[...].reshape(sb * rb, GLA_QK), rb)
    safe = _factored_is_safe(G, rb)

    def run(bounded):
        p = proj_ref[...].reshape(sb * rb, PROJ_COLS)
        states = [s0_ref[i].reshape(GLA_QK, GLA_DV).T for i in range(sb)]
        out, sts, vn = _mixer_block(p, G, states, rb, gn_ref[...], lng_ref[...], lnb_ref[...], wm_ref,
                                    bsb_ref[...], bounded)
        mix_ref[...] = out.astype(BF16).reshape(sb, rb, 2 * GLA_WIDTH)
        vn_ref[...] = vn.reshape(sb, rb, GMLP_WIDTH)
        for i in range(sb):
            s_ref[i] = sts[i].T.reshape(GLA_HEADS, GLA_DK, GLA_DV)

    pl.when(safe)(functools.partial(run, False))
    pl.when(jnp.logical_not(safe))(functools.partial(run, True))


def _mixer_sample(proj, gk, s0, gn, lng, lnb, wm, bsb):
    n, L, _ = proj.shape
    sb = ROW_BLOCK // L
    const2 = lambda s: (0, 0)
    return pl.pallas_call(
        _mixer_sample_kernel,
        out_shape=(jax.ShapeDtypeStruct((n, L, 2 * GLA_WIDTH), BF16),
                   jax.ShapeDtypeStruct((n, GLA_HEADS, GLA_DK, GLA_DV), F32),
                   jax.ShapeDtypeStruct((n, L, GMLP_WIDTH), F32)),
        grid=(n // sb,),
        in_specs=[
            pl.BlockSpec((sb, L, PROJ_COLS), lambda s: (s, 0, 0)),
            pl.BlockSpec((sb, L, GLA_QK), lambda s: (s, 0, 0)),
            pl.BlockSpec((sb, GLA_HEADS, GLA_DK, GLA_DV), lambda s: (s, 0, 0, 0)),
            pl.BlockSpec((1, GLA_DV), const2),
            pl.BlockSpec((1, GMLP_WIDTH), const2),
            pl.BlockSpec((1, GMLP_WIDTH), const2),
            pl.BlockSpec((GMLP_GROUPS, ROW_BLOCK, ROW_BLOCK), lambda s: (0, 0, 0)),
            pl.BlockSpec((ROW_BLOCK, GMLP_WIDTH), const2),
        ],
        out_specs=(pl.BlockSpec((sb, L, 2 * GLA_WIDTH), lambda s: (s, 0, 0)),
                   pl.BlockSpec((sb, GLA_HEADS, GLA_DK, GLA_DV), lambda s: (s, 0, 0, 0)),
                   pl.BlockSpec((sb, L, GMLP_WIDTH), lambda s: (s, 0, 0))),
        compiler_params=_cparams("arbitrary"),
        name="mixer_sample",
    )(proj, gk, s0, gn, lng, lnb, wm, bsb)


def _outproj_kernel(mix_ref, x_ref, g1_ref, sc_ref, sh_ref, n2_ref, wo_ref, wr_ref, br_ref, upper_ref, lower_ref,
                    xmid_ref, h2_ref, route_ref, slot_ref, n16_ref):
    sb, rb, d = x_ref.shape
    tm = sb * rb

    y = _dot(mix_ref[...].reshape(tm, d), wo_ref[...])
    xm = x_ref[...] + g1_ref[...] * y.reshape(sb, rb, d)
    xmid_ref[...] = xm
    ms = jnp.mean(xm * xm, axis=-1, keepdims=True)
    h2 = xm * lax.rsqrt(ms + EPS) * n2_ref[...]
    h2 = (h2 * (1.0 + sc_ref[...]) + sh_ref[...]).reshape(tm, d)
    h_hi = h2.astype(BF16)
    h2_ref[...] = h_hi

    h_lo = (h2 - h_hi.astype(F32)).astype(BF16)
    part = _dot(h_hi, wr_ref[...])
    logits = part[:, :LANE] + part[:, LANE:] + _dot(h_lo, wr_ref[:, :LANE]) + br_ref[...]
    l = logits.T[:N_EXPERTS]
    eid = lax.broadcasted_iota(jnp.int32, (N_EXPERTS, tm), 0).astype(F32)
    top_l, sel = [], []
    for _ in range(TOP_K):
        m = jnp.max(l, axis=0, keepdims=True)
        idx = jnp.min(jnp.where(l == m, eid, float(N_EXPERTS)), axis=0, keepdims=True)
        hit = eid == idx
        top_l.append(m)
        sel.append(hit)
        l = jnp.where(hit, -jnp.inf, l)
    ex = [jnp.exp(t - top_l[0]) for t in top_l]
    den = ex[0] + ex[1] + ex[2] + ex[3]
    top_w = [e / den for e in ex]

    chosen = (sel[0] | sel[1] | sel[2] | sel[3])
    cb = jnp.where(chosen, 1.0, 0.0)
    before = _dot(cb.astype(BF16), upper_ref[...])
    n = jnp.sum(cb, axis=1, keepdims=True)
    n16 = jnp.floor((n + (ROW_UNIT - 1)) * (1.0 / ROW_UNIT)) * ROW_UNIT
    n16b = jnp.broadcast_to(n16, (N_EXPERTS, LANE))
    ls = _dot(lower_ref[...], n16b, precision=HIGHEST)[:, 0:1]
    base = before + ls
    slots = [jnp.sum(jnp.where(s, base, 0.0), axis=0, keepdims=True) for s in sel]
    slot_ref[...] = jnp.concatenate(slots, axis=0).astype(jnp.int32)
    n16_ref[0] = n16b
    rows = jnp.concatenate(top_w + slots + [jnp.zeros((LANE - 2 * TOP_K, tm), F32)], axis=0)
    route_ref[...] = rows.T[:, :2 * TOP_K]


def _outproj(mix, x, g1, sc2, sh2, n2, wo, wr, br, upper, lower, sb, rb):
    nseq, L, d = x.shape
    tm = sb * rb
    nt = L // rb
    T = nseq * L
    grid = (nseq // sb, nt)
    const = lambda s, t: (0, 0)
    tok = lambda s, t: (0, s * nt + t)
    return pl.pallas_call(
        _outproj_kernel,
        out_shape=(jax.ShapeDtypeStruct((nseq, L, d), F32),
                   jax.ShapeDtypeStruct((T, d), BF16),
                   jax.ShapeDtypeStruct((T, 2 * TOP_K), F32),
                   jax.ShapeDtypeStruct((TOP_K, T), jnp.int32),
                   jax.ShapeDtypeStruct((T // tm, N_EXPERTS, LANE), F32)),
        grid=grid,
        in_specs=[
            pl.BlockSpec((sb, rb, d), lambda s, t: (s, t, 0)),
            pl.BlockSpec((sb, rb, d), lambda s, t: (s, t, 0)),
            pl.BlockSpec((sb, 1, d), lambda s, t: (s, 0, 0)),
            pl.BlockSpec((sb, 1, d), lambda s, t: (s, 0, 0)),
            pl.BlockSpec((sb, 1, d), lambda s, t: (s, 0, 0)),
            pl.BlockSpec((1, d), const),
            pl.BlockSpec((d, d), const),
            pl.BlockSpec((d, 2 * LANE), const),
            pl.BlockSpec((1, LANE), const),
            pl.BlockSpec((tm, tm), const),
            pl.BlockSpec((N_EXPERTS, N_EXPERTS), const),
        ],
        out_specs=(pl.BlockSpec((sb, rb, d), lambda s, t: (s, t, 0)),
                   pl.BlockSpec((tm, d), lambda s, t: (s * nt + t, 0)),
                   pl.BlockSpec((tm, 2 * TOP_K), lambda s, t: (s * nt + t, 0)),
                   pl.BlockSpec((TOP_K, tm), tok),
                   pl.BlockSpec((1, N_EXPERTS, LANE), lambda s, t: (s * nt + t, 0, 0))),
        compiler_params=_cparams("arbitrary", "arbitrary"),
        name="outproj",
    )(mix, x, g1, sc2, sh2, n2, wo, wr, br, upper, lower)


def _unit_copy(src_ref, src_unit, dst_ref, dst_unit, sem):
    return pltpu.make_async_copy(src_ref.at[src_unit], dst_ref.at[dst_unit], sem)


def _piece_copy(size, src_ref, src_unit, dst_ref, dst_unit, sem):
    return pltpu.make_async_copy(src_ref.at[pl.ds(src_unit, size)], dst_ref.at[pl.ds(dst_unit, size)], sem)


def _max_pieces(size):
    return LOCAL_ROWS // (ROW_UNIT * size) if size == PIECE_SIZES[0] else N_EXPERTS


def _start_run_copies(tabs, tile, loc_ref, glob_ref, sem, to_global):
    for ci, size in enumerate(PIECE_SIZES):
        loc_tab, glob_tab, n_tab = tabs[3 * ci:3 * ci + 3]
        base = tile * _max_pieces(size)

        def start(q, size=size, loc_tab=loc_tab, glob_tab=glob_tab):
            loc, glob = loc_tab[q], glob_tab[q]
            src, dst = ((loc_ref, loc), (glob_ref, glob)) if to_global else ((glob_ref, glob), (loc_ref, loc))
            _piece_copy(size, src[0], src[1], dst[0], dst[1], sem).start()

        n = n_tab[tile]
        if ci == 0:
            def issue_two(j, carry, start=start, base=base):
                start(base + 2 * j)
                start(base + 2 * j + 1)
                return carry

            lax.fori_loop(0, n // 2, issue_two, 0)
            pl.when(n % 2 == 1)(functools.partial(start, base + n - 1))
        else:
            def issue_one(j, carry, start=start, base=base):
                start(base + j)
                return carry

            lax.fori_loop(0, n, issue_one, 0)


def _wait_unit_copies(nun, src_ref, dst_ref, sem):
    def drain_group(j, carry):
        pltpu.make_async_copy(src_ref.at[pl.ds(0, WAIT_GROUP)], dst_ref.at[pl.ds(0, WAIT_GROUP)], sem).wait()
        return carry

    def drain_unit(j, carry):
        _unit_copy(src_ref, 0, dst_ref, 0, sem).wait()
        return carry

    lax.fori_loop(0, nun // WAIT_GROUP, drain_group, 0)
    lax.fori_loop(0, nun % WAIT_GROUP, drain_unit, 0)


def _dispatch_body(tabs, nun_ref, slot_ref, h2_ref, xs_ref, xloc, sem):
    i = pl.program_id(0)
    last = pl.num_programs(0) - 1
    tb = slot_ref.shape[1]
    nun = nun_ref[i]
    s = slot_ref[...]
    h2 = h2_ref[...]
    buf = i % 2

    def sort_chunk(c):
        r = lax.broadcasted_iota(jnp.int32, (SORT_CHUNK, tb), 0) + c * SORT_CHUNK
        hit = (s[0:1] == r) | (s[1:2] == r) | (s[2:3] == r) | (s[3:4] == r)
        p = jnp.where(hit, 1.0, 0.0).astype(BF16)
        units = SORT_CHUNK // ROW_UNIT
        xloc[buf, c * units:(c + 1) * units] = _dot(p, h2).astype(BF16).reshape(units, ROW_UNIT, h2.shape[1])

    typical_rows = TOP_K * tb + N_EXPERTS * ROW_UNIT // 2
    for c in range(LOCAL_ROWS // SORT_CHUNK):
        if (c + 1) * SORT_CHUNK <= typical_rows:
            sort_chunk(c)
        else:
            pl.when(c * (SORT_CHUNK // ROW_UNIT) < nun)(functools.partial(sort_chunk, c))

    _start_run_copies(tabs, i, xloc.at[buf], xs_ref, sem.at[buf], to_global=True)

    @pl.when(i > 0)
    def _():
        _wait_unit_copies(nun_ref[i - 1], xloc.at[1 - buf], xs_ref, sem.at[1 - buf])

    @pl.when(i == last)
    def _():
        _wait_unit_copies(nun, xloc.at[buf], xs_ref, sem.at[buf])


def _dispatch_first_kernel(*refs):
    tabs, (nun_ref, slot_ref, h2_ref, xs_ref, xloc, sem) = refs[:N_RUN_TABLES], refs[N_RUN_TABLES:]
    _dispatch_body(tabs, nun_ref, slot_ref, h2_ref, xs_ref, xloc, sem)


def _dispatch_next_kernel(*refs):
    tabs, (nun_ref, slot_ref, h2_ref, _, xs_ref, xloc, sem) = refs[:N_RUN_TABLES], refs[N_RUN_TABLES:]
    _dispatch_body(tabs, nun_ref, slot_ref, h2_ref, xs_ref, xloc, sem)


def _dispatch(tabs, nun, slot, h2, xs, n_rows, tb):
    T, d = h2.shape
    any_spec = pl.BlockSpec(memory_space=pl.ANY)
    in_specs = [pl.BlockSpec((TOP_K, tb), lambda i, *_: (0, i)),
                pl.BlockSpec((tb, d), lambda i, *_: (i, 0))]
    aliases = {}
    body = _dispatch_first_kernel
    args = tuple(tabs) + (nun, slot, h2)
    if xs is not None:
        in_specs.append(any_spec)
        aliases = {len(args): 0}
        body = _dispatch_next_kernel
        args = args + (xs,)
    return pl.pallas_call(
        body,
        out_shape=jax.ShapeDtypeStruct((n_rows // ROW_UNIT, ROW_UNIT, d), BF16),
        grid_spec=pltpu.PrefetchScalarGridSpec(
            num_scalar_prefetch=N_RUN_TABLES + 1, grid=(T // tb,), in_specs=in_specs, out_specs=any_spec,
            scratch_shapes=[pltpu.VMEM((2, LOCAL_ROWS // ROW_UNIT, ROW_UNIT, d), BF16),
                            pltpu.SemaphoreType.DMA((2,))]),
        input_output_aliases=aliases,
        compiler_params=pltpu.CompilerParams(dimension_semantics=("arbitrary",), has_side_effects=True,
                                             vmem_limit_bytes=VMEM_LIMIT),
        name="dispatch_next" if xs is not None else "dispatch_first",
    )(*args)


def _experts_kernel(te_ref, tv_ref, tf_ref, tn_ref, xs_ref, wg_hbm, bg_ref, wu_hbm, bu_ref, wd_hbm, bd_ref, y_ref,
                    stage_g, stage_u, stage_d, wgb, wub, wdb, sem):
    i = pl.program_id(0)
    valid = tv_ref[i]
    tm, d = xs_ref.shape
    ff = wgb.shape[1]

    def weight_copies(e):
        return [pltpu.make_async_copy(src.at[e], dst, sem.at[j])
                for j, (src, dst) in enumerate(((wg_hbm, stage_g), (wu_hbm, stage_u), (wd_hbm, stage_d)))]

    @pl.when(i == 0)
    def _():
        for cp in weight_copies(te_ref[0]):
            cp.start()

    @pl.when(tf_ref[i] == 1)
    def _():
        for cp in weight_copies(te_ref[i]):
            cp.wait()
        wgb[...] = stage_g[...].astype(BF16)
        wub[...] = stage_u[...].astype(BF16)
        wdb[...] = stage_d[...].astype(BF16)

        @pl.when(tn_ref[i] >= 0)
        def _():
            for cp in weight_copies(tn_ref[i]):
                cp.start()

    @pl.when(valid > 0)
    def _():
        row = lax.broadcasted_iota(jnp.int32, (tm, 1), 0)
        xb = jnp.where(row < valid, xs_ref[...], jnp.zeros((), BF16))
        acc = jnp.zeros((tm, d), F32)
        for c in range(ff // FF_CHUNK):
            cs = slice(c * FF_CHUNK, (c + 1) * FF_CHUNK)
            gate = jnp.minimum(_dot(xb, wgb[:, cs]) + bg_ref[0, :, cs], SWIGLU_LIMIT)
            up = jnp.clip(_dot(xb, wub[:, cs]) + bu_ref[0, :, cs], -SWIGLU_LIMIT, SWIGLU_LIMIT)
            act = (up + 1.0) * gate * jax.nn.sigmoid(SWIGLU_ALPHA * gate)
            acc = acc + _dot(act.astype(BF16), wdb[cs, :])
        y_ref[...] = (acc + bd_ref[0]).astype(BF16)

    @pl.when(valid == 0)
    def _():
        y_ref[...] = jnp.zeros_like(y_ref)


def _experts(tile_expert, tile_valid, tile_first, tile_next, xs, wg, bg, wu, bu, wd, bd, tmg):
    n_tiles = tile_expert.shape[0]
    _, d, ff = wg.shape
    bspec = lambda shp: pl.BlockSpec(shp, lambda i, te, *_: (te[i], 0, 0))
    any_spec = pl.BlockSpec(memory_space=pl.ANY)
    return pl.pallas_call(
        _experts_kernel,
        out_shape=jax.ShapeDtypeStruct(xs.shape, BF16),
        grid_spec=pltpu.PrefetchScalarGridSpec(
            num_scalar_prefetch=4,
            grid=(n_tiles,),
            in_specs=[
                pl.BlockSpec((tmg, d), lambda i, te, tv, *_: (jnp.where(tv[i] > 0, i, 0), 0)),
                any_spec, bspec((1, 1, ff)),
                any_spec, bspec((1, 1, ff)),
                any_spec, bspec((1, 1, d)),
            ],
            out_specs=pl.BlockSpec((tmg, d), lambda i, *_: (i, 0)),
            scratch_shapes=[pltpu.VMEM((d, ff), F32), pltpu.VMEM((d, ff), F32), pltpu.VMEM((ff, d), F32),
                            pltpu.VMEM((d, ff), BF16), pltpu.VMEM((d, ff), BF16), pltpu.VMEM((ff, d), BF16),
                            pltpu.SemaphoreType.DMA((3,))],
        ),
        compiler_params=pltpu.CompilerParams(dimension_semantics=("arbitrary",), vmem_limit_bytes=EXPERTS_VMEM_LIMIT),
        name="experts",
    )(tile_expert, tile_valid, tile_first, tile_next, xs, wg, bg, wu, bu, wd, bd)


def _combine_kernel(*refs):
    tabs = refs[:N_RUN_TABLES]
    nun_ref, y_ref, route_ref, xmid_ref, g2_ref, nf_ref, o_ref, ybuf, sem = refs[N_RUN_TABLES:]
    sb, rb, d = xmid_ref.shape
    tb = sb * rb
    i = pl.program_id(0) * pl.num_programs(1) + pl.program_id(1)
    n_steps = pl.num_programs(0) * pl.num_programs(1)
    buf = i % 2

    @pl.when(i == 0)
    def _():
        ybuf[...] = jnp.zeros_like(ybuf)
        _start_run_copies(tabs, 0, ybuf.at[0], y_ref, sem.at[0], to_global=False)

    _wait_unit_copies(nun_ref[i], y_ref, ybuf.at[buf], sem.at[buf])

    @pl.when(i + 1 < n_steps)
    def _():
        _start_run_copies(tabs, i + 1, ybuf.at[1 - buf], y_ref, sem.at[1 - buf], to_global=False)

    chunk_units = PERM_CHUNK // ROW_UNIT

    def sorted_rows(c):
        return ybuf[buf, c * chunk_units:(c + 1) * chunk_units].reshape(PERM_CHUNK, d)

    route = route_ref[...]
    lane = lax.broadcasted_iota(jnp.int32, (tb, PERM_CHUNK), 1).astype(F32)
    wk = [jnp.broadcast_to(route[:, k:k + 1], (tb, PERM_CHUNK)) for k in range(TOP_K)]
    sk = [jnp.broadcast_to(route[:, TOP_K + k:TOP_K + k + 1], (tb, PERM_CHUNK)) for k in range(TOP_K)]

    def weights_chunk(c):
        r = lane + float(c * PERM_CHUNK)
        pw = jnp.zeros((tb, PERM_CHUNK), F32)
        for k in range(TOP_K):
            pw = jnp.where(sk[k] == r, wk[k], pw)
        return pw.astype(BF16)

    def contribution(c):
        return _dot(weights_chunk(c), sorted_rows(c))

    moe = contribution(0)
    for c in range(1, LOCAL_ROWS // PERM_CHUNK):
        moe = moe + contribution(c)
    out = xmid_ref[...] + g2_ref[...] * moe.reshape(sb, rb, d)
    ms = jnp.mean(out * out, axis=-1, keepdims=True)
    o_ref[...] = out * lax.rsqrt(ms + EPS) * nf_ref[...]


def _combine(tabs, nun, y, route, xmid, g2, nf, sb, rb):
    nseq, L, d = xmid.shape
    tb = sb * rb
    nt = L // rb
    tok = lambda s, t, *_: (s * nt + t, 0)
    return pl.pallas_call(
        _combine_kernel,
        out_shape=jax.ShapeDtypeStruct((nseq, L, d), F32),
        grid_spec=pltpu.PrefetchScalarGridSpec(
            num_scalar_prefetch=N_RUN_TABLES + 1,
            grid=(nseq // sb, nt),
            in_specs=[
                pl.BlockSpec(memory_space=pl.ANY),
                pl.BlockSpec((tb, 2 * TOP_K), tok),
                pl.BlockSpec((sb, rb, d), lambda s, t, *_: (s, t, 0)),
                pl.BlockSpec((sb, 1, d), lambda s, t, *_: (s, 0, 0)),
                pl.BlockSpec((1, d), lambda s, t, *_: (0, 0)),
            ],
            out_specs=pl.BlockSpec((sb, rb, d), lambda s, t, *_: (s, t, 0)),
            scratch_shapes=[pltpu.VMEM((2, LOCAL_ROWS // ROW_UNIT, ROW_UNIT, d), BF16),
                            pltpu.SemaphoreType.DMA((2,))],
        ),
        compiler_params=_cparams("arbitrary", "arbitrary"),
        name="combine",
    )(*tabs, nun, y, route, xmid, g2, nf)


def _tile_rows(nseq, L, tile):
    if L >= tile:
        assert L % tile == 0
        return 1, tile
    assert tile % L == 0 and nseq % (tile // L) == 0
    return tile // L, L


def kernel(x_prompt, x_sample, state_gla, c_prompt, c_sample, w_ada, b_ada, norm1, w_in, w_gk, b_gk, gla_norm,
           gmlp_ln_g, gmlp_ln_b, gmlp_w_s, gmlp_b_s, w_out, norm2, w_router, b_router, w_gate, b_gate, w_up,
           b_up, w_down, b_down, norm_f):
    depth = w_ada.shape[0]
    assert depth == 1
    bp, lp, d = x_prompt.shape
    bs, ls, _ = x_sample.shape
    tp, ts = bp * lp, bs * ls

    nc = bp + bs
    ncp = -(-nc // SUBLANE) * SUBLANE
    c_all = jnp.concatenate([c_prompt, c_sample, jnp.zeros((ncp - nc, d), F32)], axis=0)
    mod = _ada(c_all, w_ada[0], b_ada[0][None]).reshape(ncp, N_MOD, 1, d)
    mods_p = [mod[:bp, i] for i in range(N_MOD)]
    mods_s = [mod[bp:nc, i] for i in range(N_MOD)]

    wi = w_in[0]
    c_lr = 2 * GLA_QK + GLA_WIDTH
    c_r = c_lr + GLA_LOWRANK
    wm = jnp.concatenate([wi[:, :c_lr], wi[:, c_r:]], axis=1).astype(BF16)
    wlr = jnp.pad(wi[:, c_lr:c_r], ((0, 0), (0, LANE - GLA_LOWRANK))).astype(BF16)
    wgk_f = jnp.pad(w_gk[0], ((0, LANE - GLA_LOWRANK), (0, 0)))
    wgk_hi = wgk_f.astype(BF16)
    wgk = jnp.concatenate([wgk_hi, (wgk_f - wgk_hi.astype(F32)).astype(BF16)], axis=1)
    bgk = b_gk[0][None]
    n1, n2, nf = norm1[0][None], norm2[0][None], norm_f[None]
    gn, lng, lnb = gla_norm[0][None], gmlp_ln_g[0][None], gmlp_ln_b[0][None]
    ws, bsv = gmlp_w_s[0], gmlp_b_s[0]
    pos_i = jnp.arange(GMLP_BLOCK)
    cmask = (pos_i[None, :] // CHUNK) <= (pos_i[:, None] // CHUNK)
    wm_p = jnp.where(cmask[None], ws, 0.0).astype(BF16)
    bsb_p = jnp.repeat(bsv.T, GMLP_GC, axis=1)
    reps = ROW_BLOCK // ls
    eye = jnp.eye(reps, dtype=F32)
    wm_s = jnp.einsum("ab,gij->gaibj", eye, ws[:, :ls, :ls]).reshape(GMLP_GROUPS, ROW_BLOCK, ROW_BLOCK).astype(BF16)
    bsb_s = jnp.tile(jnp.repeat(bsv[:, :ls].T, GMLP_GC, axis=1), (reps, 1))
    wo = w_out[0].astype(BF16)
    wr_f = jnp.pad(w_router[0], ((0, 0), (0, LANE - N_EXPERTS)))
    wr_hi = wr_f.astype(BF16)
    wr = jnp.concatenate([wr_hi, (wr_f - wr_hi.astype(F32)).astype(BF16)], axis=1)
    br = jnp.concatenate([b_router[0], jnp.full((LANE - N_EXPERTS,), -1e30, F32)])[None]
    upper = (jnp.arange(TOKEN_TILE)[:, None] < jnp.arange(TOKEN_TILE)[None, :]).astype(BF16)
    lower = (jnp.arange(N_EXPERTS)[None, :] < jnp.arange(N_EXPERTS)[:, None]).astype(F32)
    wg, wu, wd = w_gate[0], w_up[0], w_down[0]
    bg, bu, bd = b_gate[0][:, None], b_up[0][:, None], b_down[0][:, None]

    sbp, rbp = _tile_rows(bp, lp, TOKEN_TILE)
    sbs, rbs = _tile_rows(bs, ls, TOKEN_TILE)

    assert lp % TOKEN_TILE == 0
    mix_p, state_p = _mixer_prompt(x_prompt, mods_p[1], mods_p[0], n1, wm, wlr, wgk, bgk, gn, lng, lnb, wm_p, bsb_p,
                                   TOKEN_TILE)
    proj_s, gk_s = _inproj(x_sample, mods_s[1], mods_s[0], n1, wm, wlr, wgk, bgk, sbs, rbs)
    mix_s, state_s, vn_s = _mixer_sample(proj_s, gk_s, state_gla[0], gn, lng, lnb, wm_s, bsb_s)

    xmid_p, h2_p, route_p, slot_p, n16_p = _outproj(
        mix_p, x_prompt, mods_p[2], mods_p[4], mods_p[3], n2, wo, wr, br, upper, lower, sbp, rbp)
    xmid_s, h2_s, route_s, slot_s, n16_s = _outproj(
        mix_s, x_sample, mods_s[2], mods_s[4], mods_s[3], n2, wo, wr, br, upper, lower, sbs, rbs)

    tmg = EXPERT_TILE
    ntp = tp // TOKEN_TILE
    eids = jnp.arange(N_EXPERTS, dtype=jnp.int32)
    n16 = jnp.concatenate([n16_p[:, :, 0], n16_s[:, :, 0]], axis=0).astype(jnp.int32)
    nt_all = n16.shape[0]
    earlier = jnp.cumsum(n16, axis=0) - n16
    tot = jnp.sum(n16, axis=0)
    tiles_e = (tot + tmg - 1) // tmg
    tile_end = jnp.cumsum(tiles_e)
    tile_start = tile_end - tiles_e
    row_start = tile_start * tmg
    n_tiles = (TOP_K * (tp + ts) + nt_all * N_EXPERTS * (ROW_UNIT - 1)) // tmg + N_EXPERTS
    tid = jnp.arange(n_tiles, dtype=jnp.int32)
    te = jnp.minimum(jnp.sum((tid[:, None] >= tile_end[None, :]).astype(jnp.int32), axis=1), N_EXPERTS - 1)
    te_hot = te[:, None] == eids[None, :]
    tot_te = jnp.sum(jnp.where(te_hot, tot[None, :], 0), axis=1)
    start_te = jnp.sum(jnp.where(te_hot, tile_start[None, :], 0), axis=1)
    active = tid < tile_end[-1]
    tv = jnp.where(active, jnp.clip(tot_te - (tid - start_te) * tmg, 0, tmg), 0).astype(jnp.int32)
    last_e = jnp.max(jnp.where(tiles_e > 0, eids, 0)).astype(jnp.int32)
    te = jnp.where(active, te, last_e).astype(jnp.int32)
    te_prev = jnp.concatenate([jnp.full((1,), -1, jnp.int32), te[:-1]])
    tf = (active & (te != te_prev)).astype(jnp.int32)
    later_used = (eids[None, :] > te[:, None]) & (tiles_e[None, :] > 0)
    tn = jnp.min(jnp.where(later_used, eids[None, :], N_EXPERTS), axis=1)
    tn = jnp.where(tn < N_EXPERTS, tn, -1).astype(jnp.int32)

    run_units = n16 // ROW_UNIT
    run_loc = (jnp.cumsum(n16, axis=1) - n16) // ROW_UNIT
    run_glob = (row_start[None, :] + earlier) // ROW_UNIT
    nun = jnp.sum(run_units, axis=1).astype(jnp.int32)

    def piece_tables(cnt, loc0, glob0, stride, max_pieces):
        end = jnp.cumsum(cnt, axis=1)
        start = end - cnt
        piece = jnp.arange(max_pieces, dtype=jnp.int32)
        run = jnp.sum((piece[None, :, None] >= end[:, None, :]).astype(jnp.int32), axis=2)
        hot = run[:, :, None] == eids[None, None, :]
        pick = lambda a: jnp.sum(jnp.where(hot, a[:, None, :], 0), axis=2)
        within = (piece[None, :] - pick(start)) * stride
        live = piece[None, :] < end[:, -1:]
        loc = jnp.where(live, pick(loc0) + within, 0).astype(jnp.int32)
        glob = jnp.where(live, pick(glob0) + within, 0).astype(jnp.int32)
        return loc, glob, end[:, -1].astype(jnp.int32)

    tables = []
    done = jnp.zeros_like(run_units)
    for size in PIECE_SIZES:
        cnt = (run_units - done) // size
        tables.append(piece_tables(cnt, run_loc + done, run_glob + done, size, _max_pieces(size)))
        done = done + cnt * size

    def tables_for(tiles):
        return tuple(a[tiles].reshape(-1) for tab in tables for a in tab)

    tabs_p, tabs_s = tables_for(slice(0, ntp)), tables_for(slice(ntp, nt_all))

    n_rows = n_tiles * tmg
    xs = _dispatch(tabs_p, nun[:ntp], slot_p, h2_p, None, n_rows, TOKEN_TILE)
    xs = _dispatch(tabs_s, nun[ntp:], slot_s, h2_s, xs, n_rows, TOKEN_TILE)
    y = _experts(te, tv, tf, tn, xs.reshape(n_rows, d), wg, bg, wu, bu, wd, bd, tmg)
    y = y.reshape(n_rows // ROW_UNIT, ROW_UNIT, d)
    y_prompt = _combine(tabs_p, nun[:ntp], y, route_p, xmid_p, mods_p[5], nf, sbp, rbp)
    y_sample = _combine(tabs_s, nun[ntp:], y, route_s, xmid_s, mods_s[5], nf, sbs, rbs)

    return (y_prompt, y_sample, state_p[None], state_s[None], vn_s[None])
```

```python
import functools

import jax
import jax.numpy as jnp
from jax import lax
from jax.experimental import pallas as pl
from jax.experimental.pallas import tpu as pltpu

F32 = jnp.float32
BF16 = jnp.bfloat16
HIGHEST = lax.Precision.HIGHEST

CHUNK = 64
GLA_HEADS = 4
GLA_DK = 64
GLA_DV = 128
GLA_QK = GLA_HEADS * GLA_DK
GLA_WIDTH = GLA_HEADS * GLA_DV
GLA_LOWRANK = 16
GLA_GATE_NORM = 16.0
GATE_SAFE_MIN = -60.0
GMLP_WIDTH = 512
GMLP_GROUPS = 4
GMLP_GC = GMLP_WIDTH // GMLP_GROUPS
GMLP_BLOCK = 128
N_EXPERTS = 32
TOP_K = 4
SWIGLU_LIMIT = 7.0
SWIGLU_ALPHA = 1.702
EPS = 1e-6
N_MOD = 6

LANE = 128
SUBLANE = 8

Q0 = 0
K0 = Q0 + GLA_QK
V0 = K0 + GLA_QK
R0 = V0 + GLA_WIDTH
U0 = R0 + GLA_WIDTH
G0 = U0 + GMLP_WIDTH
PROJ_COLS = G0 + GMLP_WIDTH

ROW_BLOCK = 128
TOKEN_TILE = 512
EXPERT_TILE = 512
FF_CHUNK = 512
ROW_UNIT = 16
LOCAL_ROWS = 2560
WAIT_GROUP = 8
PIECE_SIZES = (4, 2, 1)
N_RUN_TABLES = 3 * len(PIECE_SIZES)
SORT_CHUNK = 128
PERM_CHUNK = 256
VMEM_LIMIT = 48 * 1024 * 1024
EXPERTS_VMEM_LIMIT = 56 * 1024 * 1024


def _cparams(*sem):
    return pltpu.CompilerParams(dimension_semantics=sem, vmem_limit_bytes=VMEM_LIMIT)


def _dot(a, b, **kw):
    return jnp.dot(a, b, preferred_element_type=F32, **kw)


def _dot_nt(a, b):
    return lax.dot_general(a, b, (((1,), (1,)), ((), ())), preferred_element_type=F32)


def _gelu(x):
    return 0.5 * x * (1.0 + lax.erf(x * (0.5 ** 0.5)))


def _dot_tn(a, b):
    return lax.dot_general(a, b, (((0,), (0,)), ((), ())), preferred_element_type=F32)


def _ada_kernel(c_ref, w_ref, b_ref, o_ref):
    c = c_ref[...]
    s = c * jax.nn.sigmoid(c)
    o_ref[...] = _dot(s, w_ref[...], precision=HIGHEST) + b_ref[...]


def _ada(c, w_ada, b_ada):
    n, d = c.shape
    return pl.pallas_call(
        _ada_kernel,
        out_shape=jax.ShapeDtypeStruct((n, N_MOD * d), F32),
        grid=(N_MOD,),
        in_specs=[
            pl.BlockSpec((n, d), lambda j: (0, 0)),
            pl.BlockSpec((d, d), lambda j: (0, j)),
            pl.BlockSpec((1, d), lambda j: (0, j)),
        ],
        out_specs=pl.BlockSpec((n, d), lambda j: (0, j)),
        compiler_params=_cparams("arbitrary"),
        name="ada",
    )(c, w_ada, b_ada)


PROJ_COL_CHUNK = 512


def _modulated_norm(x_ref, sc_ref, sh_ref, n1_ref):
    sb, rb, d = x_ref.shape
    x = x_ref[...]
    ms = jnp.mean(x * x, axis=-1, keepdims=True)
    h = x * lax.rsqrt(ms + EPS) * n1_ref[...]
    h = h * (1.0 + sc_ref[...]) + sh_ref[...]
    return h.reshape(sb * rb, d).astype(BF16)


def _gate_preact(hb, wlr_ref, wgk_ref, bgk_ref):
    lr = _dot(hb, wlr_ref[...])
    lr_hi = lr.astype(BF16)
    lr_lo = (lr - lr_hi.astype(F32)).astype(BF16)
    part = _dot(lr_hi, wgk_ref[...])
    return part[:, :GLA_QK] + part[:, GLA_QK:] + _dot(lr_lo, wgk_ref[:, :GLA_QK]) + bgk_ref[...]


def _inproj_kernel(x_ref, sc_ref, sh_ref, n1_ref, wm_ref, wlr_ref, wgk_ref, bgk_ref, proj_ref, gk_ref):
    sb, rb, _ = x_ref.shape
    hb = _modulated_norm(x_ref, sc_ref, sh_ref, n1_ref)
    cw = PROJ_COL_CHUNK
    for c in range(PROJ_COLS // cw):
        p = _dot(hb, wm_ref[:, c * cw:(c + 1) * cw])
        proj_ref[:, :, c * cw:(c + 1) * cw] = p.astype(BF16).reshape(sb, rb, cw)
    gk_ref[...] = _gate_preact(hb, wlr_ref, wgk_ref, bgk_ref).reshape(sb, rb, GLA_QK)


def _inproj(x, sc, sh, n1, wm, wlr, wgk, bgk, sb, rb):
    nseq, L, d = x.shape
    grid = (nseq // sb, L // rb)
    const = lambda s, t: (0, 0)
    return pl.pallas_call(
        _inproj_kernel,
        out_shape=(jax.ShapeDtypeStruct((nseq, L, PROJ_COLS), BF16),
                   jax.ShapeDtypeStruct((nseq, L, GLA_QK), F32)),
        grid=grid,
        in_specs=[
            pl.BlockSpec((sb, rb, d), lambda s, t: (s, t, 0)),
            pl.BlockSpec((sb, 1, d), lambda s, t: (s, 0, 0)),
            pl.BlockSpec((sb, 1, d), lambda s, t: (s, 0, 0)),
            pl.BlockSpec((1, d), const),
            pl.BlockSpec((d, PROJ_COLS), const),
            pl.BlockSpec((d, LANE), const),
            pl.BlockSpec((LANE, 2 * GLA_QK), const),
            pl.BlockSpec((1, GLA_QK), const),
        ],
        out_specs=(pl.BlockSpec((sb, rb, PROJ_COLS), lambda s, t: (s, t, 0)),
                   pl.BlockSpec((sb, rb, GLA_QK), lambda s, t: (s, t, 0))),
        compiler_params=_cparams("arbitrary", "arbitrary"),
        name="inproj",
    )(x, sc, sh, n1, wm, wlr, wgk, bgk)


def _head_masks():
    lane = lax.broadcasted_iota(jnp.int32, (1, GLA_QK), 1)
    return [(lane // GLA_DK) == h for h in range(GLA_HEADS)]


def _stack_heads(x, hm):
    return jnp.concatenate([jnp.where(m, x, 0.0) for m in hm], axis=0).astype(BF16)


def _chunk_pair_ids(c_len):
    ti = lax.broadcasted_iota(jnp.int32, (GLA_HEADS * c_len, c_len), 0) % c_len
    si = lax.broadcasted_iota(jnp.int32, (GLA_HEADS * c_len, c_len), 1)
    return ti, si


def _scores_factored(q4s, k, G, c_len):
    ti, si = _chunk_pair_ids(c_len)
    ke = (k * jnp.exp(-G)).astype(BF16)
    return jnp.concatenate(
        [jnp.where(si <= ti, _dot_nt(q4, ke[c * c_len:(c + 1) * c_len]), 0.0) for c, q4 in enumerate(q4s)], axis=0)


def _scores_bounded(qs, k, G, c_len, hm):
    ti, si = _chunk_pair_ids(c_len)
    t = lax.broadcasted_iota(jnp.int32, (c_len, 1), 0)
    col = lax.broadcasted_iota(jnp.int32, (c_len, c_len), 1)
    outs = []
    for c in range(qs.shape[0] // c_len):
        sl = slice(c * c_len, (c + 1) * c_len)
        q_c, k_c, g_c = qs[sl], k[sl], G[sl]
        a = jnp.where(si == ti, _dot_nt(_stack_heads(q_c, hm), k_c.astype(BF16)), 0.0)
        half = c_len // 2
        while half >= 1:
            blk = 2 * half
            sel = (col == (t // blk) * blk + (half - 1)).astype(F32)
            ref = _dot(sel, g_c, precision=HIGHEST)
            upper = (t % blk) >= half
            qh = jnp.where(upper, q_c * jnp.exp(jnp.minimum(g_c - ref, 0.0)), 0.0)
            kh = jnp.where(upper, 0.0, k_c * jnp.exp(jnp.minimum(ref - g_c, 0.0)))
            same = (ti // blk) == (si // blk)
            a = a + jnp.where(same, _dot_nt(_stack_heads(qh, hm), kh.astype(BF16)), 0.0)
            half //= 2
        outs.append(a)
    return jnp.concatenate(outs, axis=0)


def _cum_log_gates(gkpre, c_len):
    rows = gkpre.shape[0]
    g = jax.nn.log_sigmoid(gkpre) / GLA_GATE_NORM
    ri = lax.broadcasted_iota(jnp.int32, (rows, rows), 0)
    ci = lax.broadcasted_iota(jnp.int32, (rows, rows), 1)
    tri = jnp.where((ci <= ri) & ((ci // c_len) == (ri // c_len)), 1.0, 0.0).astype(BF16)
    g_hi = g.astype(BF16)
    r1 = g - g_hi.astype(F32)
    g_mid = r1.astype(BF16)
    g_lo = (r1 - g_mid.astype(F32)).astype(BF16)
    parts = _dot(tri, jnp.concatenate([g_hi, g_mid, g_lo], axis=1))
    return parts[:, :GLA_QK] + parts[:, GLA_QK:2 * GLA_QK] + parts[:, 2 * GLA_QK:]


def _factored_is_safe(G, c_len):
    ends = [G[(c + 1) * c_len - 1:(c + 1) * c_len] for c in range(G.shape[0] // c_len)]
    return jnp.min(jnp.concatenate(ends, axis=0)) > GATE_SAFE_MIN


def _mixer_block(p, G, states, c_len, gn, lng, lnb, wm_ref, bsb, bounded, side=()):
    rows = p.shape[0]
    n_chunks = rows // c_len
    chained = len(states) == 1
    hm = _head_masks()
    side = list(side)

    def side_step():
        if side:
            side.pop(0)()

    q = p[:, Q0:Q0 + GLA_QK].astype(F32)
    k = p[:, K0:K0 + GLA_QK].astype(F32)
    v = p[:, V0:V0 + GLA_WIDTH]
    r = p[:, R0:R0 + GLA_WIDTH].astype(F32)

    qs = q * (GLA_DK ** -0.5)
    qe = qs * jnp.exp(G)
    q4s = [_stack_heads(qe[c * c_len:(c + 1) * c_len], hm) for c in range(n_chunks)]
    scores = _scores_bounded(qs, k, G, c_len, hm) if bounded else _scores_factored(q4s, k, G, c_len)
    hc = GLA_HEADS * c_len

    new_states = []
    o_rows = []
    st = states[0]
    for c in range(n_chunks):
        lo, hi = c * c_len, (c + 1) * c_len
        if not chained:
            st = states[c]
        q4 = q4s[c]
        a = scores[c * hc:(c + 1) * hc].astype(BF16)
        o_inter = _dot_nt(q4, st.astype(BF16))
        v_c = v[lo:hi]
        heads = []
        for h in range(GLA_HEADS):
            o_h = o_inter[h * c_len:(h + 1) * c_len] + _dot(
                a[h * c_len:(h + 1) * c_len], v_c[:, h * GLA_DV:(h + 1) * GLA_DV])
            heads.append(o_h)
        o_rows.append(jnp.concatenate(heads, axis=1))
        g_last = G[hi - 1:hi]
        kd = (k[lo:hi] * jnp.exp(g_last - G[lo:hi])).astype(BF16)
        upd = _dot_tn(v_c, kd)
        st_new = jnp.exp(g_last) * st
        for h in range(GLA_HEADS):
            st_new = st_new + jnp.where(hm[h], upd[h * GLA_DV:(h + 1) * GLA_DV], 0.0)
        if chained:
            st = st_new
        else:
            new_states.append(st_new)
        side_step()
    if chained:
        new_states = [st]
    o = jnp.concatenate(o_rows, axis=0)

    gla = []
    for h in range(GLA_HEADS):
        o_h = o[:, h * GLA_DV:(h + 1) * GLA_DV]
        ms = jnp.mean(o_h * o_h, axis=-1, keepdims=True)
        r_h = r[:, h * GLA_DV:(h + 1) * GLA_DV]
        gla.append(o_h * lax.rsqrt(ms + EPS) * gn * (r_h * jax.nn.sigmoid(r_h)))
    side_step()

    u = _gelu(p[:, U0:U0 + GMLP_WIDTH].astype(F32))
    side_step()
    vv = _gelu(p[:, G0:G0 + GMLP_WIDTH].astype(F32))
    side_step()
    mu = jnp.mean(vv, axis=-1, keepdims=True)
    xc = vv - mu
    var = jnp.mean(xc * xc, axis=-1, keepdims=True)
    vn = xc * lax.rsqrt(var + EPS) * lng + lnb
    vnb = vn.astype(BF16)
    gm = []
    for gi in range(GMLP_GROUPS):
        sl = slice(gi * GMLP_GC, (gi + 1) * GMLP_GC)
        mixed = _dot(wm_ref[gi], vnb[:, sl]) + bsb[:, sl]
        gm.append(u[:, sl] * mixed)
    out = jnp.concatenate(gla + gm, axis=1)
    while side:
        side_step()
    return out, new_states, vn


def _mixer_prompt_kernel(x_ref, sc_ref, sh_ref, n1_ref, wm_ref, wlr_ref, wgk_ref, bgk_ref,
                         gn_ref, lng_ref, lnb_ref, ws_ref, bsb_ref, mix_ref, s_ref, st_scr):
    t = pl.program_id(1)
    nt = pl.num_programs(1)

    @pl.when(t == 0)
    def _():
        st_scr[...] = jnp.zeros_like(st_scr)

    tb = x_ref.shape[1]
    n_sub = tb // ROW_BLOCK
    cw = PROJ_COL_CHUNK
    hb = _modulated_norm(x_ref, sc_ref, sh_ref, n1_ref)

    def proj_steps(j, parts):
        hb_j = hb[j * ROW_BLOCK:(j + 1) * ROW_BLOCK]
        return [functools.partial(
            lambda c: parts.append(_dot(hb_j, wm_ref[:, c * cw:(c + 1) * cw]).astype(BF16)), c)
            for c in range(PROJ_COLS // cw)]

    gk = _gate_preact(hb, wlr_ref, wgk_ref, bgk_ref)
    first = []
    for step in proj_steps(0, first):
        step()
    Gs = [_cum_log_gates(gk[j * ROW_BLOCK:(j + 1) * ROW_BLOCK], CHUNK) for j in range(n_sub)]
    safe = _factored_is_safe(jnp.concatenate(Gs, axis=0), CHUNK)

    def run(bounded):
        st = st_scr[...]
        parts = first
        for j in range(n_sub):
            p_j = jnp.concatenate(parts, axis=1)
            parts = []
            side = proj_steps(j + 1, parts) if j + 1 < n_sub else []
            out, sts, _ = _mixer_block(p_j, Gs[j], [st], CHUNK, gn_ref[...], lng_ref[...], lnb_ref[...], ws_ref,
                                       bsb_ref[...], bounded, side)
            st = sts[0]
            mix_ref[0, j * ROW_BLOCK:(j + 1) * ROW_BLOCK, :] = out.astype(BF16)
        st_scr[...] = st

    pl.when(safe)(functools.partial(run, False))
    pl.when(jnp.logical_not(safe))(functools.partial(run, True))

    @pl.when(t == nt - 1)
    def _():
        s_ref[0] = st_scr[...].T.reshape(GLA_HEADS, GLA_DK, GLA_DV)


def _mixer_prompt(x, sc, sh, n1, wm, wlr, wgk, bgk, gn, lng, lnb, ws, bsb, tb):
    b, L, d = x.shape
    const2 = lambda s, t: (0, 0)
    return pl.pallas_call(
        _mixer_prompt_kernel,
        out_shape=(jax.ShapeDtypeStruct((b, L, 2 * GLA_WIDTH), BF16),
                   jax.ShapeDtypeStruct((b, GLA_HEADS, GLA_DK, GLA_DV), F32)),
        grid=(b, L // tb),
        in_specs=[
            pl.BlockSpec((1, tb, d), lambda s, t: (s, t, 0)),
            pl.BlockSpec((1, 1, d), lambda s, t: (s, 0, 0)),
            pl.BlockSpec((1, 1, d), lambda s, t: (s, 0, 0)),
            pl.BlockSpec((1, d), const2),
            pl.BlockSpec((d, PROJ_COLS), const2),
            pl.BlockSpec((d, LANE), const2),
            pl.BlockSpec((LANE, 2 * GLA_QK), const2),
            pl.BlockSpec((1, GLA_QK), const2),
            pl.BlockSpec((1, GLA_DV), const2),
            pl.BlockSpec((1, GMLP_WIDTH), const2),
            pl.BlockSpec((1, GMLP_WIDTH), const2),
            pl.BlockSpec((GMLP_GROUPS, ROW_BLOCK, ROW_BLOCK), lambda s, t: (0, 0, 0)),
            pl.BlockSpec((ROW_BLOCK, GMLP_WIDTH), const2),
        ],
        out_specs=(pl.BlockSpec((1, tb, 2 * GLA_WIDTH), lambda s, t: (s, t, 0)),
                   pl.BlockSpec((1, GLA_HEADS, GLA_DK, GLA_DV), lambda s, t: (s, 0, 0, 0))),
        scratch_shapes=[pltpu.VMEM((GLA_DV, GLA_QK), F32)],
        compiler_params=_cparams("arbitrary", "arbitrary"),
        name="mixer_prompt",
    )(x, sc, sh, n1, wm, wlr, wgk, bgk, gn, lng, lnb, ws, bsb)


def _mixer_sample_kernel(proj_ref, gk_ref, s0_ref, gn_ref, lng_ref, lnb_ref, wm_ref, bsb_ref,
                         mix_ref, s_ref, vn_ref):
    sb, rb, _ = proj_ref.shape
    G = _cum_log_gates(gk_ref[...].reshape(sb * rb, GLA_QK), rb)
    safe = _factored_is_safe(G, rb)

    def run(bounded):
        p = proj_ref[...].reshape(sb * rb, PROJ_COLS)
        states = [s0_ref[i].reshape(GLA_QK, GLA_DV).T for i in range(sb)]
        out, sts, vn = _mixer_block(p, G, states, rb, gn_ref[...], lng_ref[...], lnb_ref[...], wm_ref,
                                    bsb_ref[...], bounded)
        mix_ref[...] = out.astype(BF16).reshape(sb, rb, 2 * GLA_WIDTH)
        vn_ref[...] = vn.reshape(sb, rb, GMLP_WIDTH)
        for i in range(sb):
            s_ref[i] = sts[i].T.reshape(GLA_HEADS, GLA_DK, GLA_DV)

    pl.when(safe)(functools.partial(run, False))
    pl.when(jnp.logical_not(safe))(functools.partial(run, True))


def _mixer_sample(proj, gk, s0, gn, lng, lnb, wm, bsb):
    n, L, _ = proj.shape
    sb = ROW_BLOCK // L
    const2 = lambda s: (0, 0)
    return pl.pallas_call(
        _mixer_sample_kernel,
        out_shape=(jax.ShapeDtypeStruct((n, L, 2 * GLA_WIDTH), BF16),
                   jax.ShapeDtypeStruct((n, GLA_HEADS, GLA_DK, GLA_DV), F32),
                   jax.ShapeDtypeStruct((n, L, GMLP_WIDTH), F32)),
        grid=(n // sb,),
        in_specs=[
            pl.BlockSpec((sb, L, PROJ_COLS), lambda s: (s, 0, 0)),
            pl.BlockSpec((sb, L, GLA_QK), lambda s: (s, 0, 0)),
            pl.BlockSpec((sb, GLA_HEADS, GLA_DK, GLA_DV), lambda s: (s, 0, 0, 0)),
            pl.BlockSpec((1, GLA_DV), const2),
            pl.BlockSpec((1, GMLP_WIDTH), const2),
            pl.BlockSpec((1, GMLP_WIDTH), const2),
            pl.BlockSpec((GMLP_GROUPS, ROW_BLOCK, ROW_BLOCK), lambda s: (0, 0, 0)),
            pl.BlockSpec((ROW_BLOCK, GMLP_WIDTH), const2),
        ],
        out_specs=(pl.BlockSpec((sb, L, 2 * GLA_WIDTH), lambda s: (s, 0, 0)),
                   pl.BlockSpec((sb, GLA_HEADS, GLA_DK, GLA_DV), lambda s: (s, 0, 0, 0)),
                   pl.BlockSpec((sb, L, GMLP_WIDTH), lambda s: (s, 0, 0))),
        compiler_params=_cparams("arbitrary"),
        name="mixer_sample",
    )(proj, gk, s0, gn, lng, lnb, wm, bsb)


def _outproj_kernel(mix_ref, x_ref, g1_ref, sc_ref, sh_ref, n2_ref, wo_ref, wr_ref, br_ref, upper_ref, lower_ref,
                    xmid_ref, h2_ref, route_ref, slot_ref, n16_ref):
    sb, rb, d = x_ref.shape
    tm = sb * rb

    y = _dot(mix_ref[...].reshape(tm, d), wo_ref[...])
    xm = x_ref[...] + g1_ref[...] * y.reshape(sb, rb, d)
    xmid_ref[...] = xm
    ms = jnp.mean(xm * xm, axis=-1, keepdims=True)
    h2 = xm * lax.rsqrt(ms + EPS) * n2_ref[...]
    h2 = (h2 * (1.0 + sc_ref[...]) + sh_ref[...]).reshape(tm, d)
    h_hi = h2.astype(BF16)
    h2_ref[...] = h_hi

    h_lo = (h2 - h_hi.astype(F32)).astype(BF16)
    part = _dot(h_hi, wr_ref[...])
    logits = part[:, :LANE] + part[:, LANE:] + _dot(h_lo, wr_ref[:, :LANE]) + br_ref[...]
    l = logits.T[:N_EXPERTS]
    eid = lax.broadcasted_iota(jnp.int32, (N_EXPERTS, tm), 0).astype(F32)
    top_l, sel = [], []
    for _ in range(TOP_K):
        m = jnp.max(l, axis=0, keepdims=True)
        idx = jnp.min(jnp.where(l == m, eid, float(N_EXPERTS)), axis=0, keepdims=True)
        hit = eid == idx
        top_l.append(m)
        sel.append(hit)
        l = jnp.where(hit, -jnp.inf, l)
    ex = [jnp.exp(t - top_l[0]) for t in top_l]
    den = ex[0] + ex[1] + ex[2] + ex[3]
    top_w = [e / den for e in ex]

    chosen = (sel[0] | sel[1] | sel[2] | sel[3])
    cb = jnp.where(chosen, 1.0, 0.0)
    before = _dot(cb.astype(BF16), upper_ref[...])
    n = jnp.sum(cb, axis=1, keepdims=True)
    n16 = jnp.floor((n + (ROW_UNIT - 1)) * (1.0 / ROW_UNIT)) * ROW_UNIT
    n16b = jnp.broadcast_to(n16, (N_EXPERTS, LANE))
    ls = _dot(lower_ref[...], n16b, precision=HIGHEST)[:, 0:1]
    base = before + ls
    slots = [jnp.sum(jnp.where(s, base, 0.0), axis=0, keepdims=True) for s in sel]
    slot_ref[...] = jnp.concatenate(slots, axis=0).astype(jnp.int32)
    n16_ref[0] = n16b
    rows = jnp.concatenate(top_w + slots + [jnp.zeros((LANE - 2 * TOP_K, tm), F32)], axis=0)
    route_ref[...] = rows.T[:, :2 * TOP_K]


def _outproj(mix, x, g1, sc2, sh2, n2, wo, wr, br, upper, lower, sb, rb):
    nseq, L, d = x.shape
    tm = sb * rb
    nt = L // rb
    T = nseq * L
    grid = (nseq // sb, nt)
    const = lambda s, t: (0, 0)
    tok = lambda s, t: (0, s * nt + t)
    return pl.pallas_call(
        _outproj_kernel,
        out_shape=(jax.ShapeDtypeStruct((nseq, L, d), F32),
                   jax.ShapeDtypeStruct((T, d), BF16),
                   jax.ShapeDtypeStruct((T, 2 * TOP_K), F32),
                   jax.ShapeDtypeStruct((TOP_K, T), jnp.int32),
                   jax.ShapeDtypeStruct((T // tm, N_EXPERTS, LANE), F32)),
        grid=grid,
        in_specs=[
            pl.BlockSpec((sb, rb, d), lambda s, t: (s, t, 0)),
            pl.BlockSpec((sb, rb, d), lambda s, t: (s, t, 0)),
            pl.BlockSpec((sb, 1, d), lambda s, t: (s, 0, 0)),
            pl.BlockSpec((sb, 1, d), lambda s, t: (s, 0, 0)),
            pl.BlockSpec((sb, 1, d), lambda s, t: (s, 0, 0)),
            pl.BlockSpec((1, d), const),
            pl.BlockSpec((d, d), const),
            pl.BlockSpec((d, 2 * LANE), const),
            pl.BlockSpec((1, LANE), const),
            pl.BlockSpec((tm, tm), const),
            pl.BlockSpec((N_EXPERTS, N_EXPERTS), const),
        ],
        out_specs=(pl.BlockSpec((sb, rb, d), lambda s, t: (s, t, 0)),
                   pl.BlockSpec((tm, d), lambda s, t: (s * nt + t, 0)),
                   pl.BlockSpec((tm, 2 * TOP_K), lambda s, t: (s * nt + t, 0)),
                   pl.BlockSpec((TOP_K, tm), tok),
                   pl.BlockSpec((1, N_EXPERTS, LANE), lambda s, t: (s * nt + t, 0, 0))),
        compiler_params=_cparams("arbitrary", "arbitrary"),
        name="outproj",
    )(mix, x, g1, sc2, sh2, n2, wo, wr, br, upper, lower)


def _unit_copy(src_ref, src_unit, dst_ref, dst_unit, sem):
    return pltpu.make_async_copy(src_ref.at[src_unit], dst_ref.at[dst_unit], sem)


def _piece_copy(size, src_ref, src_unit, dst_ref, dst_unit, sem):
    return pltpu.make_async_copy(src_ref.at[pl.ds(src_unit, size)], dst_ref.at[pl.ds(dst_unit, size)], sem)


def _max_pieces(size):
    return LOCAL_ROWS // (ROW_UNIT * size) if size == PIECE_SIZES[0] else N_EXPERTS


def _start_run_copies(tabs, tile, loc_ref, glob_ref, sem, to_global):
    for ci, size in enumerate(PIECE_SIZES):
        loc_tab, glob_tab, n_tab = tabs[3 * ci:3 * ci + 3]
        base = tile * _max_pieces(size)

        def start(q, size=size, loc_tab=loc_tab, glob_tab=glob_tab):
            loc, glob = loc_tab[q], glob_tab[q]
            src, dst = ((loc_ref, loc), (glob_ref, glob)) if to_global else ((glob_ref, glob), (loc_ref, loc))
            _piece_copy(size, src[0], src[1], dst[0], dst[1], sem).start()

        n = n_tab[tile]
        if ci == 0:
            def issue_two(j, carry, start=start, base=base):
                start(base + 2 * j)
                start(base + 2 * j + 1)
                return carry

            lax.fori_loop(0, n // 2, issue_two, 0)
            pl.when(n % 2 == 1)(functools.partial(start, base + n - 1))
        else:
            def issue_one(j, carry, start=start, base=base):
                start(base + j)
                return carry

            lax.fori_loop(0, n, issue_one, 0)


def _wait_unit_copies(nun, src_ref, dst_ref, sem):
    def drain_group(j, carry):
        pltpu.make_async_copy(src_ref.at[pl.ds(0, WAIT_GROUP)], dst_ref.at[pl.ds(0, WAIT_GROUP)], sem).wait()
        return carry

    def drain_unit(j, carry):
        _unit_copy(src_ref, 0, dst_ref, 0, sem).wait()
        return carry

    lax.fori_loop(0, nun // WAIT_GROUP, drain_group, 0)
    lax.fori_loop(0, nun % WAIT_GROUP, drain_unit, 0)


def _dispatch_body(tabs, nun_ref, slot_ref, h2_ref, xs_ref, xloc, sem):
    i = pl.program_id(0)
    last = pl.num_programs(0) - 1
    tb = slot_ref.shape[1]
    nun = nun_ref[i]
    s = slot_ref[...]
    h2 = h2_ref[...]
    buf = i % 2

    def sort_chunk(c):
        r = lax.broadcasted_iota(jnp.int32, (SORT_CHUNK, tb), 0) + c * SORT_CHUNK
        p = jnp.zeros((SORT_CHUNK, tb), F32)
        for k in range(TOP_K):
            p = jnp.where(s[k:k + 1] == r, 1.0, p)
        p = p.astype(BF16)
        units = SORT_CHUNK // ROW_UNIT
        xloc[buf, c * units:(c + 1) * units] = _dot(p, h2).astype(BF16).reshape(units, ROW_UNIT, h2.shape[1])

    typical_rows = TOP_K * tb + N_EXPERTS * ROW_UNIT // 2
    for c in range(LOCAL_ROWS // SORT_CHUNK):
        if (c + 1) * SORT_CHUNK <= typical_rows:
            sort_chunk(c)
        else:
            pl.when(c * (SORT_CHUNK // ROW_UNIT) < nun)(functools.partial(sort_chunk, c))

    _start_run_copies(tabs, i, xloc.at[buf], xs_ref, sem.at[buf], to_global=True)

    @pl.when(i > 0)
    def _():
        _wait_unit_copies(nun_ref[i - 1], xloc.at[1 - buf], xs_ref, sem.at[1 - buf])

    @pl.when(i == last)
    def _():
        _wait_unit_copies(nun, xloc.at[buf], xs_ref, sem.at[buf])


def _dispatch_first_kernel(*refs):
    tabs, (nun_ref, slot_ref, h2_ref, xs_ref, xloc, sem) = refs[:N_RUN_TABLES], refs[N_RUN_TABLES:]
    _dispatch_body(tabs, nun_ref, slot_ref, h2_ref, xs_ref, xloc, sem)


def _dispatch_next_kernel(*refs):
    tabs, (nun_ref, slot_ref, h2_ref, _, xs_ref, xloc, sem) = refs[:N_RUN_TABLES], refs[N_RUN_TABLES:]
    _dispatch_body(tabs, nun_ref, slot_ref, h2_ref, xs_ref, xloc, sem)


def _dispatch(tabs, nun, slot, h2, xs, n_rows, tb):
    T, d = h2.shape
    any_spec = pl.BlockSpec(memory_space=pl.ANY)
    in_specs = [pl.BlockSpec((TOP_K, tb), lambda i, *_: (0, i)),
                pl.BlockSpec((tb, d), lambda i, *_: (i, 0))]
    aliases = {}
    body = _dispatch_first_kernel
    args = tuple(tabs) + (nun, slot, h2)
    if xs is not None:
        in_specs.append(any_spec)
        aliases = {len(args): 0}
        body = _dispatch_next_kernel
        args = args + (xs,)
    return pl.pallas_call(
        body,
        out_shape=jax.ShapeDtypeStruct((n_rows // ROW_UNIT, ROW_UNIT, d), BF16),
        grid_spec=pltpu.PrefetchScalarGridSpec(
            num_scalar_prefetch=N_RUN_TABLES + 1, grid=(T // tb,), in_specs=in_specs, out_specs=any_spec,
            scratch_shapes=[pltpu.VMEM((2, LOCAL_ROWS // ROW_UNIT, ROW_UNIT, d), BF16),
                            pltpu.SemaphoreType.DMA((2,))]),
        input_output_aliases=aliases,
        compiler_params=pltpu.CompilerParams(dimension_semantics=("arbitrary",), has_side_effects=True,
                                             vmem_limit_bytes=VMEM_LIMIT),
        name="dispatch_next" if xs is not None else "dispatch_first",
    )(*args)


def _experts_kernel(te_ref, tv_ref, tf_ref, tn_ref, xs_ref, wg_hbm, bg_ref, wu_hbm, bu_ref, wd_hbm, bd_ref, y_ref,
                    stage_g, stage_u, stage_d, wgb, wub, wdb, sem):
    i = pl.program_id(0)
    valid = tv_ref[i]
    tm, d = xs_ref.shape
    ff = wgb.shape[1]

    def weight_copies(e):
        return [pltpu.make_async_copy(src.at[e], dst, sem.at[j])
                for j, (src, dst) in enumerate(((wg_hbm, stage_g), (wu_hbm, stage_u), (wd_hbm, stage_d)))]

    @pl.when(i == 0)
    def _():
        for cp in weight_copies(te_ref[0]):
            cp.start()

    @pl.when(tf_ref[i] == 1)
    def _():
        for cp in weight_copies(te_ref[i]):
            cp.wait()
        wgb[...] = stage_g[...].astype(BF16)
        wub[...] = stage_u[...].astype(BF16)
        wdb[...] = stage_d[...].astype(BF16)

        @pl.when(tn_ref[i] >= 0)
        def _():
            for cp in weight_copies(tn_ref[i]):
                cp.start()

    @pl.when(valid > 0)
    def _():
        row = lax.broadcasted_iota(jnp.int32, (tm, 1), 0)
        xb = jnp.where(row < valid, xs_ref[...], jnp.zeros((), BF16))
        acc = jnp.zeros((tm, d), F32)
        for c in range(ff // FF_CHUNK):
            cs = slice(c * FF_CHUNK, (c + 1) * FF_CHUNK)
            gate = jnp.minimum(_dot(xb, wgb[:, cs]) + bg_ref[0, :, cs], SWIGLU_LIMIT)
            up = jnp.clip(_dot(xb, wub[:, cs]) + bu_ref[0, :, cs], -SWIGLU_LIMIT, SWIGLU_LIMIT)
            act = (up + 1.0) * gate * jax.nn.sigmoid(SWIGLU_ALPHA * gate)
            acc = acc + _dot(act.astype(BF16), wdb[cs, :])
        y_ref[...] = (acc + bd_ref[0]).astype(BF16)

    @pl.when(valid == 0)
    def _():
        y_ref[...] = jnp.zeros_like(y_ref)


def _experts(tile_expert, tile_valid, tile_first, tile_next, xs, wg, bg, wu, bu, wd, bd, tmg):
    n_tiles = tile_expert.shape[0]
    _, d, ff = wg.shape
    bspec = lambda shp: pl.BlockSpec(shp, lambda i, te, *_: (te[i], 0, 0))
    any_spec = pl.BlockSpec(memory_space=pl.ANY)
    return pl.pallas_call(
        _experts_kernel,
        out_shape=jax.ShapeDtypeStruct(xs.shape, BF16),
        grid_spec=pltpu.PrefetchScalarGridSpec(
            num_scalar_prefetch=4,
            grid=(n_tiles,),
            in_specs=[
                pl.BlockSpec((tmg, d), lambda i, te, tv, *_: (jnp.where(tv[i] > 0, i, 0), 0)),
                any_spec, bspec((1, 1, ff)),
                any_spec, bspec((1, 1, ff)),
                any_spec, bspec((1, 1, d)),
            ],
            out_specs=pl.BlockSpec((tmg, d), lambda i, *_: (i, 0)),
            scratch_shapes=[pltpu.VMEM((d, ff), F32), pltpu.VMEM((d, ff), F32), pltpu.VMEM((ff, d), F32),
                            pltpu.VMEM((d, ff), BF16), pltpu.VMEM((d, ff), BF16), pltpu.VMEM((ff, d), BF16),
                            pltpu.SemaphoreType.DMA((3,))],
        ),
        compiler_params=pltpu.CompilerParams(dimension_semantics=("arbitrary",), vmem_limit_bytes=EXPERTS_VMEM_LIMIT),
        name="experts",
    )(tile_expert, tile_valid, tile_first, tile_next, xs, wg, bg, wu, bu, wd, bd)


def _combine_kernel(*refs):
    tabs = refs[:N_RUN_TABLES]
    nun_ref, y_ref, route_ref, xmid_ref, g2_ref, nf_ref, o_ref, ybuf, sem = refs[N_RUN_TABLES:]
    sb, rb, d = xmid_ref.shape
    tb = sb * rb
    i = pl.program_id(0) * pl.num_programs(1) + pl.program_id(1)
    n_steps = pl.num_programs(0) * pl.num_programs(1)
    buf = i % 2

    @pl.when(i == 0)
    def _():
        ybuf[...] = jnp.zeros_like(ybuf)
        _start_run_copies(tabs, 0, ybuf.at[0], y_ref, sem.at[0], to_global=False)

    _wait_unit_copies(nun_ref[i], y_ref, ybuf.at[buf], sem.at[buf])

    @pl.when(i + 1 < n_steps)
    def _():
        _start_run_copies(tabs, i + 1, ybuf.at[1 - buf], y_ref, sem.at[1 - buf], to_global=False)

    chunk_units = PERM_CHUNK // ROW_UNIT

    def sorted_rows(c):
        return ybuf[buf, c * chunk_units:(c + 1) * chunk_units].reshape(PERM_CHUNK, d)

    route = route_ref[...]
    lane = lax.broadcasted_iota(jnp.int32, (tb, PERM_CHUNK), 1).astype(F32)
    wk = [jnp.broadcast_to(route[:, k:k + 1], (tb, PERM_CHUNK)) for k in range(TOP_K)]
    sk = [jnp.broadcast_to(route[:, TOP_K + k:TOP_K + k + 1], (tb, PERM_CHUNK)) for k in range(TOP_K)]

    def weights_chunk(c):
        r = lane + float(c * PERM_CHUNK)
        pw = jnp.zeros((tb, PERM_CHUNK), F32)
        for k in range(TOP_K):
            pw = jnp.where(sk[k] == r, wk[k], pw)
        return pw.astype(BF16)

    def contribution(c):
        return _dot(weights_chunk(c), sorted_rows(c))

    moe = contribution(0)
    for c in range(1, LOCAL_ROWS // PERM_CHUNK):
        moe = moe + contribution(c)
    out = xmid_ref[...] + g2_ref[...] * moe.reshape(sb, rb, d)
    ms = jnp.mean(out * out, axis=-1, keepdims=True)
    o_ref[...] = out * lax.rsqrt(ms + EPS) * nf_ref[...]


def _combine(tabs, nun, y, route, xmid, g2, nf, sb, rb):
    nseq, L, d = xmid.shape
    tb = sb * rb
    nt = L // rb
    tok = lambda s, t, *_: (s * nt + t, 0)
    return pl.pallas_call(
        _combine_kernel,
        out_shape=jax.ShapeDtypeStruct((nseq, L, d), F32),
        grid_spec=pltpu.PrefetchScalarGridSpec(
            num_scalar_prefetch=N_RUN_TABLES + 1,
            grid=(nseq // sb, nt),
            in_specs=[
                pl.BlockSpec(memory_space=pl.ANY),
                pl.BlockSpec((tb, 2 * TOP_K), tok),
                pl.BlockSpec((sb, rb, d), lambda s, t, *_: (s, t, 0)),
                pl.BlockSpec((sb, 1, d), lambda s, t, *_: (s, 0, 0)),
                pl.BlockSpec((1, d), lambda s, t, *_: (0, 0)),
            ],
            out_specs=pl.BlockSpec((sb, rb, d), lambda s, t, *_: (s, t, 0)),
            scratch_shapes=[pltpu.VMEM((2, LOCAL_ROWS // ROW_UNIT, ROW_UNIT, d), BF16),
                            pltpu.SemaphoreType.DMA((2,))],
        ),
        compiler_params=_cparams("arbitrary", "arbitrary"),
        name="combine",
    )(*tabs, nun, y, route, xmid, g2, nf)


def _tile_rows(nseq, L, tile):
    if L >= tile:
        assert L % tile == 0
        return 1, tile
    assert tile % L == 0 and nseq % (tile // L) == 0
    return tile // L, L


def kernel(x_prompt, x_sample, state_gla, c_prompt, c_sample, w_ada, b_ada, norm1, w_in, w_gk, b_gk, gla_norm,
           gmlp_ln_g, gmlp_ln_b, gmlp_w_s, gmlp_b_s, w_out, norm2, w_router, b_router, w_gate, b_gate, w_up,
           b_up, w_down, b_down, norm_f):
    depth = w_ada.shape[0]
    assert depth == 1
    bp, lp, d = x_prompt.shape
    bs, ls, _ = x_sample.shape
    tp, ts = bp * lp, bs * ls

    nc = bp + bs
    ncp = -(-nc // SUBLANE) * SUBLANE
    c_all = jnp.concatenate([c_prompt, c_sample, jnp.zeros((ncp - nc, d), F32)], axis=0)
    mod = _ada(c_all, w_ada[0], b_ada[0][None]).reshape(ncp, N_MOD, 1, d)
    mods_p = [mod[:bp, i] for i in range(N_MOD)]
    mods_s = [mod[bp:nc, i] for i in range(N_MOD)]

    wi = w_in[0]
    c_lr = 2 * GLA_QK + GLA_WIDTH
    c_r = c_lr + GLA_LOWRANK
    wm = jnp.concatenate([wi[:, :c_lr], wi[:, c_r:]], axis=1).astype(BF16)
    wlr = jnp.pad(wi[:, c_lr:c_r], ((0, 0), (0, LANE - GLA_LOWRANK))).astype(BF16)
    wgk_f = jnp.pad(w_gk[0], ((0, LANE - GLA_LOWRANK), (0, 0)))
    wgk_hi = wgk_f.astype(BF16)
    wgk = jnp.concatenate([wgk_hi, (wgk_f - wgk_hi.astype(F32)).astype(BF16)], axis=1)
    bgk = b_gk[0][None]
    n1, n2, nf = norm1[0][None], norm2[0][None], norm_f[None]
    gn, lng, lnb = gla_norm[0][None], gmlp_ln_g[0][None], gmlp_ln_b[0][None]
    ws, bsv = gmlp_w_s[0], gmlp_b_s[0]
    pos_i = jnp.arange(GMLP_BLOCK)
    cmask = (pos_i[None, :] // CHUNK) <= (pos_i[:, None] // CHUNK)
    wm_p = jnp.where(cmask[None], ws, 0.0).astype(BF16)
    bsb_p = jnp.repeat(bsv.T, GMLP_GC, axis=1)
    reps = ROW_BLOCK // ls
    eye = jnp.eye(reps, dtype=F32)
    wm_s = jnp.einsum("ab,gij->gaibj", eye, ws[:, :ls, :ls]).reshape(GMLP_GROUPS, ROW_BLOCK, ROW_BLOCK).astype(BF16)
    bsb_s = jnp.tile(jnp.repeat(bsv[:, :ls].T, GMLP_GC, axis=1), (reps, 1))
    wo = w_out[0].astype(BF16)
    wr_f = jnp.pad(w_router[0], ((0, 0), (0, LANE - N_EXPERTS)))
    wr_hi = wr_f.astype(BF16)
    wr = jnp.concatenate([wr_hi, (wr_f - wr_hi.astype(F32)).astype(BF16)], axis=1)
    br = jnp.concatenate([b_router[0], jnp.full((LANE - N_EXPERTS,), -1e30, F32)])[None]
    upper = (jnp.arange(TOKEN_TILE)[:, None] < jnp.arange(TOKEN_TILE)[None, :]).astype(BF16)
    lower = (jnp.arange(N_EXPERTS)[None, :] < jnp.arange(N_EXPERTS)[:, None]).astype(F32)
    wg, wu, wd = w_gate[0], w_up[0], w_down[0]
    bg, bu, bd = b_gate[0][:, None], b_up[0][:, None], b_down[0][:, None]

    sbp, rbp = _tile_rows(bp, lp, TOKEN_TILE)
    sbs, rbs = _tile_rows(bs, ls, TOKEN_TILE)

    assert lp % TOKEN_TILE == 0
    mix_p, state_p = _mixer_prompt(x_prompt, mods_p[1], mods_p[0], n1, wm, wlr, wgk, bgk, gn, lng, lnb, wm_p, bsb_p,
                                   TOKEN_TILE)
    proj_s, gk_s = _inproj(x_sample, mods_s[1], mods_s[0], n1, wm, wlr, wgk, bgk, sbs, rbs)
    mix_s, state_s, vn_s = _mixer_sample(proj_s, gk_s, state_gla[0], gn, lng, lnb, wm_s, bsb_s)

    xmid_p, h2_p, route_p, slot_p, n16_p = _outproj(
        mix_p, x_prompt, mods_p[2], mods_p[4], mods_p[3], n2, wo, wr, br, upper, lower, sbp, rbp)
    xmid_s, h2_s, route_s, slot_s, n16_s = _outproj(
        mix_s, x_sample, mods_s[2], mods_s[4], mods_s[3], n2, wo, wr, br, upper, lower, sbs, rbs)

    tmg = EXPERT_TILE
    ntp = tp // TOKEN_TILE
    eids = jnp.arange(N_EXPERTS, dtype=jnp.int32)
    n16 = jnp.concatenate([n16_p[:, :, 0], n16_s[:, :, 0]], axis=0).astype(jnp.int32)
    nt_all = n16.shape[0]
    earlier = jnp.cumsum(n16, axis=0) - n16
    tot = jnp.sum(n16, axis=0)
    tiles_e = (tot + tmg - 1) // tmg
    tile_end = jnp.cumsum(tiles_e)
    tile_start = tile_end - tiles_e
    row_start = tile_start * tmg
    n_tiles = (TOP_K * (tp + ts) + nt_all * N_EXPERTS * (ROW_UNIT - 1)) // tmg + N_EXPERTS
    tid = jnp.arange(n_tiles, dtype=jnp.int32)
    te = jnp.minimum(jnp.sum((tid[:, None] >= tile_end[None, :]).astype(jnp.int32), axis=1), N_EXPERTS - 1)
    te_hot = te[:, None] == eids[None, :]
    tot_te = jnp.sum(jnp.where(te_hot, tot[None, :], 0), axis=1)
    start_te = jnp.sum(jnp.where(te_hot, tile_start[None, :], 0), axis=1)
    active = tid < tile_end[-1]
    tv = jnp.where(active, jnp.clip(tot_te - (tid - start_te) * tmg, 0, tmg), 0).astype(jnp.int32)
    last_e = jnp.max(jnp.where(tiles_e > 0, eids, 0)).astype(jnp.int32)
    te = jnp.where(active, te, last_e).astype(jnp.int32)
    te_prev = jnp.concatenate([jnp.full((1,), -1, jnp.int32), te[:-1]])
    tf = (active & (te != te_prev)).astype(jnp.int32)
    later_used = (eids[None, :] > te[:, None]) & (tiles_e[None, :] > 0)
    tn = jnp.min(jnp.where(later_used, eids[None, :], N_EXPERTS), axis=1)
    tn = jnp.where(tn < N_EXPERTS, tn, -1).astype(jnp.int32)

    run_units = n16 // ROW_UNIT
    run_loc = (jnp.cumsum(n16, axis=1) - n16) // ROW_UNIT
    run_glob = (row_start[None, :] + earlier) // ROW_UNIT
    nun = jnp.sum(run_units, axis=1).astype(jnp.int32)

    def piece_tables(cnt, loc0, glob0, stride, max_pieces):
        end = jnp.cumsum(cnt, axis=1)
        start = end - cnt
        piece = jnp.arange(max_pieces, dtype=jnp.int32)
        run = jnp.sum((piece[None, :, None] >= end[:, None, :]).astype(jnp.int32), axis=2)
        hot = run[:, :, None] == eids[None, None, :]
        pick = lambda a: jnp.sum(jnp.where(hot, a[:, None, :], 0), axis=2)
        within = (piece[None, :] - pick(start)) * stride
        live = piece[None, :] < end[:, -1:]
        loc = jnp.where(live, pick(loc0) + within, 0).astype(jnp.int32)
        glob = jnp.where(live, pick(glob0) + within, 0).astype(jnp.int32)
        return loc, glob, end[:, -1].astype(jnp.int32)

    tables = []
    done = jnp.zeros_like(run_units)
    for size in PIECE_SIZES:
        cnt = (run_units - done) // size
        tables.append(piece_tables(cnt, run_loc + done, run_glob + done, size, _max_pieces(size)))
        done = done + cnt * size

    def tables_for(tiles):
        return tuple(a[tiles].reshape(-1) for tab in tables for a in tab)

    tabs_p, tabs_s = tables_for(slice(0, ntp)), tables_for(slice(ntp, nt_all))

    n_rows = n_tiles * tmg
    xs = _dispatch(tabs_p, nun[:ntp], slot_p, h2_p, None, n_rows, TOKEN_TILE)
    xs = _dispatch(tabs_s, nun[ntp:], slot_s, h2_s, xs, n_rows, TOKEN_TILE)
    y = _experts(te, tv, tf, tn, xs.reshape(n_rows, d), wg, bg, wu, bu, wd, bd, tmg)
    y = y.reshape(n_rows // ROW_UNIT, ROW_UNIT, d)
    y_prompt = _combine(tabs_p, nun[:ntp], y, route_p, xmid_p, mods_p[5], nf, sbp, rbp)
    y_sample = _combine(tabs_s, nun[ntp:], y, route_s, xmid_s, mods_s[5], nf, sbs, rbs)

    return (y_prompt, y_sample, state_p[None], state_s[None], vn_s[None])
```
